```python
import math
import jax
import jax.numpy as jnp
from jax import lax
import numpy as np

D_MODEL = 1024
BATCH = 16
SEQ = 256
DEPTH = 2
DEC_BATCH = 8
DEC_SEQ = 4096
PAST_LEN = 512

GRID_W = 64
HEAD_DIM = 64
ATT_HEADS = 4
ATT_KV_HEADS = 2
WIN_HEADS = 4
WIN_KV_HEADS = 2
WINDOW = 128
Q_BLOCK = 128
ROPE_THETA = 10000.0
SSM_HEADS = 8
SSM_HEAD_DIM = 64
D_SSM = SSM_HEADS * SSM_HEAD_DIM
SSM_GROUPS = 2
SSM_STATE = 128
D_CONV = 5
SSM_CHUNK = 128
XBC_DIM = D_SSM + 2 * SSM_GROUPS * SSM_STATE
MIX_WIDTH = ATT_HEADS * HEAD_DIM + D_SSM + WIN_HEADS * HEAD_DIM
IN_SPLITS = (ATT_HEADS * HEAD_DIM, ATT_KV_HEADS * HEAD_DIM, ATT_KV_HEADS * HEAD_DIM,
             WIN_HEADS * HEAD_DIM, WIN_KV_HEADS * HEAD_DIM, WIN_KV_HEADS * HEAD_DIM,
             D_SSM, XBC_DIM, 2 * SSM_HEADS)
IN_WIDTH = sum(IN_SPLITS)
D_FF = 2816
N_EXPERTS = 8
TOP_K = 2
D_FF_EXPERT = 3584
MOE_BLOCK = 128
N_DENSE = (DEPTH + 1) // 2
N_MOE = DEPTH // 2
NORM_EPS = 1e-6

kernel_name = 'hybrid_dit_prefix_step'


def rmsnorm(x, g):
    xf = x.astype(jnp.float32)
    y = xf * lax.rsqrt(jnp.mean(xf * xf, axis=-1, keepdims=True) + NORM_EPS)
    return (y * g.astype(jnp.float32)).astype(x.dtype)


def modulate(x, shift, scale):
    return x * (1.0 + scale) + shift


def swiglu(x, w_gate, w_up, w_down):
    return (jax.nn.silu(x @ w_gate) * (x @ w_up)) @ w_down


def rope_tables(n_tok):
    n_rows = n_tok // GRID_W
    row = jnp.repeat(jnp.arange(n_rows, dtype=jnp.float32), GRID_W)
    col = (jnp.arange(n_tok) % GRID_W).astype(jnp.float32)
    nf = HEAD_DIM // 4
    inv = ROPE_THETA ** (-jnp.arange(nf, dtype=jnp.float32) / nf)
    ang = jnp.stack([row[:, None] * inv, col[:, None] * inv], axis=1)
    return jnp.cos(ang)[None, :, None], jnp.sin(ang)[None, :, None]


def apply_rope(x, cos, sin):
    b, n, h, hd = x.shape
    xr = x.astype(jnp.float32).reshape(b, n, h, 2, 2, hd // 4)
    x1, x2 = xr[..., 0, :], xr[..., 1, :]
    out = jnp.stack([x1 * cos - x2 * sin, x2 * cos + x1 * sin], axis=-2)
    return out.reshape(b, n, h, hd).astype(x.dtype)


def softmax_with_sink(s, sink):
    if sink is None:
        return jax.nn.softmax(s, axis=-1)
    snk = sink.astype(jnp.float32).reshape(1, s.shape[1], s.shape[2], 1, 1)
    m = jnp.maximum(jnp.max(s, axis=-1, keepdims=True), snk)
    e = jnp.exp(s - m)
    return e / (jnp.sum(e, axis=-1, keepdims=True) + jnp.exp(snk - m))


def block_attention(q, k, v, sink):
    b, lq, hq, hd = q.shape
    kv = k.shape[2]
    g = hq // kv
    nb = lq // Q_BLOCK
    qb = jnp.moveaxis(q.reshape(b, nb, Q_BLOCK, kv, g, hd), 1, 0)
    scale = hd ** -0.5

    def one_block(q_blk):
        s = jnp.einsum('bqkgd,bskd->bkgqs', q_blk, k).astype(jnp.float32) * scale
        p = softmax_with_sink(s, sink).astype(v.dtype)
        return jnp.einsum('bkgqs,bskd->bqkgd', p, v)

    out = lax.map(one_block, qb)
    return jnp.moveaxis(out, 0, 1).reshape(b, lq, hq, hd)


def window_attention(q, k, v, k_ctx, v_ctx, sink):
    b, n, hq, hd = q.shape
    kv = k.shape[2]
    g = hq // kv
    nb = n // Q_BLOCK
    band = Q_BLOCK + 2 * WINDOW
    pad = ((0, 0), (WINDOW, WINDOW), (0, 0), (0, 0))
    k_pad = jnp.pad(k, pad)
    v_pad = jnp.pad(v, pad)
    qb = jnp.moveaxis(q.reshape(b, nb, Q_BLOCK, kv, g, hd), 1, 0)
    n_ctx = k_ctx.shape[1]
    scale = hd ** -0.5

    def one_block(args):
        i, q_blk = args
        start = i * Q_BLOCK
        k_band = lax.dynamic_slice_in_dim(k_pad, start, band, axis=1)
        v_band = lax.dynamic_slice_in_dim(v_pad, start, band, axis=1)
        q_pos = start + jnp.arange(Q_BLOCK)
        k_pos = start - WINDOW + jnp.arange(band)
        valid = ((jnp.abs(q_pos[:, None] - k_pos[None, :]) <= WINDOW)
                 & (k_pos >= 0)[None, :] & (k_pos < n)[None, :])
        s_ctx = jnp.einsum('bqkgd,bskd->bkgqs', q_blk, k_ctx).astype(jnp.float32) * scale
        s_lat = jnp.einsum('bqkgd,bskd->bkgqs', q_blk, k_band).astype(jnp.float32) * scale
        s_lat = jnp.where(valid, s_lat, -jnp.inf)
        p = softmax_with_sink(jnp.concatenate([s_ctx, s_lat], axis=-1), sink).astype(v.dtype)
        return (jnp.einsum('bkgqs,bskd->bqkgd', p[..., :n_ctx], v_ctx)
                + jnp.einsum('bkgqs,bskd->bqkgd', p[..., n_ctx:], v_band))

    out = lax.map(one_block, (jnp.arange(nb), qb))
    return jnp.moveaxis(out, 0, 1).reshape(b, n, hq, hd)


def depthwise_conv(x, w, bias):
    y = lax.conv_general_dilated(x, w[:, None, :], window_strides=(1,),
                                 padding=[(D_CONV // 2, D_CONV // 2)],
                                 dimension_numbers=('NWC', 'WIO', 'NWC'),
                                 feature_group_count=x.shape[-1])
    return y + bias


def ssd_scan(x, dt, a, bm, cm, h0):
    b, n, nh, hp = x.shape
    ns = bm.shape[-1]
    nc = n // SSM_CHUNK
    q = SSM_CHUNK
    x = x.astype(jnp.float32).reshape(b, nc, q, nh, hp)
    dt = dt.reshape(b, nc, q, nh)
    bm = bm.astype(jnp.float32).reshape(b, nc, q, nh, ns)
    cm = cm.astype(jnp.float32).reshape(b, nc, q, nh, ns)
    a_cum = jnp.cumsum(dt * a, axis=2)
    tri = jnp.tril(jnp.ones((q, q), dtype=bool))
    seg = a_cum[:, :, :, None, :] - a_cum[:, :, None, :, :]
    decay = jnp.exp(jnp.where(tri[:, :, None], seg, -jnp.inf))
    xdt = x * dt[..., None]
    cb = jnp.einsum('bcihn,bcjhn->bcijh', cm, bm) * decay
    y_diag = jnp.einsum('bcijh,bcjhp->bcihp', cb, xdt)
    to_end = jnp.exp(a_cum[:, :, -1:, :] - a_cum)
    states = jnp.einsum('bcjhn,bcjhp->bchpn', bm * to_end[..., None], xdt)
    chunk_decay = jnp.exp(a_cum[:, :, -1, :])

    def carry_step(h, inp):
        s_c, d_c = inp
        return h * d_c[:, :, None, None] + s_c, h

    h_final, h_enter = lax.scan(carry_step, h0.astype(jnp.float32),
                                (jnp.moveaxis(states, 1, 0), jnp.moveaxis(chunk_decay, 1, 0)))
    h_enter = jnp.moveaxis(h_enter, 0, 1)
    y_off = jnp.einsum('bcihn,bchpn->bcihp', cm * jnp.exp(a_cum)[..., None], h_enter)
    return (y_diag + y_off).reshape(b, n, nh, hp), h_final


def ssm_mixer(z, xbc, dt_raw, h0_f, h0_b, p):
    b, n, _ = z.shape
    xbc = jax.nn.silu(depthwise_conv(xbc, p['conv_w'], p['conv_b']))
    xs, bm, cm = jnp.split(xbc, [D_SSM, D_SSM + SSM_GROUPS * SSM_STATE], axis=-1)
    rep = SSM_HEADS // SSM_GROUPS
    xh = xs.reshape(b, n, SSM_HEADS, SSM_HEAD_DIM)
    bm = jnp.repeat(bm.reshape(b, n, SSM_GROUPS, SSM_STATE), rep, axis=2)
    cm = jnp.repeat(cm.reshape(b, n, SSM_GROUPS, SSM_STATE), rep, axis=2)
    dt = jax.nn.softplus(dt_raw.astype(jnp.float32).reshape(b, n, 2, SSM_HEADS)
                         + p['dt_bias'].astype(jnp.float32))
    a = -jnp.exp(p['a_log'].astype(jnp.float32))
    flip = lambda t: jnp.flip(t, axis=1)
    y_f, h_f = ssd_scan(xh, dt[:, :, 0], a[0], bm, cm, h0_f)
    y_b, h_b = ssd_scan(flip(xh), flip(dt[:, :, 1]), a[1], flip(bm), flip(cm), h0_b)
    y = y_f + flip(y_b) + p['d_skip'].astype(jnp.float32)[:, None] * xh.astype(jnp.float32)
    y = y.reshape(b, n, D_SSM) * jax.nn.silu(z.astype(jnp.float32))
    y = rmsnorm(y.reshape(b, n, SSM_GROUPS, D_SSM // SSM_GROUPS),
                p['ssm_norm_g'].reshape(SSM_GROUPS, D_SSM // SSM_GROUPS)).reshape(b, n, D_SSM)
    return y.astype(z.dtype), h_f, h_b


def split_projection(h, w_in):
    offsets = [int(o) for o in np.cumsum(IN_SPLITS)[:-1]]
    return jnp.split(h @ w_in, offsets, axis=-1)


def merge_heads(oa, ob, oc, w_out):
    b, n = ob.shape[:2]
    return jnp.concatenate([oa.reshape(b, n, -1), ob, oc.reshape(b, n, -1)], axis=-1) @ w_out


def mix_context(h, p):
    b, n, _ = h.shape
    qa, ka, va, qc, kc, vc, z, xbc, dt = split_projection(h, p['w_in'])
    qa = rmsnorm(qa.reshape(b, n, ATT_HEADS, HEAD_DIM), p['q_norm_g'])
    ka = rmsnorm(ka.reshape(b, n, ATT_KV_HEADS, HEAD_DIM), p['k_norm_g'])
    va = va.reshape(b, n, ATT_KV_HEADS, HEAD_DIM)
    qc = qc.reshape(b, n, WIN_HEADS, HEAD_DIM)
    kc = kc.reshape(b, n, WIN_KV_HEADS, HEAD_DIM)
    vc = vc.reshape(b, n, WIN_KV_HEADS, HEAD_DIM)
    oa = block_attention(qa, ka, va, None)
    oc = block_attention(qc, kc, vc, p['sink'])
    h0 = jnp.zeros((b, SSM_HEADS, SSM_HEAD_DIM, SSM_STATE), jnp.float32)
    ob, h_f, h_b = ssm_mixer(z, xbc, dt, h0, h0, p)
    return merge_heads(oa, ob, oc, p['w_out']), ka, va, kc, vc, h_f, h_b


def mix_latent(h, cos, sin, ka_ctx, va_ctx, kc_ctx, vc_ctx, h0_f, h0_b, p):
    b, n, _ = h.shape
    qa, ka, va, qc, kc, vc, z, xbc, dt = split_projection(h, p['w_in'])
    qa = apply_rope(rmsnorm(qa.reshape(b, n, ATT_HEADS, HEAD_DIM), p['q_norm_g']), cos, sin)
    ka = apply_rope(rmsnorm(ka.reshape(b, n, ATT_KV_HEADS, HEAD_DIM), p['k_norm_g']), cos, sin)
    va = va.reshape(b, n, ATT_KV_HEADS, HEAD_DIM)
    oa = block_attention(qa, jnp.concatenate([ka_ctx, ka], axis=1),
                         jnp.concatenate([va_ctx, va], axis=1), None)
    qc = apply_rope(qc.reshape(b, n, WIN_HEADS, HEAD_DIM), cos, sin)
    kc = apply_rope(kc.reshape(b, n, WIN_KV_HEADS, HEAD_DIM), cos, sin)
    vc = vc.reshape(b, n, WIN_KV_HEADS, HEAD_DIM)
    oc = window_attention(qc, kc, vc, kc_ctx, vc_ctx, p['sink'])
    ob, _, _ = ssm_mixer(z, xbc, dt, h0_f, h0_b, p)
    return merge_heads(oa, ob, oc, p['w_out'])


def moe_ffn(h, router, w_gate, w_up, w_down):
    b, n, d = h.shape
    t = b * n
    x = h.reshape(t, d)
    logits = (x @ router).astype(jnp.float32)
    top_logit, top_e = lax.top_k(logits, TOP_K)
    top_w = jax.nn.softmax(top_logit, axis=-1)
    n_assign = t * TOP_K
    e_flat = top_e.reshape(n_assign)
    tok_flat = jnp.arange(n_assign) // TOP_K
    order = jnp.argsort(e_flat)
    e_sorted = e_flat[order]
    tok_sorted = tok_flat[order]
    w_sorted = top_w.reshape(n_assign)[order]
    counts = jnp.bincount(e_flat, length=N_EXPERTS)
    padded = (counts + MOE_BLOCK - 1) // MOE_BLOCK * MOE_BLOCK
    pad_end = jnp.cumsum(padded)
    pad_start = pad_end - padded
    start = jnp.cumsum(counts) - counts
    dest = pad_start[e_sorted] + jnp.arange(n_assign) - start[e_sorted]
    n_rows = n_assign + N_EXPERTS * MOE_BLOCK
    n_blocks = n_rows // MOE_BLOCK
    row_tok = jnp.zeros((n_rows,), jnp.int32).at[dest].set(tok_sorted)
    block_start = jnp.arange(n_blocks) * MOE_BLOCK
    block_e = jnp.minimum(jnp.sum(pad_end[None, :] <= block_start[:, None], axis=1), N_EXPERTS - 1)
    xb = x[row_tok].reshape(n_blocks, MOE_BLOCK, d)

    def expert_block(args):
        xe, e = args
        return swiglu(xe, w_gate[e], w_up[e], w_down[e])

    yb = lax.map(expert_block, (xb, block_e)).reshape(n_rows, d)
    y_assign = yb[dest] * w_sorted[:, None].astype(yb.dtype)
    out = jnp.zeros((t, d), h.dtype).at[tok_sorted].add(y_assign)
    return out.reshape(b, n, d)


def channel_mixer(h, layer, ffn_w_gate, ffn_w_up, ffn_w_down, moe_router, moe_w_gate, moe_w_up, moe_w_down):
    i = layer // 2
    if layer % 2 == 0:
        return swiglu(h, ffn_w_gate[i], ffn_w_up[i], ffn_w_down[i])
    return moe_ffn(h, moe_router[i], moe_w_gate[i], moe_w_up[i], moe_w_down[i])


def setup_inputs(seed: int = 0) -> dict:
    keys = iter(jax.random.split(jax.random.key(seed), 48))

    def nrm(shape, scale=1.0):
        return jax.random.normal(next(keys), shape, jnp.float32) * scale

    def gain(shape):
        return 1.0 + nrm(shape, 0.02)

    d = D_MODEL
    a_log = jnp.log(jax.random.uniform(next(keys), (DEPTH, 2, SSM_HEADS), jnp.float32, 1.0, 16.0))
    dt0 = jnp.exp(jax.random.uniform(next(keys), (DEPTH, 2, SSM_HEADS), jnp.float32,
                                     math.log(1e-3), math.log(1e-1)))
    dt_bias = dt0 + jnp.log(-jnp.expm1(-dt0))
    return {
        'x_prompt': nrm((BATCH, SEQ, d)),
        'x_sample': nrm((DEC_BATCH, DEC_SEQ, d)),
        'cache_attn_k': nrm((DEC_BATCH, DEPTH, PAST_LEN, ATT_KV_HEADS, HEAD_DIM)),
        'cache_attn_v': nrm((DEC_BATCH, DEPTH, PAST_LEN, ATT_KV_HEADS, HEAD_DIM)),
        'cache_win_k': nrm((DEC_BATCH, DEPTH, PAST_LEN, WIN_KV_HEADS, HEAD_DIM)),
        'cache_win_v': nrm((DEC_BATCH, DEPTH, PAST_LEN, WIN_KV_HEADS, HEAD_DIM)),
        'state_ssm_fwd': nrm((DEC_BATCH, DEPTH, SSM_HEADS, SSM_HEAD_DIM, SSM_STATE), 0.5),
        'state_ssm_bwd': nrm((DEC_BATCH, DEPTH, SSM_HEADS, SSM_HEAD_DIM, SSM_STATE), 0.5),
        'c': nrm((DEC_BATCH, d)),
        'c_ctx': nrm((d,)),
        'norm1_g': gain((DEPTH, d)),
        'norm2_g': gain((DEPTH, d)),
        'w_mod': nrm((DEPTH, d, 6 * d), 0.5 * d ** -0.5),
        'b_mod': nrm((DEPTH, 6 * d), 0.01),
        'w_in': nrm((DEPTH, d, IN_WIDTH), d ** -0.5),
        'q_norm_g': gain((DEPTH, HEAD_DIM)),
        'k_norm_g': gain((DEPTH, HEAD_DIM)),
        'conv_w': nrm((DEPTH, D_CONV, XBC_DIM), D_CONV ** -0.5),
        'conv_b': nrm((DEPTH, XBC_DIM), 0.01),
        'a_log': a_log,
        'dt_bias': dt_bias,
        'd_skip': gain((DEPTH, SSM_HEADS)),
        'ssm_norm_g': gain((DEPTH, D_SSM)),
        'sink': nrm((DEPTH, WIN_HEADS)),
        'w_out': nrm((DEPTH, MIX_WIDTH, d), MIX_WIDTH ** -0.5),
        'ffn_w_gate': nrm((N_DENSE, d, D_FF), d ** -0.5),
        'ffn_w_up': nrm((N_DENSE, d, D_FF), d ** -0.5),
        'ffn_w_down': nrm((N_DENSE, D_FF, d), D_FF ** -0.5),
        'moe_router': nrm((N_MOE, d, N_EXPERTS), d ** -0.5),
        'moe_w_gate': nrm((N_MOE, N_EXPERTS, d, D_FF_EXPERT), d ** -0.5),
        'moe_w_up': nrm((N_MOE, N_EXPERTS, d, D_FF_EXPERT), d ** -0.5),
        'moe_w_down': nrm((N_MOE, N_EXPERTS, D_FF_EXPERT, d), D_FF_EXPERT ** -0.5),
        'final_g': gain((d,)),
    }


def reference(x_prompt, x_sample, cache_attn_k, cache_attn_v, cache_win_k, cache_win_v,
              state_ssm_fwd, state_ssm_bwd, c, c_ctx, norm1_g, norm2_g, w_mod, b_mod, w_in,
              q_norm_g, k_norm_g, conv_w, conv_b, a_log, dt_bias, d_skip, ssm_norm_g, sink, w_out,
              ffn_w_gate, ffn_w_up, ffn_w_down, moe_router, moe_w_gate, moe_w_up, moe_w_down, final_g):
    cos, sin = rope_tables(x_sample.shape[1])
    xp = x_prompt
    xs = x_sample
    ak_list, av_list, wk_list, wv_list, sf_list, sb_list = [], [], [], [], [], []
    for l in range(DEPTH):
        p = {'w_in': w_in[l], 'w_out': w_out[l], 'q_norm_g': q_norm_g[l], 'k_norm_g': k_norm_g[l],
             'conv_w': conv_w[l], 'conv_b': conv_b[l], 'a_log': a_log[l], 'dt_bias': dt_bias[l],
             'd_skip': d_skip[l], 'ssm_norm_g': ssm_norm_g[l], 'sink': sink[l]}
        mp = jnp.split(jax.nn.silu(c_ctx) @ w_mod[l] + b_mod[l], 6, axis=-1)
        ms = [m[:, None, :] for m in jnp.split(jax.nn.silu(c) @ w_mod[l] + b_mod[l], 6, axis=-1)]

        h = modulate(rmsnorm(xp, norm1_g[l]), mp[0], mp[1])
        o, ak, av, wk, wv, sf, sb = mix_context(h, p)
        xp = xp + mp[2] * o
        h = modulate(rmsnorm(xp, norm2_g[l]), mp[3], mp[4])
        xp = xp + mp[5] * channel_mixer(h, l, ffn_w_gate, ffn_w_up, ffn_w_down,
                                        moe_router, moe_w_gate, moe_w_up, moe_w_down)
        ak_list.append(ak)
        av_list.append(av)
        wk_list.append(wk)
        wv_list.append(wv)
        sf_list.append(sf)
        sb_list.append(sb)

        h = modulate(rmsnorm(xs, norm1_g[l]), ms[0], ms[1])
        o = mix_latent(h, cos, sin, cache_attn_k[:, l], cache_attn_v[:, l], cache_win_k[:, l],
                       cache_win_v[:, l], state_ssm_fwd[:, l], state_ssm_bwd[:, l], p)
        xs = xs + ms[2] * o
        h = modulate(rmsnorm(xs, norm2_g[l]), ms[3], ms[4])
        xs = xs + ms[5] * channel_mixer(h, l, ffn_w_gate, ffn_w_up, ffn_w_down,
                                        moe_router, moe_w_gate, moe_w_up, moe_w_down)

    y_prompt = rmsnorm(xp, final_g)
    y_sample = rmsnorm(xs, final_g)
    new_attn_k = jnp.stack(ak_list, axis=1)
    new_attn_v = jnp.stack(av_list, axis=1)
    new_win_k = jnp.stack(wk_list, axis=1)
    new_win_v = jnp.stack(wv_list, axis=1)
    new_ssm_fwd = jnp.stack(sf_list, axis=1)
    new_ssm_bwd = jnp.stack(sb_list, axis=1)
    return (y_prompt, y_sample, new_attn_k, new_attn_v, new_win_k, new_win_v, new_ssm_fwd, new_ssm_bwd)
```

```python
import functools
import math

import numpy as np
import jax
import jax.numpy as jnp
from jax import lax
from jax.experimental import pallas as pl
from jax.experimental.pallas import tpu as pltpu

F32 = jnp.float32
BF16 = jnp.bfloat16
HIGHEST = lax.Precision.HIGHEST

GRID_W = 64
HEAD_DIM = 64
ATT_HEADS = 4
ATT_KV_HEADS = 2
WIN_HEADS = 4
WIN_KV_HEADS = 2
WINDOW = 128
ROPE_THETA = 10000.0
SSM_HEADS = 8
SSM_HEAD_DIM = 64
D_SSM = SSM_HEADS * SSM_HEAD_DIM
SSM_GROUPS = 2
SSM_STATE = 128
D_CONV = 5
SSM_CHUNK = 128
XBC_DIM = D_SSM + 2 * SSM_GROUPS * SSM_STATE
N_EXPERTS = 8
TOP_K = 2
NORM_EPS = 1e-6
NEG_BIG = -1e30

LANES = 128
SUBLANES = 8
MOD_ROWS = 16

OFF_QA, OFF_KA, OFF_VA, OFF_QC, OFF_KC, OFF_VC, OFF_Z, OFF_XBC, OFF_DT = (
    0, 256, 384, 512, 768, 896, 1024, 1536, 2560)

TM = 512
TQ_GLOBAL = 128
TQ_WINDOW = 256
MOE_BM = 1024
MOE_TF = 512
FFN_TF = 1408


def _cparams(sem, vmem_mb):
    return pltpu.CompilerParams(dimension_semantics=sem, vmem_limit_bytes=vmem_mb * 1024 * 1024)


def _dot(a, b, precision=None):
    return jnp.dot(a, b, preferred_element_type=F32, precision=precision)


def _dot_nt(a, b):
    return lax.dot_general(a, b, (((1,), (1,)), ((), ())), preferred_element_type=F32)


def _silu(x):
    return x / (1.0 + jnp.exp(-x))


def _norm_mod(x, g, shift, scale):
    ms = jnp.mean(x * x, axis=-1, keepdims=True)
    y = x * lax.rsqrt(ms + NORM_EPS) * g
    return y * (1.0 + scale) + shift


def _mod_kernel(c_ref, w_ref, b_ref, o_ref):
    a = _silu(c_ref[...])
    o_ref[...] = _dot(a, w_ref[...], HIGHEST) + b_ref[...]


def _mod_vectors(cc, w_mod, b_mod):
    depth, d, n = w_mod.shape
    tn = 1536
    return pl.pallas_call(
        _mod_kernel,
        grid=(depth, n // tn),
        in_specs=[pl.BlockSpec((MOD_ROWS, d), lambda l, j: (0, 0)),
                  pl.BlockSpec((None, d, tn), lambda l, j: (l, 0, j)),
                  pl.BlockSpec((None, 1, tn), lambda l, j: (l, 0, j))],
        out_specs=pl.BlockSpec((None, MOD_ROWS, tn), lambda l, j: (l, 0, j)),
        out_shape=jax.ShapeDtypeStruct((depth, MOD_ROWS, n), F32),
        compiler_params=_cparams(("arbitrary", "arbitrary"), 40),
        name="mod_vectors",
    )(cc, w_mod, b_mod.reshape(depth, 1, n))


def _mod_spec(layer, k, d, row_of_tile):
    base = (layer * 6 + k) * MOD_ROWS
    return pl.BlockSpec((None, 1, d), lambda i, *_: (base + row_of_tile(i), 0, 0))


def _inproj_kernel(x_ref, g_ref, sh_ref, sc_ref, w_ref, wdt_ref, bd_ref, qg_ref, kg_ref,
                   cos_ref, sin_ref,
                   qa_ref, ka_ref, va_ref, qc_ref, kc_ref, vc_ref, z_ref, xbc_ref, dt_ref,
                   cka_ref, cva_ref, ckc_ref, cvc_ref):
    h = _norm_mod(x_ref[...], g_ref[...], sh_ref[...], sc_ref[...]).astype(BF16)

    def proj(a, b):
        return _dot(h, w_ref[:, a:b])

    cos = cos_ref[...]
    sin = sin_ref[...]
    tm = cos.shape[0]
    lane = lax.broadcasted_iota(jnp.int32, (tm, LANES), 1)
    first = (lane % 32) < 16
    low = lane < HEAD_DIM
    bd = bd_ref[...]

    def rope(x):
        partner = jnp.where(first, pltpu.roll(x, LANES - 16, 1), pltpu.roll(x, 16, 1))
        return x * cos + partner * sin

    def head_norm(x, g):
        ms = _dot(x * x, bd, HIGHEST)
        return x * lax.rsqrt(ms + NORM_EPS) * g

    def stack_heads(c0, c1, out_ref):
        zero = jnp.zeros_like(c0)
        out_ref[0] = jnp.where(low, c0, zero).astype(out_ref.dtype)
        out_ref[1] = jnp.where(low, pltpu.roll(c0, HEAD_DIM, 1), zero).astype(out_ref.dtype)
        out_ref[2] = jnp.where(low, zero, pltpu.roll(c1, HEAD_DIM, 1)).astype(out_ref.dtype)
        out_ref[3] = jnp.where(low, zero, c1).astype(out_ref.dtype)

    scale = HEAD_DIM ** -0.5
    qg = qg_ref[...]
    qa0 = rope(head_norm(proj(OFF_QA, OFF_QA + LANES), qg)) * scale
    qa1 = rope(head_norm(proj(OFF_QA + LANES, OFF_KA), qg)) * scale
    stack_heads(qa0, qa1, qa_ref)
    ka = rope(head_norm(proj(OFF_KA, OFF_VA), kg_ref[...]))
    ka_ref[...] = ka.astype(ka_ref.dtype)
    cka_ref[...] = ka
    va = proj(OFF_VA, OFF_QC)
    va_ref[...] = va.astype(va_ref.dtype)
    cva_ref[...] = va

    qc0 = rope(proj(OFF_QC, OFF_QC + LANES)) * scale
    qc1 = rope(proj(OFF_QC + LANES, OFF_KC)) * scale
    stack_heads(qc0, qc1, qc_ref)
    kc = rope(proj(OFF_KC, OFF_VC))
    kc_ref[...] = kc.astype(kc_ref.dtype)
    ckc_ref[...] = kc
    vc = proj(OFF_VC, OFF_Z)
    vc_ref[...] = vc.astype(vc_ref.dtype)
    cvc_ref[...] = vc

    z_ref[...] = proj(OFF_Z, OFF_XBC)
    xbc_ref[...] = proj(OFF_XBC, OFF_DT)
    dt_ref[...] = _dot(h, wdt_ref[...])


def _inproj(x, lay, modv, layer, dims):
    t, d = x.shape
    n_tiles = t // TM
    npt = dims["prompt_tokens"] // TM
    tpb = dims["dec_seq"] // TM
    row_of_tile = dims["row_of_tile"](TM)

    def rope_idx(i):
        return jnp.where(i < npt, tpb, (i - npt) % tpb)

    def cache_idx(i):
        return jnp.minimum(i, npt)

    tok = lambda w: pl.BlockSpec((TM, w), lambda i: (i, 0))
    full = lambda a: pl.BlockSpec(a.shape, lambda i: (0,) * a.ndim)
    stack = pl.BlockSpec((4, TM, LANES), lambda i: (0, i, 0))
    cache = pl.BlockSpec((TM, LANES), lambda i: (cache_idx(i), 0))
    rope = pl.BlockSpec((TM, LANES), lambda i: (rope_idx(i), 0))
    cache_shape = jax.ShapeDtypeStruct(((npt + 1) * TM, LANES), F32)
    outs = pl.pallas_call(
        _inproj_kernel,
        grid=(n_tiles,),
        in_specs=[tok(d), full(lay["norm1_g"]),
                  _mod_spec(layer, 0, d, row_of_tile), _mod_spec(layer, 1, d, row_of_tile),
                  full(lay["w_main"]), full(lay["w_dt"]), full(lay["bd"]),
                  full(lay["q_norm_g"]), full(lay["k_norm_g"]), rope, rope],
        out_specs=[stack, tok(LANES), tok(LANES), stack, tok(LANES), tok(LANES),
                   tok(D_SSM), tok(XBC_DIM), tok(LANES), cache, cache, cache, cache],
        out_shape=[jax.ShapeDtypeStruct((4, t, LANES), BF16),
                   jax.ShapeDtypeStruct((t, LANES), BF16), jax.ShapeDtypeStruct((t, LANES), BF16),
                   jax.ShapeDtypeStruct((4, t, LANES), BF16),
                   jax.ShapeDtypeStruct((t, LANES), BF16), jax.ShapeDtypeStruct((t, LANES), BF16),
                   jax.ShapeDtypeStruct((t, D_SSM), F32), jax.ShapeDtypeStruct((t, XBC_DIM), F32),
                   jax.ShapeDtypeStruct((t, LANES), F32),
                   cache_shape, cache_shape, cache_shape, cache_shape],
        compiler_params=_cparams(("arbitrary",), 48),
        name="inproj",
    )(x, lay["norm1_g"], modv, modv, lay["w_main"], lay["w_dt"], lay["bd"],
      lay["q_norm_g"], lay["k_norm_g"], dims["rope_cos"], dims["rope_sin"])
    return outs


def _make_attn_kernel(tq, n_ctx, n_lat, window_len, has_sink):
    def kern(*refs):
        refs = list(refs)
        sink_ref = refs.pop(0) if has_sink else None
        q_ref = refs.pop(0)
        kc_ref = refs.pop(0) if n_ctx else None
        vc_ref = refs.pop(0) if n_ctx else None
        kl_ref, vl_ref = refs.pop(0), refs.pop(0)
        refs.pop(0)
        o_ref = refs.pop(0)

        rows = 4 * tq
        q = q_ref[...].reshape(rows, LANES)
        if window_len:
            q0 = pl.program_id(1) * tq
            ws = pl.multiple_of(jnp.clip(q0 - WINDOW, 0, n_lat - window_len), LANES)
            kl = kl_ref[pl.ds(ws, window_len), :]
            vl = vl_ref[pl.ds(ws, window_len), :]
            s_l = _dot_nt(q, kl)
            qpos = q0 + lax.broadcasted_iota(jnp.int32, (rows, window_len), 0) % tq
            kpos = ws + lax.broadcasted_iota(jnp.int32, (rows, window_len), 1)
            s_l = jnp.where(jnp.abs(qpos - kpos) <= WINDOW, s_l, NEG_BIG)
        else:
            kl = kl_ref[...]
            vl = vl_ref[...]
            s_l = _dot_nt(q, kl)
        m = jnp.max(s_l, axis=-1, keepdims=True)
        if n_ctx:
            s_c = _dot_nt(q, kc_ref[...])
            m = jnp.maximum(m, jnp.max(s_c, axis=-1, keepdims=True))
        if has_sink:
            head = lax.broadcasted_iota(jnp.int32, (rows, 1), 0) // tq
            snk = jnp.where(head == 0, sink_ref[0],
                            jnp.where(head == 1, sink_ref[1],
                                      jnp.where(head == 2, sink_ref[2], sink_ref[3])))
            m = jnp.maximum(m, snk)
        e_l = jnp.exp(s_l - m)
        den = jnp.sum(e_l, axis=-1, keepdims=True)
        o = _dot(e_l.astype(BF16), vl)
        if n_ctx:
            e_c = jnp.exp(s_c - m)
            den = den + jnp.sum(e_c, axis=-1, keepdims=True)
            o = o + _dot(e_c.astype(BF16), vc_ref[...])
        if has_sink:
            den = den + jnp.exp(snk - m)
        o = o / den
        low = lax.broadcasted_iota(jnp.int32, (tq, LANES), 1) < HEAD_DIM
        o0, o1, o2, o3 = (o[h * tq:(h + 1) * tq] for h in range(4))
        c0 = jnp.where(low, o0, pltpu.roll(o1, HEAD_DIM, 1))
        c1 = jnp.where(low, pltpu.roll(o2, HEAD_DIM, 1), o3)
        o_ref[:, 0:LANES] = c0.astype(o_ref.dtype)
        o_ref[:, LANES:2 * LANES] = c1.astype(o_ref.dtype)
    return kern


def _attention(q_stack, k_lat, v_lat, k_ctx, v_ctx, sink, prev_out, *, nb, seq, tq, row_off,
               window, name):
    t = q_stack.shape[1]
    nq = seq // tq
    qoff = row_off // tq
    boff = row_off // seq
    n_ctx = 0 if k_ctx is None else k_ctx.shape[1]
    window_len = min(seq, tq + 2 * WINDOW) if window else 0
    kern = _make_attn_kernel(tq, n_ctx, seq, window_len, sink is not None)
    in_specs, args = [], []
    if sink is not None:
        in_specs.append(pl.BlockSpec(memory_space=pltpu.SMEM))
        args.append(sink)
    in_specs.append(pl.BlockSpec((4, tq, LANES), lambda b, j: (0, qoff + b * nq + j, 0)))
    args.append(q_stack)
    if n_ctx:
        ctx_spec = pl.BlockSpec((None, n_ctx, LANES), lambda b, j: (b, 0, 0))
        in_specs += [ctx_spec, ctx_spec]
        args += [k_ctx, v_ctx]
    lat_spec = pl.BlockSpec((seq, LANES), lambda b, j: (boff + b, 0))
    in_specs += [lat_spec, lat_spec]
    args += [k_lat, v_lat]
    if prev_out is None:
        prev_out = jnp.zeros((t, 2 * LANES), BF16)
    in_specs.append(pl.BlockSpec(memory_space=pl.ANY))
    args.append(prev_out)
    aliases = {len(args) - 1: 0}
    return pl.pallas_call(
        kern,
        grid=(nb, nq),
        in_specs=in_specs,
        out_specs=pl.BlockSpec((tq, 2 * LANES), lambda b, j: (qoff + b * nq + j, 0)),
        out_shape=jax.ShapeDtypeStruct((t, 2 * LANES), BF16),
        input_output_aliases=aliases,
        compiler_params=_cparams(("arbitrary", "arbitrary"), 56),
        name=name,
    )(*args)


def _make_ssd_kernel(direction, nc):
    q = SSM_CHUNK
    halo = SUBLANES

    def kern(xp_ref, xm_ref, xn_ref, dt_ref, cw_ref, cb_ref, dtb_ref, alog_ref, tri_ref, h0_ref,
             *rest):
        if direction == 0:
            dsk_ref, y_ref, hout_ref, ext_scr, h_scr = rest
        else:
            yf_ref, z_ref, ng_ref, y_ref, hout_ref, ext_scr, h_scr = rest
        c = pl.program_id(1)
        cpos = c if direction == 0 else nc - 1 - c

        @pl.when(c == 0)
        def _():
            h_scr[...] = h0_ref[...]

        ext_scr[0:halo, :] = jnp.where(cpos > 0, xp_ref[...], 0.0)
        ext_scr[halo:halo + q, :] = xm_ref[...]
        ext_scr[halo + q:2 * halo + q, :] = jnp.where(cpos < nc - 1, xn_ref[...], 0.0)
        acc = cb_ref[...] + cw_ref[0:1, :] * ext_scr[halo - 2:halo - 2 + q, :]
        for k in range(1, D_CONV):
            acc = acc + cw_ref[k:k + 1, :] * ext_scr[halo - 2 + k:halo - 2 + k + q, :]
        xc = _silu(acc)
        xs = xc[:, 0:D_SSM]
        bm = xc[:, D_SSM:D_SSM + SSM_GROUPS * SSM_STATE]
        cm = xc[:, D_SSM + SSM_GROUPS * SSM_STATE:]

        dtx = dt_ref[...] + dtb_ref[...]
        dt = jnp.maximum(dtx, 0.0) + jnp.log1p(jnp.exp(-jnp.abs(dtx)))
        dta = dt * (-jnp.exp(alog_ref[...]))
        a_col = _dot(tri_ref[...], dta, HIGHEST)
        a_row = a_col.T
        row = lax.broadcasted_iota(jnp.int32, (q, q), 0)
        col = lax.broadcasted_iota(jnp.int32, (q, q), 1)
        live = (row >= col) if direction == 0 else (col >= row)
        low_lane = col < SSM_HEAD_DIM
        low_row = row < SSM_HEAD_DIM
        edge = q - 1 if direction == 0 else 0

        cbs = [_dot_nt(cm[:, g * SSM_STATE:(g + 1) * SSM_STATE].astype(BF16),
                       bm[:, g * SSM_STATE:(g + 1) * SSM_STATE].astype(BF16))
               for g in range(SSM_GROUPS)]

        rep = SSM_HEADS // SSM_GROUPS
        y_pairs = []
        for p in range(SSM_HEADS // 2):
            g = (2 * p) // rep
            cg = cm[:, g * SSM_STATE:(g + 1) * SSM_STATE]
            bg = bm[:, g * SSM_STATE:(g + 1) * SSM_STATE]
            x_pair = xs[:, p * LANES:(p + 1) * LANES]
            hl = [direction * SSM_HEADS + 2 * p, direction * SSM_HEADS + 2 * p + 1]
            ai = [a_col[:, l:l + 1] for l in hl]
            aj = [a_row[l:l + 1, :] for l in hl]
            a_edge = [a_col[edge:edge + 1, l:l + 1] for l in hl]
            dtp = jnp.where(low_lane, dt[:, hl[0]:hl[0] + 1], dt[:, hl[1]:hl[1] + 1])
            xdt = x_pair * dtp
            xdt_b = xdt.astype(BF16)
            xdt_t = xdt.T.astype(BF16)
            hs = h_scr[p]
            hs_b = hs.astype(BF16)
            yd, yo, st = [], [], []
            for k in range(2):
                decay = jnp.exp(jnp.where(live, ai[k] - aj[k], NEG_BIG))
                yd.append(_dot((cbs[g] * decay).astype(BF16), xdt_b))
                yo.append(_dot_nt((cg * jnp.exp(ai[k])).astype(BF16), hs_b))
                st.append(_dot(xdt_t, (bg * jnp.exp(a_edge[k] - ai[k])).astype(BF16)))
            y_pairs.append(jnp.where(low_lane, yd[0] + yo[0], yd[1] + yo[1]))
            chunk_decay = jnp.where(low_row, jnp.exp(a_edge[0]), jnp.exp(a_edge[1]))
            h_scr[p] = hs * chunk_decay + jnp.where(low_row, st[0], st[1])
        y = jnp.concatenate(y_pairs, axis=1)

        if direction == 0:
            y_ref[...] = y + dsk_ref[...] * xs
        else:
            gated = (yf_ref[...] + y) * _silu(z_ref[...])
            gw = D_SSM // SSM_GROUPS
            for g in range(SSM_GROUPS):
                part = gated[:, g * gw:(g + 1) * gw]
                ms = jnp.mean(part * part, axis=-1, keepdims=True)
                y_ref[:, g * gw:(g + 1) * gw] = (
                    part * lax.rsqrt(ms + NORM_EPS) * ng_ref[:, g * gw:(g + 1) * gw]).astype(y_ref.dtype)

        @pl.when(c == nc - 1)
        def _():
            hout_ref[...] = h_scr[...]
    return kern


def _ssd(direction, xbc, dt, lay, h0, extra, prev_out, *, nb, seq, row_off):
    t = xbc.shape[0]
    q = SSM_CHUNK
    nc = seq // q
    coff = row_off // q
    last_halo = t // SUBLANES - 1
    per = q // SUBLANES

    def chunk(b, c):
        cpos = c if direction == 0 else nc - 1 - c
        return coff + b * nc + cpos

    main = lambda w: pl.BlockSpec((q, w), lambda b, c: (chunk(b, c), 0))
    full = lambda a: pl.BlockSpec(a.shape, lambda b, c: (0,) * a.ndim)
    prev = pl.BlockSpec((SUBLANES, XBC_DIM), lambda b, c: (jnp.maximum(chunk(b, c) * per - 1, 0), 0))
    nxt = pl.BlockSpec((SUBLANES, XBC_DIM),
                       lambda b, c: (jnp.minimum((chunk(b, c) + 1) * per, last_halo), 0))
    state = pl.BlockSpec((None, SSM_HEADS // 2, LANES, SSM_STATE), lambda b, c: (b, 0, 0, 0))
    tri = lay["tri_fwd"] if direction == 0 else lay["tri_bwd"]
    in_specs = [prev, main(XBC_DIM), nxt, main(LANES), full(lay["conv_w"]), full(lay["conv_b"]),
                full(lay["dt_bias"]), full(lay["a_log"]), full(tri), state]
    args = [xbc, xbc, xbc, dt, lay["conv_w"], lay["conv_b"], lay["dt_bias"], lay["a_log"], tri, h0]
    if direction == 0:
        in_specs.append(full(lay["d_skip"]))
        args.append(lay["d_skip"])
        out_dtype = F32
    else:
        y_fwd, z = extra
        in_specs += [main(D_SSM), main(D_SSM), full(lay["ssm_norm_g"])]
        args += [y_fwd, z, lay["ssm_norm_g"]]
        out_dtype = BF16
    if prev_out is None:
        prev_out = jnp.zeros((t, D_SSM), out_dtype)
    in_specs.append(pl.BlockSpec(memory_space=pl.ANY))
    args.append(prev_out)
    aliases = {len(args) - 1: 0}
    inner = _make_ssd_kernel(direction, nc)
    n_in = len(in_specs)

    def kern(*refs):
        inner(*refs[:n_in - 1], *refs[n_in:])
    y, h_fin = pl.pallas_call(
        kern,
        grid=(nb, nc),
        in_specs=in_specs,
        out_specs=[main(D_SSM), state],
        out_shape=[jax.ShapeDtypeStruct((t, D_SSM), out_dtype),
                   jax.ShapeDtypeStruct((nb, SSM_HEADS // 2, LANES, SSM_STATE), F32)],
        scratch_shapes=[pltpu.VMEM((q + 2 * SUBLANES, XBC_DIM), F32),
                        pltpu.VMEM((SSM_HEADS // 2, LANES, SSM_STATE), F32)],
        input_output_aliases=aliases,
        compiler_params=_cparams(("arbitrary", "arbitrary"), 40),
        name="ssd_fwd" if direction == 0 else "ssd_bwd",
    )(*args)
    return y, h_fin


def _outproj_kernel(oa_ref, ob_ref, oc_ref, x_ref, gate_ref, w_ref, g_ref, sh_ref, sc_ref,
                    xo_ref, h_ref):
    na = ATT_HEADS * HEAD_DIM
    o = (_dot(oa_ref[...], w_ref[0:na, :]) + _dot(ob_ref[...], w_ref[na:na + D_SSM, :])
         + _dot(oc_ref[...], w_ref[na + D_SSM:, :]))
    xn = x_ref[...] + gate_ref[...] * o
    xo_ref[...] = xn
    h_ref[...] = _norm_mod(xn, g_ref[...], sh_ref[...], sc_ref[...]).astype(h_ref.dtype)


def _outproj(oa, ob, oc, x, lay, modv, layer, dims, h_dtype):
    t, d = x.shape
    row_of_tile = dims["row_of_tile"](TM)
    tok = lambda w: pl.BlockSpec((TM, w), lambda i: (i, 0))
    full = lambda a: pl.BlockSpec(a.shape, lambda i: (0,) * a.ndim)
    return pl.pallas_call(
        _outproj_kernel,
        grid=(t // TM,),
        in_specs=[tok(oa.shape[1]), tok(ob.shape[1]), tok(oc.shape[1]), tok(d),
                  _mod_spec(layer, 2, d, row_of_tile), full(lay["w_out"]), full(lay["norm2_g"]),
                  _mod_spec(layer, 3, d, row_of_tile), _mod_spec(layer, 4, d, row_of_tile)],
        out_specs=[tok(d), tok(d)],
        out_shape=[jax.ShapeDtypeStruct((t, d), F32), jax.ShapeDtypeStruct((t, d), h_dtype)],
        compiler_params=_cparams(("arbitrary",), 40),
        name="outproj",
    )(oa, ob, oc, x, modv, lay["w_out"], lay["norm2_g"], modv, modv)


def _ffn_kernel(h_ref, wg_ref, wu_ref, wd_ref, x_ref, gate_ref, o_ref, acc_ref):
    j = pl.program_id(1)
    h = h_ref[...]
    g = _dot(h, wg_ref[...])
    u = _dot(h, wu_ref[...])
    part = _dot((_silu(g) * u).astype(BF16), wd_ref[...])

    @pl.when(j == 0)
    def _():
        acc_ref[...] = part

    @pl.when(j > 0)
    def _():
        acc_ref[...] += part

    @pl.when(j == pl.num_programs(1) - 1)
    def _():
        o_ref[...] = x_ref[...] + gate_ref[...] * acc_ref[...]


def _dense_ffn(h, x, wg, wu, wd, modv, layer, dims):
    t, d = x.shape
    f = wg.shape[1]
    row_of_tile = dims["row_of_tile"](TM)
    gate_base = (layer * 6 + 5) * MOD_ROWS
    return pl.pallas_call(
        _ffn_kernel,
        grid=(t // TM, f // FFN_TF),
        in_specs=[pl.BlockSpec((TM, d), lambda i, j: (i, 0)),
                  pl.BlockSpec((d, FFN_TF), lambda i, j: (0, j)),
                  pl.BlockSpec((d, FFN_TF), lambda i, j: (0, j)),
                  pl.BlockSpec((FFN_TF, d), lambda i, j: (j, 0)),
                  pl.BlockSpec((TM, d), lambda i, j: (i, 0)),
                  pl.BlockSpec((None, 1, d), lambda i, j: (gate_base + row_of_tile(i), 0, 0))],
        out_specs=pl.BlockSpec((TM, d), lambda i, j: (i, 0)),
        out_shape=jax.ShapeDtypeStruct((t, d), F32),
        scratch_shapes=[pltpu.VMEM((TM, d), F32)],
        compiler_params=_cparams(("arbitrary", "arbitrary"), 56),
        name="dense_ffn",
    )(h, wg, wu, wd, x, modv)


def _route_kernel(h_ref, r_ref, tri_ref, info_ref, w_ref, cnt_ref, carry_ref):
    i = pl.program_id(0)

    @pl.when(i == 0)
    def _():
        carry_ref[...] = jnp.zeros_like(carry_ref)

    logits = _dot(h_ref[...], r_ref[...], HIGHEST)
    tm = logits.shape[0]
    lane = lax.broadcasted_iota(jnp.int32, (tm, LANES), 1)
    logits = jnp.where(lane < N_EXPERTS, logits, NEG_BIG)
    l1 = jnp.max(logits, axis=-1, keepdims=True)
    e1 = jnp.min(jnp.where(logits == l1, lane, LANES), axis=-1, keepdims=True)
    rest = jnp.where(lane == e1, NEG_BIG, logits)
    l2 = jnp.max(rest, axis=-1, keepdims=True)
    e2 = jnp.min(jnp.where(rest == l2, lane, LANES), axis=-1, keepdims=True)
    ex = jnp.exp(l2 - l1)
    w1 = 1.0 / (1.0 + ex)
    w2 = ex / (1.0 + ex)
    member = jnp.where((lane == e1) | (lane == e2), 1.0, 0.0)
    before = _dot(tri_ref[...], member.astype(BF16)) + carry_ref[...]
    r1 = jnp.sum(jnp.where(lane == e1, before, 0.0), axis=-1, keepdims=True)
    r2 = jnp.sum(jnp.where(lane == e2, before, 0.0), axis=-1, keepdims=True)
    info = jnp.where(lane == 0, e1.astype(F32),
                     jnp.where(lane == 1, e2.astype(F32),
                               jnp.where(lane == 2, r1, jnp.where(lane == 3, r2, 0.0))))
    info_ref[...] = info.astype(jnp.int32)
    w_ref[...] = jnp.where(lane == 0, w1, jnp.where(lane == 1, w2, 0.0))
    carry_ref[...] += jnp.sum(member, axis=0, keepdims=True)
    cnt_ref[...] = carry_ref[...]


def _route(h, router_pad, tri):
    t, d = h.shape
    return pl.pallas_call(
        _route_kernel,
        grid=(t // TM,),
        in_specs=[pl.BlockSpec((TM, d), lambda i: (i, 0)),
                  pl.BlockSpec(router_pad.shape, lambda i: (0, 0)),
                  pl.BlockSpec(tri.shape, lambda i: (0, 0))],
        out_specs=[pl.BlockSpec((TM, LANES), lambda i: (i, 0)),
                   pl.BlockSpec((TM, LANES), lambda i: (i, 0)),
                   pl.BlockSpec((1, LANES), lambda i: (0, 0))],
        out_shape=[jax.ShapeDtypeStruct((t, LANES), jnp.int32),
                   jax.ShapeDtypeStruct((t, LANES), F32),
                   jax.ShapeDtypeStruct((1, LANES), F32)],
        scratch_shapes=[pltpu.VMEM((1, LANES), F32)],
        compiler_params=_cparams(("arbitrary",), 40),
        name="moe_route",
    )(h, router_pad, tri)


def _row_copy(src, src_row, dst, dst_row, sem):
    return pltpu.make_async_copy(src.at[pl.ds(src_row, 1), :], dst.at[pl.ds(dst_row, 1), :], sem)


def _dispatch_kernel(dest_ref, pad_ref, h_ref, xb_ref, zero_ref, sem, zsem):
    i = pl.program_id(0)
    tm = h_ref.shape[0]
    base = i * tm * TOP_K

    def start(r, carry):
        for k in range(TOP_K):
            _row_copy(h_ref, r, xb_ref, dest_ref[base + r * TOP_K + k], sem).start()
        return carry

    lax.fori_loop(0, tm, start, 0)

    @pl.when(i == 0)
    def _():
        zero_ref[...] = jnp.zeros_like(zero_ref)
        for e in range(N_EXPERTS + 1):
            lo, hi = pad_ref[2 * e], pad_ref[2 * e + 1]
            lax.fori_loop(lo, hi, lambda r, c: (_row_copy(zero_ref, 0, xb_ref, r, zsem).start(), c)[1], 0)
        for e in range(N_EXPERTS + 1):
            lo, hi = pad_ref[2 * e], pad_ref[2 * e + 1]
            lax.fori_loop(lo, hi, lambda r, c: (_row_copy(zero_ref, 0, xb_ref, 0, zsem).wait(), c)[1], 0)

    def wait(r, carry):
        for k in range(TOP_K):
            _row_copy(h_ref, 0, xb_ref, 0, sem).wait()
        return carry

    lax.fori_loop(0, tm, wait, 0)


def _dispatch(h, dest_flat, pad_ranges, n_rows):
    t, d = h.shape
    return pl.pallas_call(
        _dispatch_kernel,
        grid_spec=pltpu.PrefetchScalarGridSpec(
            num_scalar_prefetch=2,
            grid=(t // TM,),
            in_specs=[pl.BlockSpec((TM, d), lambda i, *_: (i, 0))],
            out_specs=pl.BlockSpec(memory_space=pl.ANY),
            scratch_shapes=[pltpu.VMEM((SUBLANES, d), F32),
                            pltpu.SemaphoreType.DMA(()), pltpu.SemaphoreType.DMA(())]),
        out_shape=jax.ShapeDtypeStruct((n_rows, d), F32),
        compiler_params=_cparams(("arbitrary",), 40),
        name="moe_dispatch",
    )(dest_flat, pad_ranges, h)


def _expert_kernel(be_ref, bv_ref, x_ref, wg_ref, wu_ref, wd_ref, y_ref, xb_scr):
    i = pl.program_id(0)
    j = pl.program_id(1)

    @pl.when(bv_ref[i] > 0)
    def _():
        @pl.when(j == 0)
        def _():
            xb_scr[...] = x_ref[...].astype(BF16)

        x = xb_scr[...]
        g = _dot(x, wg_ref[...].astype(BF16))
        u = _dot(x, wu_ref[...].astype(BF16))
        part = _dot((_silu(g) * u).astype(BF16), wd_ref[...].astype(BF16))

        @pl.when(j == 0)
        def _():
            y_ref[...] = part

        @pl.when(j > 0)
        def _():
            y_ref[...] += part

    @pl.when((bv_ref[i] == 0) & (j == 0))
    def _():
        y_ref[...] = jnp.zeros_like(y_ref)


def _experts(xb, block_e, block_v, wg, wu, wd):
    n_rows, d = xb.shape
    f = wg.shape[2]

    def jj(i, j, bv):
        return jnp.where(bv[i] > 0, j, 0)

    return pl.pallas_call(
        _expert_kernel,
        grid_spec=pltpu.PrefetchScalarGridSpec(
            num_scalar_prefetch=2,
            grid=(n_rows // MOE_BM, f // MOE_TF),
            in_specs=[pl.BlockSpec((MOE_BM, d), lambda i, j, be, bv: (i, 0)),
                      pl.BlockSpec((None, d, MOE_TF), lambda i, j, be, bv: (be[i], 0, jj(i, j, bv))),
                      pl.BlockSpec((None, d, MOE_TF), lambda i, j, be, bv: (be[i], 0, jj(i, j, bv))),
                      pl.BlockSpec((None, MOE_TF, d), lambda i, j, be, bv: (be[i], jj(i, j, bv), 0))],
            out_specs=pl.BlockSpec((MOE_BM, d), lambda i, j, be, bv: (i, 0)),
            scratch_shapes=[pltpu.VMEM((MOE_BM, d), BF16)]),
        out_shape=jax.ShapeDtypeStruct((n_rows, d), F32),
        compiler_params=_cparams(("arbitrary", "arbitrary"), 56),
        name="moe_experts",
    )(block_e, block_v, xb, wg, wu, wd)


def _combine_kernel(dest_ref, yb_ref, w_ref, x_ref, gate_ref, fg_ref, o_ref, g0_ref, g1_ref, sem):
    i = pl.program_id(0)
    tm = x_ref.shape[0]
    base = i * tm * TOP_K
    bufs = (g0_ref, g1_ref)

    def start(r, carry):
        for k in range(TOP_K):
            _row_copy(yb_ref, dest_ref[base + r * TOP_K + k], bufs[k], r, sem).start()
        return carry

    lax.fori_loop(0, tm, start, 0)

    def wait(r, carry):
        for k in range(TOP_K):
            _row_copy(yb_ref, 0, bufs[k], 0, sem).wait()
        return carry

    lax.fori_loop(0, tm, wait, 0)
    w = w_ref[...]
    y = w[:, 0:1] * g0_ref[...] + w[:, 1:2] * g1_ref[...]
    xn = x_ref[...] + gate_ref[...] * y
    ms = jnp.mean(xn * xn, axis=-1, keepdims=True)
    o_ref[...] = xn * lax.rsqrt(ms + NORM_EPS) * fg_ref[...]


def _combine(yb, dest_flat, top_w, x, modv, layer, final_g, dims):
    t, d = x.shape
    tm = 256
    row_of_tile = dims["row_of_tile"](tm)
    gate_base = (layer * 6 + 5) * MOD_ROWS
    return pl.pallas_call(
        _combine_kernel,
        grid_spec=pltpu.PrefetchScalarGridSpec(
            num_scalar_prefetch=1,
            grid=(t // tm,),
            in_specs=[pl.BlockSpec(memory_space=pl.ANY),
                      pl.BlockSpec((tm, LANES), lambda i, *_: (i, 0)),
                      pl.BlockSpec((tm, d), lambda i, *_: (i, 0)),
                      pl.BlockSpec((None, 1, d), lambda i, *_: (gate_base + row_of_tile(i), 0, 0)),
                      pl.BlockSpec((1, d), lambda i, *_: (0, 0))],
            out_specs=pl.BlockSpec((tm, d), lambda i, *_: (i, 0)),
            scratch_shapes=[pltpu.VMEM((tm, d), F32), pltpu.VMEM((tm, d), F32),
                            pltpu.SemaphoreType.DMA(())]),
        out_shape=jax.ShapeDtypeStruct((t, d), F32),
        compiler_params=_cparams(("arbitrary",), 40),
        name="moe_combine",
    )(dest_flat, yb, top_w, x, modv, final_g)


def _moe_layer(h, x, router, wg, wu, wd, modv, layer, final_g, dims):
    t, d = x.shape
    router_pad = jnp.zeros((d, LANES), F32).at[:, :N_EXPERTS].set(router)
    info, top_w, counts = _route(h, router_pad, dims["tri_strict"])
    counts = counts[0, :N_EXPERTS].astype(jnp.int32)
    padded = (counts + MOE_BM - 1) // MOE_BM * MOE_BM
    pad_end = jnp.cumsum(padded)
    pad_start = pad_end - padded
    n_rows = t * TOP_K + N_EXPERTS * MOE_BM
    dest = pad_start[info[:, 0:TOP_K]] + info[:, TOP_K:2 * TOP_K]
    dest_flat = dest.reshape(t * TOP_K).astype(jnp.int32)
    lo = jnp.concatenate([pad_start + counts, pad_end[-1:]])
    hi = jnp.concatenate([pad_end, jnp.array([n_rows], jnp.int32)])
    pad_ranges = jnp.stack([lo, hi], axis=1).reshape(-1).astype(jnp.int32)
    block_start = jnp.arange(n_rows // MOE_BM, dtype=jnp.int32) * MOE_BM
    block_e = jnp.sum(pad_end[None, :] <= block_start[:, None], axis=1).astype(jnp.int32)
    block_v = (block_e < N_EXPERTS).astype(jnp.int32)
    last_e = jnp.max(jnp.where(padded > 0, jnp.arange(N_EXPERTS), 0)).astype(jnp.int32)
    block_e = jnp.where(block_v > 0, block_e, last_e)
    xb = _dispatch(h, dest_flat, pad_ranges, n_rows)
    yb = _experts(xb, block_e, block_v, wg, wu, wd)
    return _combine(yb, dest_flat, top_w, x, modv, layer, final_g, dims)


def _rope_tables(dec_seq):
    n_rows = dec_seq // GRID_W
    row = jnp.repeat(jnp.arange(n_rows, dtype=F32), GRID_W)
    col = (jnp.arange(dec_seq) % GRID_W).astype(F32)
    nf = HEAD_DIM // 4
    inv = ROPE_THETA ** (-jnp.arange(nf, dtype=F32) / nf)
    ang = jnp.stack([row[:, None] * inv, col[:, None] * inv], axis=1)
    cos = jnp.cos(ang)
    sin = jnp.sin(ang)
    cos_h = jnp.concatenate([cos, cos], axis=-1).reshape(dec_seq, HEAD_DIM)
    sin_h = jnp.concatenate([-sin, sin], axis=-1).reshape(dec_seq, HEAD_DIM)
    cos_t = jnp.concatenate([cos_h, cos_h], axis=-1)
    sin_t = jnp.concatenate([sin_h, sin_h], axis=-1)
    cos_t = jnp.concatenate([cos_t, jnp.ones((TM, LANES), F32)], axis=0)
    sin_t = jnp.concatenate([sin_t, jnp.zeros((TM, LANES), F32)], axis=0)
    return cos_t, sin_t


def _layer_params(l, norm1_g, norm2_g, w_in, q_norm_g, k_norm_g, conv_w, conv_b, a_log, dt_bias,
                  d_skip, ssm_norm_g, sink, w_out):
    d = w_in.shape[1]
    pad = lambda v: jnp.zeros((1, LANES), F32).at[0, :v.size].set(v.reshape(-1))
    seg = np.arange(LANES) // HEAD_DIM
    bd = jnp.asarray((seg[:, None] == seg[None, :]).astype(np.float32) / HEAD_DIM)
    r = np.arange(SSM_CHUNK)
    return {
        "norm1_g": norm1_g[l].reshape(1, d), "norm2_g": norm2_g[l].reshape(1, d),
        "w_main": w_in[l, :, :OFF_DT].astype(BF16),
        "w_dt": jnp.zeros((d, LANES), F32).at[:, :2 * SSM_HEADS].set(w_in[l, :, OFF_DT:]).astype(BF16),
        "bd": bd,
        "q_norm_g": jnp.tile(q_norm_g[l], 2).reshape(1, LANES),
        "k_norm_g": jnp.tile(k_norm_g[l], 2).reshape(1, LANES),
        "conv_w": jnp.zeros((SUBLANES, XBC_DIM), F32).at[:D_CONV].set(conv_w[l]),
        "conv_b": conv_b[l].reshape(1, XBC_DIM),
        "a_log": pad(a_log[l]), "dt_bias": pad(dt_bias[l]),
        "d_skip": jnp.repeat(d_skip[l], SSM_HEAD_DIM).reshape(1, D_SSM),
        "ssm_norm_g": ssm_norm_g[l].reshape(1, D_SSM),
        "sink": sink[l],
        "w_out": w_out[l].astype(BF16),
        "tri_fwd": jnp.asarray((r[:, None] >= r[None, :]).astype(np.float32)),
        "tri_bwd": jnp.asarray((r[:, None] <= r[None, :]).astype(np.float32)),
    }


def kernel(x_prompt, x_sample, cache_attn_k, cache_attn_v, cache_win_k, cache_win_v, state_ssm_fwd, state_ssm_bwd, c, c_ctx, norm1_g, norm2_g, w_mod, b_mod, w_in, q_norm_g, k_norm_g, conv_w, conv_b, a_log, dt_bias, d_skip, ssm_norm_g, sink, w_out, ffn_w_gate, ffn_w_up, ffn_w_down, moe_router, moe_w_gate, moe_w_up, moe_w_down, final_g):
    batch, seq, d = x_prompt.shape
    dec_batch, dec_seq, _ = x_sample.shape
    depth = w_in.shape[0]
    past = cache_attn_k.shape[2]
    tp = batch * seq
    t = tp + dec_batch * dec_seq
    assert depth == 2 and tp % (2 * TM) == 0 and dec_seq % (2 * TM) == 0
    assert 1 + dec_batch <= MOD_ROWS and seq % SSM_CHUNK == 0 and tp % dec_seq == 0

    def row_of_tile(tm):
        npt, tpb = tp // tm, dec_seq // tm
        return lambda i: jnp.where(i < npt, 0, 1 + (i - npt) // tpb)

    cos_t, sin_t = _rope_tables(dec_seq)
    rr = np.arange(TM)
    dims = {"prompt_tokens": tp, "dec_seq": dec_seq, "row_of_tile": row_of_tile,
            "rope_cos": cos_t, "rope_sin": sin_t,
            "tri_strict": jnp.asarray((rr[:, None] > rr[None, :]).astype(np.float32)).astype(BF16)}

    cc = jnp.zeros((MOD_ROWS, d), F32).at[0].set(c_ctx).at[1:1 + dec_batch].set(c)
    mod = _mod_vectors(cc, w_mod, b_mod)
    modv = mod.reshape(depth, MOD_ROWS, 6, d).transpose(0, 2, 1, 3).reshape(depth * 6 * MOD_ROWS, 1, d)

    x = jnp.concatenate([x_prompt.reshape(tp, d), x_sample.reshape(dec_batch * dec_seq, d)], axis=0)
    zero_state = jnp.zeros((batch, SSM_HEADS // 2, LANES, SSM_STATE), F32)
    caches = [[] for _ in range(6)]
    y = None
    for l in range(depth):
        lay = _layer_params(l, norm1_g, norm2_g, w_in, q_norm_g, k_norm_g, conv_w, conv_b, a_log,
                            dt_bias, d_skip, ssm_norm_g, sink, w_out)
        (qa, ka, va, qc, kc, vc, z, xbc, dt, cka, cva, ckc, cvc) = _inproj(x, lay, modv, l, dims)
        ctx = lambda a: a[:, l].reshape(dec_batch, past, LANES).astype(BF16)

        oa = _attention(qa, ka, va, None, None, None, None, nb=batch, seq=seq, tq=seq, row_off=0,
                        window=False, name="attn_a_ctx")
        oa = _attention(qa, ka, va, ctx(cache_attn_k), ctx(cache_attn_v), None, oa, nb=dec_batch,
                        seq=dec_seq, tq=TQ_GLOBAL, row_off=tp, window=False, name="attn_a_lat")
        oc = _attention(qc, kc, vc, None, None, lay["sink"], None, nb=batch, seq=seq, tq=seq,
                        row_off=0, window=False, name="attn_c_ctx")
        oc = _attention(qc, kc, vc, ctx(cache_win_k), ctx(cache_win_v), lay["sink"], oc,
                        nb=dec_batch, seq=dec_seq, tq=TQ_WINDOW, row_off=tp, window=True,
                        name="attn_c_lat")
        pair = lambda s: s[:, l].reshape(dec_batch, SSM_HEADS // 2, LANES, SSM_STATE)
        yf, hf_p = _ssd(0, xbc, dt, lay, zero_state, None, None, nb=batch, seq=seq, row_off=0)
        yf, _ = _ssd(0, xbc, dt, lay, pair(state_ssm_fwd), None, yf, nb=dec_batch, seq=dec_seq,
                     row_off=tp)
        ob, hb_p = _ssd(1, xbc, dt, lay, zero_state, (yf, z), None, nb=batch, seq=seq, row_off=0)
        ob, _ = _ssd(1, xbc, dt, lay, pair(state_ssm_bwd), (yf, z), ob, nb=dec_batch, seq=dec_seq,
                     row_off=tp)

        kv = lambda a: a[:tp].reshape(batch, seq, ATT_KV_HEADS, HEAD_DIM)
        st = lambda s: s.reshape(batch, SSM_HEADS, SSM_HEAD_DIM, SSM_STATE)
        for lst, val in zip(caches, (kv(cka), kv(cva), kv(ckc), kv(cvc), st(hf_p), st(hb_p))):
            lst.append(val)

        if l % 2 == 0:
            x, h2 = _outproj(oa, ob, oc, x, lay, modv, l, dims, BF16)
            i = l // 2
            x = _dense_ffn(h2, x, ffn_w_gate[i].astype(BF16), ffn_w_up[i].astype(BF16),
                           ffn_w_down[i].astype(BF16), modv, l, dims)
        else:
            x, h2 = _outproj(oa, ob, oc, x, lay, modv, l, dims, F32)
            i = l // 2
            y = _moe_layer(h2, x, moe_router[i], moe_w_gate[i], moe_w_up[i], moe_w_down[i], modv, l,
                           final_g.reshape(1, d), dims)

    y_prompt = y[:tp].reshape(batch, seq, d)
    y_sample = y[tp:].reshape(dec_batch, dec_seq, d)
    return (y_prompt, y_sample) + tuple(jnp.stack(lst, axis=1) for lst in caches)
```

```python
import numpy as np
import jax
import jax.numpy as jnp
from jax import lax
from jax.experimental import pallas as pl
from jax.experimental.pallas import tpu as pltpu

F32 = jnp.float32
BF16 = jnp.bfloat16
HIGHEST = lax.Precision.HIGHEST

GRID_W = 64
HEAD_DIM = 64
ATT_HEADS = 4
ATT_KV_HEADS = 2
WIN_HEADS = 4
WIN_KV_HEADS = 2
WINDOW = 128
ROPE_THETA = 10000.0
SSM_HEADS = 8
SSM_HEAD_DIM = 64
D_SSM = SSM_HEADS * SSM_HEAD_DIM
SSM_GROUPS = 2
SSM_STATE = 128
D_CONV = 5
SSM_CHUNK = 128
XBC_DIM = D_SSM + 2 * SSM_GROUPS * SSM_STATE
N_EXPERTS = 8
TOP_K = 2
NORM_EPS = 1e-6
NEG_BIG = -1e30
LOG2E = 1.4426950408889634

LANES = 128
SUBLANES = 8
MOD_ROWS = 16

OFF_QA, OFF_KA, OFF_VA, OFF_QC, OFF_KC, OFF_VC, OFF_Z, OFF_XBC, OFF_DT = (
    0, 256, 384, 512, 768, 896, 1024, 1536, 2560)

TM = 512
TQ_GLOBAL = 256
TQ_WINDOW = 256
ATTN_KEY_CHUNK = 256
ATTN_KEY_CHUNK_WINDOW = 512
MOE_BM = 1024
MOE_TF = 512
FFN_TF = 1408
DMA_UNROLL = 8


def _cparams(sem, vmem_mb):
    return pltpu.CompilerParams(dimension_semantics=sem, vmem_limit_bytes=vmem_mb * 1024 * 1024)


def _dot(a, b, precision=None):
    return jnp.dot(a, b, preferred_element_type=F32, precision=precision)


def _dot_nt(a, b):
    return lax.dot_general(a, b, (((1,), (1,)), ((), ())), preferred_element_type=F32)


def _silu(x):
    return x / (1.0 + jnp.exp(-x))


def _norm_mod(x, g, shift, scale):
    ms = jnp.mean(x * x, axis=-1, keepdims=True)
    y = x * lax.rsqrt(ms + NORM_EPS) * g
    return y * (1.0 + scale) + shift


def _pick_x(i, npt, xs):
    if len(xs) == 1:
        return xs[0][...]
    return jnp.where(i < npt, xs[0][...], xs[1][...])


def _x_specs(x, tm, npt):
    d = x[0].shape[1]
    if len(x) == 1:
        return [pl.BlockSpec((tm, d), lambda i, *_: (i, 0))]
    return [pl.BlockSpec((tm, d), lambda i, *_: (jnp.minimum(i, npt - 1), 0)),
            pl.BlockSpec((tm, d), lambda i, *_: (jnp.maximum(i - npt, 0), 0))]


def _mod_kernel(c_ref, w_ref, b_ref, o_ref):
    a = _silu(c_ref[...])
    o_ref[...] = _dot(a, w_ref[...], HIGHEST) + b_ref[...]


def _mod_vectors(cc, w_mod, b_mod):
    depth, d, n = w_mod.shape
    tn = 1536
    return pl.pallas_call(
        _mod_kernel,
        grid=(depth, n // tn),
        in_specs=[pl.BlockSpec((MOD_ROWS, d), lambda l, j: (0, 0)),
                  pl.BlockSpec((None, d, tn), lambda l, j: (l, 0, j)),
                  pl.BlockSpec((None, 1, tn), lambda l, j: (l, 0, j))],
        out_specs=pl.BlockSpec((None, MOD_ROWS, tn), lambda l, j: (l, 0, j)),
        out_shape=jax.ShapeDtypeStruct((depth, MOD_ROWS, n), F32),
        compiler_params=_cparams(("arbitrary", "arbitrary"), 40),
        name="mod_vectors",
    )(cc, w_mod, b_mod.reshape(depth, 1, n))


def _mod_spec(layer, k, d, row_of_tile):
    base = (layer * 6 + k) * MOD_ROWS
    return pl.BlockSpec((None, 1, d), lambda i, *_: (base + row_of_tile(i), 0, 0))


def _make_inproj_kernel(n_x, npt):
    def kern(*refs):
        xs = refs[:n_x]
        (g_ref, sh_ref, sc_ref, w_ref, wdt_ref, bd_ref, qg_ref, kg_ref, cos_ref, sin_ref,
         qa_ref, ka_ref, va_ref, qc_ref, kc_ref, vc_ref, z_ref, xbc_ref, dt_ref,
         cka_ref, cva_ref, ckc_ref, cvc_ref) = refs[n_x:]
        x = _pick_x(pl.program_id(0), npt, xs)
        h = _norm_mod(x, g_ref[...], sh_ref[...], sc_ref[...]).astype(BF16)

        def proj(a, b):
            return _dot(h, w_ref[:, a:b])

        cos = cos_ref[...]
        sin = sin_ref[...]
        tm = cos.shape[0]
        lane = lax.broadcasted_iota(jnp.int32, (tm, LANES), 1)
        first = (lane % 32) < 16
        low = lane < HEAD_DIM
        bd = bd_ref[...]

        def rope(v):
            partner = jnp.where(first, pltpu.roll(v, LANES - 16, 1), pltpu.roll(v, 16, 1))
            return v * cos + partner * sin

        def head_norm(v, g):
            sq = v * v
            hi = sq.astype(BF16)
            lo = (sq - hi.astype(F32)).astype(BF16)
            ms = _dot(hi, bd) + _dot(lo, bd)
            return v * lax.rsqrt(ms + NORM_EPS) * g

        def stack_heads(c0, c1, out_ref):
            zero = jnp.zeros_like(c0)
            out_ref[0] = jnp.where(low, c0, zero).astype(out_ref.dtype)
            out_ref[1] = jnp.where(low, pltpu.roll(c0, HEAD_DIM, 1), zero).astype(out_ref.dtype)
            out_ref[2] = jnp.where(low, zero, pltpu.roll(c1, HEAD_DIM, 1)).astype(out_ref.dtype)
            out_ref[3] = jnp.where(low, zero, c1).astype(out_ref.dtype)

        scale = LOG2E * HEAD_DIM ** -0.5
        qg = qg_ref[...]
        qa0 = rope(head_norm(proj(OFF_QA, OFF_QA + LANES), qg)) * scale
        qa1 = rope(head_norm(proj(OFF_QA + LANES, OFF_KA), qg)) * scale
        stack_heads(qa0, qa1, qa_ref)
        ka = rope(head_norm(proj(OFF_KA, OFF_VA), kg_ref[...]))
        ka_ref[...] = ka.astype(ka_ref.dtype)
        cka_ref[...] = ka
        va = proj(OFF_VA, OFF_QC)
        va_ref[...] = va.astype(va_ref.dtype)
        cva_ref[...] = va

        qc0 = rope(proj(OFF_QC, OFF_QC + LANES)) * scale
        qc1 = rope(proj(OFF_QC + LANES, OFF_KC)) * scale
        stack_heads(qc0, qc1, qc_ref)
        kc = rope(proj(OFF_KC, OFF_VC))
        kc_ref[...] = kc.astype(kc_ref.dtype)
        ckc_ref[...] = kc
        vc = proj(OFF_VC, OFF_Z)
        vc_ref[...] = vc.astype(vc_ref.dtype)
        cvc_ref[...] = vc

        z_ref[...] = proj(OFF_Z, OFF_XBC)
        xbc_ref[...] = proj(OFF_XBC, OFF_DT)
        dt_ref[...] = _dot(h, wdt_ref[...])
    return kern


def _inproj(x, lay, modv, layer, dims):
    t, d = dims["tokens"], x[0].shape[1]
    n_tiles = t // TM
    npt = dims["prompt_tokens"] // TM
    tpb = dims["dec_seq"] // TM
    row_of_tile = dims["row_of_tile"](TM)

    def rope_idx(i):
        return jnp.where(i < npt, tpb, (i - npt) % tpb)

    def cache_idx(i):
        return jnp.minimum(i, npt)

    tok = lambda w: pl.BlockSpec((TM, w), lambda i: (i, 0))
    full = lambda a: pl.BlockSpec(a.shape, lambda i: (0,) * a.ndim)
    stack = pl.BlockSpec((4, TM, LANES), lambda i: (0, i, 0))
    cache = pl.BlockSpec((TM, LANES), lambda i: (cache_idx(i), 0))
    rope = pl.BlockSpec((TM, LANES), lambda i: (rope_idx(i), 0))
    cache_shape = jax.ShapeDtypeStruct(((npt + 1) * TM, LANES), F32)
    return pl.pallas_call(
        _make_inproj_kernel(len(x), npt),
        grid=(n_tiles,),
        in_specs=_x_specs(x, TM, npt) + [
            full(lay["norm1_g"]),
            _mod_spec(layer, 0, d, row_of_tile), _mod_spec(layer, 1, d, row_of_tile),
            full(lay["w_main"]), full(lay["w_dt"]), full(lay["bd"]),
            full(lay["q_norm_g"]), full(lay["k_norm_g"]), rope, rope],
        out_specs=[stack, tok(LANES), tok(LANES), stack, tok(LANES), tok(LANES),
                   tok(D_SSM), tok(XBC_DIM), tok(LANES), cache, cache, cache, cache],
        out_shape=[jax.ShapeDtypeStruct((4, t, LANES), BF16),
                   jax.ShapeDtypeStruct((t, LANES), BF16), jax.ShapeDtypeStruct((t, LANES), BF16),
                   jax.ShapeDtypeStruct((4, t, LANES), BF16),
                   jax.ShapeDtypeStruct((t, LANES), BF16), jax.ShapeDtypeStruct((t, LANES), BF16),
                   jax.ShapeDtypeStruct((t, D_SSM), F32), jax.ShapeDtypeStruct((t, XBC_DIM), F32),
                   jax.ShapeDtypeStruct((t, LANES), F32),
                   cache_shape, cache_shape, cache_shape, cache_shape],
        compiler_params=_cparams(("arbitrary",), 48),
        name="inproj",
    )(*x, lay["norm1_g"], modv, modv, lay["w_main"], lay["w_dt"], lay["bd"],
      lay["q_norm_g"], lay["k_norm_g"], dims["rope_cos"], dims["rope_sin"])


def _make_attn_kernel(tq, n_ctx, n_lat, window_len, has_sink):
    ck = ATTN_KEY_CHUNK_WINDOW if window_len else ATTN_KEY_CHUNK

    def kern(*refs):
        refs = list(refs)
        sink_ref = refs.pop(0) if has_sink else None
        q_ref = refs.pop(0)
        kc_ref = refs.pop(0) if n_ctx else None
        vc_ref = refs.pop(0) if n_ctx else None
        kl_ref, vl_ref = refs.pop(0), refs.pop(0)
        refs.pop(0)
        o_ref, vaug_ref = refs

        @pl.when(pl.program_id(1) == 0)
        def _():
            if n_ctx:
                vaug_ref[0:n_ctx, 0:LANES] = vc_ref[...]
            vaug_ref[n_ctx:n_ctx + n_lat, 0:LANES] = vl_ref[...]
            vaug_ref[:, LANES:2 * LANES] = jnp.ones((n_ctx + n_lat, LANES), BF16)

        rows = 4 * tq
        q = q_ref[...].reshape(rows, LANES)
        if has_sink:
            head = lax.broadcasted_iota(jnp.int32, (rows, 1), 0) // tq
            snk = LOG2E * jnp.where(head == 0, sink_ref[0],
                                    jnp.where(head == 1, sink_ref[1],
                                              jnp.where(head == 2, sink_ref[2], sink_ref[3])))
            m = snk
        else:
            m = jnp.full((rows, 1), NEG_BIG, F32)
        acc = jnp.zeros((rows, 2 * LANES), F32)

        chunks = []
        for c0 in range(0, n_ctx, ck):
            n = min(ck, n_ctx - c0)
            chunks.append((kc_ref[c0:c0 + n, :], vaug_ref[c0:c0 + n, :], None))
        if window_len:
            q0 = pl.program_id(1) * tq
            ws = pl.multiple_of(jnp.clip(q0 - WINDOW, 0, n_lat - window_len), LANES)
            qpos = q0 + lax.broadcasted_iota(jnp.int32, (rows, window_len), 0) % tq
            kpos = ws + lax.broadcasted_iota(jnp.int32, (rows, window_len), 1)
            valid = jnp.abs(qpos - kpos) <= WINDOW
            chunks.append((kl_ref[pl.ds(ws, window_len), :],
                           vaug_ref[pl.ds(n_ctx + ws, window_len), :], valid))
        else:
            for c0 in range(0, n_lat, ck):
                n = min(ck, n_lat - c0)
                chunks.append((kl_ref[c0:c0 + n, :], vaug_ref[n_ctx + c0:n_ctx + c0 + n, :], None))

        for keys, vaug, valid in chunks:
            s = _dot_nt(q, keys)
            if valid is not None:
                s = jnp.where(valid, s, NEG_BIG)
            m_new = jnp.maximum(m, jnp.max(s, axis=-1, keepdims=True))
            p = jnp.exp2(s - m_new).astype(BF16)
            acc = acc * jnp.exp2(m - m_new) + _dot(p, vaug)
            m = m_new

        den = acc[:, LANES:2 * LANES]
        if has_sink:
            den = den + jnp.exp2(snk - m)
        o = acc[:, 0:LANES] / den
        low = lax.broadcasted_iota(jnp.int32, (tq, LANES), 1) < HEAD_DIM
        o0, o1, o2, o3 = (o[h * tq:(h + 1) * tq] for h in range(4))
        c0 = jnp.where(low, o0, pltpu.roll(o1, HEAD_DIM, 1))
        c1 = jnp.where(low, pltpu.roll(o2, HEAD_DIM, 1), o3)
        o_ref[:, 0:LANES] = c0.astype(o_ref.dtype)
        o_ref[:, LANES:2 * LANES] = c1.astype(o_ref.dtype)
    return kern


def _attention(q_stack, k_lat, v_lat, k_ctx, v_ctx, sink, prev_out, *, nb, seq, tq, row_off,
               window, name):
    t = q_stack.shape[1]
    nq = seq // tq
    qoff = row_off // tq
    boff = row_off // seq
    n_ctx = 0 if k_ctx is None else k_ctx.shape[1]
    window_len = min(seq, tq + 2 * WINDOW) if window else 0
    kern = _make_attn_kernel(tq, n_ctx, seq, window_len, sink is not None)
    in_specs, args = [], []
    if sink is not None:
        in_specs.append(pl.BlockSpec(memory_space=pltpu.SMEM))
        args.append(sink)
    in_specs.append(pl.BlockSpec((4, tq, LANES), lambda b, j: (0, qoff + b * nq + j, 0)))
    args.append(q_stack)
    if n_ctx:
        ctx_spec = pl.BlockSpec((None, n_ctx, LANES), lambda b, j: (b, 0, 0))
        in_specs += [ctx_spec, ctx_spec]
        args += [k_ctx, v_ctx]
    lat_spec = pl.BlockSpec((seq, LANES), lambda b, j: (boff + b, 0))
    in_specs += [lat_spec, lat_spec]
    args += [k_lat, v_lat]
    if prev_out is None:
        prev_out = jnp.zeros((t, 2 * LANES), BF16)
    in_specs.append(pl.BlockSpec(memory_space=pl.ANY))
    args.append(prev_out)
    aliases = {len(args) - 1: 0}
    return pl.pallas_call(
        kern,
        grid=(nb, nq),
        in_specs=in_specs,
        out_specs=pl.BlockSpec((tq, 2 * LANES), lambda b, j: (qoff + b * nq + j, 0)),
        out_shape=jax.ShapeDtypeStruct((t, 2 * LANES), BF16),
        scratch_shapes=[pltpu.VMEM((n_ctx + seq, 2 * LANES), BF16)],
        input_output_aliases=aliases,
        compiler_params=_cparams(("arbitrary", "arbitrary"), 48),
        name=name,
    )(*args)


def _split_bf16(v, n):
    parts = []
    for _ in range(n):
        part = v.astype(BF16)
        parts.append(part)
        v = v - part.astype(F32)
    return parts


def _dot_parts(a, parts):
    out = _dot(a, parts[0])
    for part in parts[1:]:
        out = out + _dot(a, part)
    return out


def _parts_dot(parts, b):
    out = _dot(parts[0], b)
    for part in parts[1:]:
        out = out + _dot(part, b)
    return out


def _ssd_chunk(direction, xc, dt_ref, dtb_ref, alog_ref, tri_ref, ewide_ref, epair_ref, h_scr):
    q = SSM_CHUNK
    xs = xc[:, 0:D_SSM]
    bm = xc[:, D_SSM:D_SSM + SSM_GROUPS * SSM_STATE]
    cm = xc[:, D_SSM + SSM_GROUPS * SSM_STATE:]

    dtx = dt_ref[...] + dtb_ref[...]
    dt = jnp.maximum(dtx, 0.0) + jnp.log1p(jnp.exp(-jnp.abs(dtx)))
    dta = dt * (-LOG2E * jnp.exp(alog_ref[...]))
    a_col = _dot_parts(tri_ref[...], _split_bf16(dta, 3))
    a_row = a_col.T
    a_wide = _parts_dot(_split_bf16(a_col, 3), ewide_ref[...])
    dt_lanes = _parts_dot(_split_bf16(dt, 2), epair_ref[...])
    row = lax.broadcasted_iota(jnp.int32, (q, q), 0)
    col = lax.broadcasted_iota(jnp.int32, (q, q), 1)
    live = (row >= col) if direction == 0 else (col >= row)
    low_lane = col < SSM_HEAD_DIM
    low_row = row < SSM_HEAD_DIM
    edge = q - 1 if direction == 0 else 0

    group = lambda v, g: v[:, g * SSM_STATE:(g + 1) * SSM_STATE]
    cm_b = [group(cm, g).astype(BF16) for g in range(SSM_GROUPS)]
    bm_b = [group(bm, g).astype(BF16) for g in range(SSM_GROUPS)]
    cbs = [_dot_nt(cm_b[g], bm_b[g]) for g in range(SSM_GROUPS)]

    rep = SSM_HEADS // SSM_GROUPS
    y_pairs = []
    for p in range(SSM_HEADS // 2):
        g = (2 * p) // rep
        hl = [direction * SSM_HEADS + 2 * p, direction * SSM_HEADS + 2 * p + 1]
        a_head = [a_wide[:, (2 * p + k) * LANES:(2 * p + k + 1) * LANES] for k in range(2)]
        a_pair = jnp.where(low_lane, a_head[0], a_head[1])
        a_edge = a_pair[edge:edge + 1, :]
        xdt = xs[:, p * LANES:(p + 1) * LANES] * dt_lanes[:, p * LANES:(p + 1) * LANES]
        hs = h_scr[p]
        m_both = jnp.concatenate(
            [(cbs[g] * jnp.exp2(jnp.where(live, a_head[k] - a_row[hl[k]:hl[k] + 1, :], NEG_BIG))
              ).astype(BF16) for k in range(2)], axis=1)
        x_both = jnp.concatenate([jnp.where(low_lane, xdt, 0.0).astype(BF16),
                                  jnp.where(low_lane, 0.0, xdt).astype(BF16)], axis=0)
        yd = _dot(m_both, x_both)
        yo = _dot_nt(cm_b[g], hs.astype(BF16)) * jnp.exp2(a_pair)
        y_pairs.append(yd + yo)
        st = _dot((xdt * jnp.exp2(a_edge - a_pair)).T.astype(BF16), bm_b[g])
        carry = [jnp.exp2(a_col[edge:edge + 1, l:l + 1]) for l in hl]
        h_scr[p] = hs * jnp.where(low_row, carry[0], carry[1]) + st
    return jnp.concatenate(y_pairs, axis=1), xs


def _make_ssd_kernel(nc):
    q = SSM_CHUNK
    halo = SUBLANES

    def kern(xp_ref, xm_ref, xn_ref, dt_ref, z_ref, cw_ref, cb_ref, dtb_ref, alog_ref,
             trif_ref, trib_ref, ewf_ref, ewb_ref, epf_ref, epb_ref,
             h0f_ref, h0b_ref, dsk_ref, ng_ref, _alias_ref,
             y_ref, hf_ref, hb_ref, ext_scr, xc_scr, yf_scr, h_scr):
        s = pl.program_id(1)

        @pl.when(s == 0)
        def _():
            h_scr[...] = h0f_ref[...]

        @pl.when(s == nc)
        def _():
            h_scr[...] = h0b_ref[...]

        @pl.when(s < nc)
        def _():
            ext_scr[0:halo, :] = jnp.where(s > 0, xp_ref[...], 0.0)
            ext_scr[halo:halo + q, :] = xm_ref[...]
            ext_scr[halo + q:2 * halo + q, :] = jnp.where(s < nc - 1, xn_ref[...], 0.0)
            ext = ext_scr[...]
            acc = cb_ref[...] + cw_ref[D_CONV // 2:D_CONV // 2 + 1, :] * ext[halo:halo + q]
            for k in range(D_CONV):
                if k != D_CONV // 2:
                    shifted = pltpu.roll(ext, (D_CONV // 2 - k) % (q + 2 * halo), 0)
                    acc = acc + cw_ref[k:k + 1, :] * shifted[halo:halo + q]
            xc = _silu(acc)
            r0 = pl.multiple_of(s * q, q)
            xc_scr[pl.ds(r0, q), :] = xc
            y, xs = _ssd_chunk(0, xc, dt_ref, dtb_ref, alog_ref, trif_ref, ewf_ref, epf_ref, h_scr)
            yf_scr[pl.ds(r0, q), :] = y + dsk_ref[...] * xs

            @pl.when(s == nc - 1)
            def _():
                hf_ref[...] = h_scr[...]

        @pl.when(s >= nc)
        def _():
            r0 = pl.multiple_of((2 * nc - 1 - s) * q, q)
            y, _ = _ssd_chunk(1, xc_scr[pl.ds(r0, q), :], dt_ref, dtb_ref, alog_ref, trib_ref,
                              ewb_ref, epb_ref, h_scr)
            gated = (yf_scr[pl.ds(r0, q), :] + y) * _silu(z_ref[...])
            gw = D_SSM // SSM_GROUPS
            for g in range(SSM_GROUPS):
                part = gated[:, g * gw:(g + 1) * gw]
                ms = jnp.mean(part * part, axis=-1, keepdims=True)
                y_ref[:, g * gw:(g + 1) * gw] = (
                    part * lax.rsqrt(ms + NORM_EPS) * ng_ref[:, g * gw:(g + 1) * gw]).astype(y_ref.dtype)

            @pl.when(s == 2 * nc - 1)
            def _():
                hb_ref[...] = h_scr[...]
    return kern


def _ssd(xbc, dt, z, lay, h0f, h0b, prev_out, *, nb, seq, row_off):
    t = xbc.shape[0]
    q = SSM_CHUNK
    nc = seq // q
    coff = row_off // q
    last_halo = t // SUBLANES - 1
    per = q // SUBLANES

    fwd_chunk = lambda b, s: coff + b * nc + jnp.minimum(s, nc - 1)
    any_chunk = lambda b, s: coff + b * nc + jnp.where(s < nc, s, 2 * nc - 1 - s)
    bwd_chunk = lambda b, s: coff + b * nc + jnp.where(s < nc, nc - 1, 2 * nc - 1 - s)

    full = lambda a: pl.BlockSpec(a.shape, lambda b, s: (0,) * a.ndim)
    prev = pl.BlockSpec((SUBLANES, XBC_DIM), lambda b, s: (jnp.maximum(fwd_chunk(b, s) * per - 1, 0), 0))
    nxt = pl.BlockSpec((SUBLANES, XBC_DIM),
                       lambda b, s: (jnp.minimum((fwd_chunk(b, s) + 1) * per, last_halo), 0))
    state = pl.BlockSpec((None, SSM_HEADS // 2, LANES, SSM_STATE), lambda b, s: (b, 0, 0, 0))
    out_spec = pl.BlockSpec((q, D_SSM), lambda b, s: (bwd_chunk(b, s), 0))
    consts = [lay["conv_w"], lay["conv_b"], lay["dt_bias"], lay["a_log"], lay["tri_fwd"], lay["tri_bwd"],
              lay["wide_fwd"], lay["wide_bwd"], lay["pair_fwd"], lay["pair_bwd"]]
    if prev_out is None:
        prev_out = jnp.zeros((t, D_SSM), BF16)
    state_shape = jax.ShapeDtypeStruct((nb, SSM_HEADS // 2, LANES, SSM_STATE), F32)
    return pl.pallas_call(
        _make_ssd_kernel(nc),
        grid=(nb, 2 * nc),
        in_specs=[prev, pl.BlockSpec((q, XBC_DIM), lambda b, s: (fwd_chunk(b, s), 0)), nxt,
                  pl.BlockSpec((q, LANES), lambda b, s: (any_chunk(b, s), 0)), out_spec]
                 + [full(a) for a in consts]
                 + [state, state, full(lay["d_skip"]), full(lay["ssm_norm_g"]),
                    pl.BlockSpec(memory_space=pl.ANY)],
        out_specs=[out_spec, state, state],
        out_shape=[jax.ShapeDtypeStruct((t, D_SSM), BF16), state_shape, state_shape],
        scratch_shapes=[pltpu.VMEM((q + 2 * SUBLANES, XBC_DIM), F32),
                        pltpu.VMEM((seq, XBC_DIM), F32),
                        pltpu.VMEM((seq, D_SSM), F32),
                        pltpu.VMEM((SSM_HEADS // 2, LANES, SSM_STATE), F32)],
        input_output_aliases={5 + len(consts) + 4: 0},
        compiler_params=_cparams(("arbitrary", "arbitrary"), 48),
        name="ssd",
    )(xbc, xbc, xbc, dt, z, *consts, h0f, h0b, lay["d_skip"], lay["ssm_norm_g"], prev_out)


def _make_outproj_kernel(n_x, npt):
    def kern(*refs):
        oa_ref, ob_ref, oc_ref = refs[:3]
        xs = refs[3:3 + n_x]
        gate_ref, w_ref, g_ref, sh_ref, sc_ref, xo_ref, h_ref = refs[3 + n_x:]
        na = ATT_HEADS * HEAD_DIM
        o = (_dot(oa_ref[...], w_ref[0:na, :]) + _dot(ob_ref[...], w_ref[na:na + D_SSM, :])
             + _dot(oc_ref[...], w_ref[na + D_SSM:, :]))
        xn = _pick_x(pl.program_id(0), npt, xs) + gate_ref[...] * o
        xo_ref[...] = xn
        h_ref[...] = _norm_mod(xn, g_ref[...], sh_ref[...], sc_ref[...]).astype(h_ref.dtype)
    return kern


def _outproj(oa, ob, oc, x, lay, modv, layer, dims, h_dtype):
    t, d = dims["tokens"], x[0].shape[1]
    npt = dims["prompt_tokens"] // TM
    row_of_tile = dims["row_of_tile"](TM)
    tok = lambda w: pl.BlockSpec((TM, w), lambda i: (i, 0))
    full = lambda a: pl.BlockSpec(a.shape, lambda i: (0,) * a.ndim)
    return pl.pallas_call(
        _make_outproj_kernel(len(x), npt),
        grid=(t // TM,),
        in_specs=[tok(oa.shape[1]), tok(ob.shape[1]), tok(oc.shape[1])] + _x_specs(x, TM, npt) + [
            _mod_spec(layer, 2, d, row_of_tile), full(lay["w_out"]), full(lay["norm2_g"]),
            _mod_spec(layer, 3, d, row_of_tile), _mod_spec(layer, 4, d, row_of_tile)],
        out_specs=[tok(d), tok(d)],
        out_shape=[jax.ShapeDtypeStruct((t, d), F32), jax.ShapeDtypeStruct((t, d), h_dtype)],
        compiler_params=_cparams(("arbitrary",), 40),
        name="outproj",
    )(oa, ob, oc, *x, modv, lay["w_out"], lay["norm2_g"], modv, modv)


def _ffn_kernel(h_ref, wg_ref, wu_ref, wd_ref, x_ref, gate_ref, o_ref, acc_ref):
    j = pl.program_id(1)
    h = h_ref[...]
    g = _dot(h, wg_ref[...])
    u = _dot(h, wu_ref[...])
    part = _dot((_silu(g) * u).astype(BF16), wd_ref[...])

    @pl.when(j == 0)
    def _():
        acc_ref[...] = part

    @pl.when(j > 0)
    def _():
        acc_ref[...] += part

    @pl.when(j == pl.num_programs(1) - 1)
    def _():
        o_ref[...] = x_ref[...] + gate_ref[...] * acc_ref[...]


def _dense_ffn(h, x, wg, wu, wd, modv, layer, dims):
    t, d = x.shape
    f = wg.shape[1]
    row_of_tile = dims["row_of_tile"](TM)
    gate_base = (layer * 6 + 5) * MOD_ROWS
    return pl.pallas_call(
        _ffn_kernel,
        grid=(t // TM, f // FFN_TF),
        in_specs=[pl.BlockSpec((TM, d), lambda i, j: (i, 0)),
                  pl.BlockSpec((d, FFN_TF), lambda i, j: (0, j)),
                  pl.BlockSpec((d, FFN_TF), lambda i, j: (0, j)),
                  pl.BlockSpec((FFN_TF, d), lambda i, j: (j, 0)),
                  pl.BlockSpec((TM, d), lambda i, j: (i, 0)),
                  pl.BlockSpec((None, 1, d), lambda i, j: (gate_base + row_of_tile(i), 0, 0))],
        out_specs=pl.BlockSpec((TM, d), lambda i, j: (i, 0)),
        out_shape=jax.ShapeDtypeStruct((t, d), F32),
        scratch_shapes=[pltpu.VMEM((TM, d), F32)],
        compiler_params=_cparams(("arbitrary", "arbitrary"), 56),
        name="dense_ffn",
    )(h, wg, wu, wd, x, modv)


def _route_kernel(h_ref, r_ref, tri_ref, info_ref, w_ref, cnt_ref, carry_ref):
    i = pl.program_id(0)

    @pl.when(i == 0)
    def _():
        carry_ref[...] = jnp.zeros_like(carry_ref)

    logits = _dot(h_ref[...], r_ref[...], HIGHEST)
    tm = logits.shape[0]
    lane = lax.broadcasted_iota(jnp.int32, (tm, LANES), 1)
    logits = jnp.where(lane < N_EXPERTS, logits, NEG_BIG)
    l1 = jnp.max(logits, axis=-1, keepdims=True)
    e1 = jnp.min(jnp.where(logits == l1, lane, LANES), axis=-1, keepdims=True)
    rest = jnp.where(lane == e1, NEG_BIG, logits)
    l2 = jnp.max(rest, axis=-1, keepdims=True)
    e2 = jnp.min(jnp.where(rest == l2, lane, LANES), axis=-1, keepdims=True)
    ex = jnp.exp(l2 - l1)
    w1 = 1.0 / (1.0 + ex)
    w2 = ex / (1.0 + ex)
    member = jnp.where((lane == e1) | (lane == e2), 1.0, 0.0)
    before = _dot(tri_ref[...], member.astype(BF16)) + carry_ref[...]
    r1 = jnp.sum(jnp.where(lane == e1, before, 0.0), axis=-1, keepdims=True)
    r2 = jnp.sum(jnp.where(lane == e2, before, 0.0), axis=-1, keepdims=True)
    info = jnp.where(lane == 0, e1.astype(F32),
                     jnp.where(lane == 1, e2.astype(F32),
                               jnp.where(lane == 2, r1, jnp.where(lane == 3, r2, 0.0))))
    info_ref[...] = info.astype(jnp.int32)
    w_ref[...] = jnp.where(lane == 0, w1, jnp.where(lane == 1, w2, 0.0))
    carry_ref[...] += jnp.sum(member, axis=0, keepdims=True)
    cnt_ref[...] = carry_ref[...]


def _route(h, router_pad, tri):
    t, d = h.shape
    return pl.pallas_call(
        _route_kernel,
        grid=(t // TM,),
        in_specs=[pl.BlockSpec((TM, d), lambda i: (i, 0)),
                  pl.BlockSpec(router_pad.shape, lambda i: (0, 0)),
                  pl.BlockSpec(tri.shape, lambda i: (0, 0))],
        out_specs=[pl.BlockSpec((TM, LANES), lambda i: (i, 0)),
                   pl.BlockSpec((TM, LANES), lambda i: (i, 0)),
                   pl.BlockSpec((1, LANES), lambda i: (0, 0))],
        out_shape=[jax.ShapeDtypeStruct((t, LANES), jnp.int32),
                   jax.ShapeDtypeStruct((t, LANES), F32),
                   jax.ShapeDtypeStruct((1, LANES), F32)],
        scratch_shapes=[pltpu.VMEM((1, LANES), F32)],
        compiler_params=_cparams(("arbitrary",), 40),
        name="moe_route",
    )(h, router_pad, tri)


def _row_copy(src, src_row, dst, dst_row, sem):
    return pltpu.make_async_copy(src.at[pl.ds(src_row, 1), :], dst.at[pl.ds(dst_row, 1), :], sem)


def _rows_copy(src, dst, n, sem):
    return pltpu.make_async_copy(src.at[pl.ds(0, n), :], dst.at[pl.ds(0, n), :], sem)


def _dispatch_kernel(dest_ref, pad_ref, h_ref, xb_ref, zero_ref, sem, zsem):
    i = pl.program_id(0)
    tm = h_ref.shape[0]
    base = i * tm * TOP_K

    def start(blk, carry):
        for u in range(DMA_UNROLL):
            r = blk * DMA_UNROLL + u
            for k in range(TOP_K):
                _row_copy(h_ref, r, xb_ref, dest_ref[base + r * TOP_K + k], sem).start()
        return carry

    lax.fori_loop(0, tm // DMA_UNROLL, start, 0)

    @pl.when(i == 0)
    def _():
        zero_ref[...] = jnp.zeros_like(zero_ref)
        for e in range(N_EXPERTS + 1):
            lo, hi = pad_ref[2 * e], pad_ref[2 * e + 1]
            lax.fori_loop(lo, hi, lambda r, c: (_row_copy(zero_ref, 0, xb_ref, r, zsem).start(), c)[1], 0)
        for e in range(N_EXPERTS + 1):
            lo, hi = pad_ref[2 * e], pad_ref[2 * e + 1]
            lax.fori_loop(lo, hi, lambda r, c: (_row_copy(zero_ref, 0, xb_ref, 0, zsem).wait(), c)[1], 0)

    for k in range(TOP_K):
        _rows_copy(h_ref, xb_ref, tm, sem).wait()


def _dispatch(h, dest_flat, pad_ranges, n_rows):
    t, d = h.shape
    return pl.pallas_call(
        _dispatch_kernel,
        grid_spec=pltpu.PrefetchScalarGridSpec(
            num_scalar_prefetch=2,
            grid=(t // TM,),
            in_specs=[pl.BlockSpec((TM, d), lambda i, *_: (i, 0))],
            out_specs=pl.BlockSpec(memory_space=pl.ANY),
            scratch_shapes=[pltpu.VMEM((SUBLANES, d), F32),
                            pltpu.SemaphoreType.DMA(()), pltpu.SemaphoreType.DMA(())]),
        out_shape=jax.ShapeDtypeStruct((n_rows, d), F32),
        compiler_params=_cparams(("arbitrary",), 40),
        name="moe_dispatch",
    )(dest_flat, pad_ranges, h)


def _expert_kernel(be_ref, bv_ref, x_ref, wg_ref, wu_ref, wd_ref, y_ref, xb_scr):
    i = pl.program_id(0)
    j = pl.program_id(1)

    @pl.when(bv_ref[i] > 0)
    def _():
        @pl.when(j == 0)
        def _():
            xb_scr[...] = x_ref[...].astype(BF16)

        x = xb_scr[...]
        g = _dot(x, wg_ref[...].astype(BF16))
        u = _dot(x, wu_ref[...].astype(BF16))
        part = _dot((_silu(g) * u).astype(BF16), wd_ref[...].astype(BF16))

        @pl.when(j == 0)
        def _():
            y_ref[...] = part

        @pl.when(j > 0)
        def _():
            y_ref[...] += part

    @pl.when((bv_ref[i] == 0) & (j == 0))
    def _():
        y_ref[...] = jnp.zeros_like(y_ref)


def _experts(xb, block_e, block_v, wg, wu, wd):
    n_rows, d = xb.shape
    f = wg.shape[2]

    def jj(i, j, bv):
        return jnp.where(bv[i] > 0, j, 0)

    return pl.pallas_call(
        _expert_kernel,
        grid_spec=pltpu.PrefetchScalarGridSpec(
            num_scalar_prefetch=2,
            grid=(n_rows // MOE_BM, f // MOE_TF),
            in_specs=[pl.BlockSpec((MOE_BM, d), lambda i, j, be, bv: (i, 0)),
                      pl.BlockSpec((None, d, MOE_TF), lambda i, j, be, bv: (be[i], 0, jj(i, j, bv))),
                      pl.BlockSpec((None, d, MOE_TF), lambda i, j, be, bv: (be[i], 0, jj(i, j, bv))),
                      pl.BlockSpec((None, MOE_TF, d), lambda i, j, be, bv: (be[i], jj(i, j, bv), 0))],
            out_specs=pl.BlockSpec((MOE_BM, d), lambda i, j, be, bv: (i, 0)),
            scratch_shapes=[pltpu.VMEM((MOE_BM, d), BF16)]),
        out_shape=jax.ShapeDtypeStruct((n_rows, d), F32),
        compiler_params=_cparams(("arbitrary", "arbitrary"), 56),
        name="moe_experts",
    )(block_e, block_v, xb, wg, wu, wd)


def _make_combine_kernel(npt):
    def kern(dest_ref, yb_ref, w_ref, x_ref, gate_ref, fg_ref, op_ref, os_ref, g0_ref, g1_ref, sem):
        i = pl.program_id(0)
        tm = x_ref.shape[0]
        base = i * tm * TOP_K
        bufs = (g0_ref, g1_ref)

        def start(blk, carry):
            for u in range(DMA_UNROLL):
                r = blk * DMA_UNROLL + u
                for k in range(TOP_K):
                    _row_copy(yb_ref, dest_ref[base + r * TOP_K + k], bufs[k], r, sem).start()
            return carry

        lax.fori_loop(0, tm // DMA_UNROLL, start, 0)
        for k in range(TOP_K):
            _rows_copy(yb_ref, bufs[k], tm, sem).wait()
        w = w_ref[...]
        y = w[:, 0:1] * g0_ref[...] + w[:, 1:2] * g1_ref[...]
        xn = x_ref[...] + gate_ref[...] * y
        ms = jnp.mean(xn * xn, axis=-1, keepdims=True)
        out = xn * lax.rsqrt(ms + NORM_EPS) * fg_ref[...]

        @pl.when(i < npt)
        def _():
            op_ref[...] = out

        @pl.when(i >= npt)
        def _():
            os_ref[...] = out
    return kern


def _combine(yb, dest_flat, top_w, x, modv, layer, final_g, dims):
    t, d = x.shape
    tm = 256
    tp = dims["prompt_tokens"]
    npt = tp // tm
    row_of_tile = dims["row_of_tile"](tm)
    gate_base = (layer * 6 + 5) * MOD_ROWS
    return pl.pallas_call(
        _make_combine_kernel(npt),
        grid_spec=pltpu.PrefetchScalarGridSpec(
            num_scalar_prefetch=1,
            grid=(t // tm,),
            in_specs=[pl.BlockSpec(memory_space=pl.ANY),
                      pl.BlockSpec((tm, LANES), lambda i, *_: (i, 0)),
                      pl.BlockSpec((tm, d), lambda i, *_: (i, 0)),
                      pl.BlockSpec((None, 1, d), lambda i, *_: (gate_base + row_of_tile(i), 0, 0)),
                      pl.BlockSpec((1, d), lambda i, *_: (0, 0))],
            out_specs=[pl.BlockSpec((tm, d), lambda i, *_: (jnp.minimum(i, npt - 1), 0)),
                       pl.BlockSpec((tm, d), lambda i, *_: (jnp.maximum(i - npt, 0), 0))],
            scratch_shapes=[pltpu.VMEM((tm, d), F32), pltpu.VMEM((tm, d), F32),
                            pltpu.SemaphoreType.DMA(())]),
        out_shape=[jax.ShapeDtypeStruct((tp, d), F32), jax.ShapeDtypeStruct((t - tp, d), F32)],
        compiler_params=_cparams(("arbitrary",), 40),
        name="moe_combine",
    )(dest_flat, yb, top_w, x, modv, final_g)


def _moe_layer(h, x, router, wg, wu, wd, modv, layer, final_g, dims):
    t, d = x.shape
    router_pad = jnp.zeros((d, LANES), F32).at[:, :N_EXPERTS].set(router)
    info, top_w, counts = _route(h, router_pad, dims["tri_strict"])
    counts = counts[0, :N_EXPERTS].astype(jnp.int32)
    padded = (counts + MOE_BM - 1) // MOE_BM * MOE_BM
    pad_end = jnp.cumsum(padded)
    pad_start = pad_end - padded
    n_rows = t * TOP_K + N_EXPERTS * MOE_BM
    dest = pad_start[info[:, 0:TOP_K]] + info[:, TOP_K:2 * TOP_K]
    dest_flat = dest.reshape(t * TOP_K).astype(jnp.int32)
    lo = jnp.concatenate([pad_start + counts, pad_end[-1:]])
    hi = jnp.concatenate([pad_end, jnp.array([n_rows], jnp.int32)])
    pad_ranges = jnp.stack([lo, hi], axis=1).reshape(-1).astype(jnp.int32)
    block_start = jnp.arange(n_rows // MOE_BM, dtype=jnp.int32) * MOE_BM
    block_e = jnp.sum(pad_end[None, :] <= block_start[:, None], axis=1).astype(jnp.int32)
    block_v = (block_e < N_EXPERTS).astype(jnp.int32)
    last_e = jnp.max(jnp.where(padded > 0, jnp.arange(N_EXPERTS), 0)).astype(jnp.int32)
    block_e = jnp.where(block_v > 0, block_e, last_e)
    xb = _dispatch(h, dest_flat, pad_ranges, n_rows)
    yb = _experts(xb, block_e, block_v, wg, wu, wd)
    return _combine(yb, dest_flat, top_w, x, modv, layer, final_g, dims)


def _rope_tables(dec_seq):
    n_rows = dec_seq // GRID_W
    row = jnp.repeat(jnp.arange(n_rows, dtype=F32), GRID_W)
    col = (jnp.arange(dec_seq) % GRID_W).astype(F32)
    nf = HEAD_DIM // 4
    inv = ROPE_THETA ** (-jnp.arange(nf, dtype=F32) / nf)
    ang = jnp.stack([row[:, None] * inv, col[:, None] * inv], axis=1)
    cos = jnp.cos(ang)
    sin = jnp.sin(ang)
    cos_h = jnp.concatenate([cos, cos], axis=-1).reshape(dec_seq, HEAD_DIM)
    sin_h = jnp.concatenate([-sin, sin], axis=-1).reshape(dec_seq, HEAD_DIM)
    cos_t = jnp.concatenate([cos_h, cos_h], axis=-1)
    sin_t = jnp.concatenate([sin_h, sin_h], axis=-1)
    cos_t = jnp.concatenate([cos_t, jnp.ones((TM, LANES), F32)], axis=0)
    sin_t = jnp.concatenate([sin_t, jnp.zeros((TM, LANES), F32)], axis=0)
    return cos_t, sin_t


def _expansion(direction, width):
    src = direction * SSM_HEADS + np.arange(SSM_HEADS * width) // width
    return jnp.asarray((np.arange(LANES)[:, None] == src[None, :]).astype(np.float32)).astype(BF16)


def _layer_params(l, norm1_g, norm2_g, w_in, q_norm_g, k_norm_g, conv_w, conv_b, a_log, dt_bias,
                  d_skip, ssm_norm_g, sink, w_out):
    d = w_in.shape[1]
    pad = lambda v: jnp.zeros((1, LANES), F32).at[0, :v.size].set(v.reshape(-1))
    seg = np.arange(LANES) // HEAD_DIM
    bd = jnp.asarray((seg[:, None] == seg[None, :]).astype(np.float32) / HEAD_DIM).astype(BF16)
    r = np.arange(SSM_CHUNK)
    return {
        "norm1_g": norm1_g[l].reshape(1, d), "norm2_g": norm2_g[l].reshape(1, d),
        "w_main": w_in[l, :, :OFF_DT].astype(BF16),
        "w_dt": jnp.zeros((d, LANES), F32).at[:, :2 * SSM_HEADS].set(w_in[l, :, OFF_DT:]).astype(BF16),
        "bd": bd,
        "q_norm_g": jnp.tile(q_norm_g[l], 2).reshape(1, LANES),
        "k_norm_g": jnp.tile(k_norm_g[l], 2).reshape(1, LANES),
        "conv_w": jnp.zeros((SUBLANES, XBC_DIM), F32).at[:D_CONV].set(conv_w[l]),
        "conv_b": conv_b[l].reshape(1, XBC_DIM),
        "a_log": pad(a_log[l]), "dt_bias": pad(dt_bias[l]),
        "d_skip": jnp.repeat(d_skip[l], SSM_HEAD_DIM).reshape(1, D_SSM),
        "ssm_norm_g": ssm_norm_g[l].reshape(1, D_SSM),
        "sink": sink[l],
        "w_out": w_out[l].astype(BF16),
        "tri_fwd": jnp.asarray((r[:, None] >= r[None, :]).astype(np.float32)).astype(BF16),
        "tri_bwd": jnp.asarray((r[:, None] <= r[None, :]).astype(np.float32)).astype(BF16),
        "wide_fwd": _expansion(0, LANES), "wide_bwd": _expansion(1, LANES),
        "pair_fwd": _expansion(0, SSM_HEAD_DIM), "pair_bwd": _expansion(1, SSM_HEAD_DIM),
    }


def kernel(x_prompt, x_sample, cache_attn_k, cache_attn_v, cache_win_k, cache_win_v, state_ssm_fwd, state_ssm_bwd, c, c_ctx, norm1_g, norm2_g, w_mod, b_mod, w_in, q_norm_g, k_norm_g, conv_w, conv_b, a_log, dt_bias, d_skip, ssm_norm_g, sink, w_out, ffn_w_gate, ffn_w_up, ffn_w_down, moe_router, moe_w_gate, moe_w_up, moe_w_down, final_g):
    batch, seq, d = x_prompt.shape
    dec_batch, dec_seq, _ = x_sample.shape
    depth = w_in.shape[0]
    past = cache_attn_k.shape[2]
    tp = batch * seq
    t = tp + dec_batch * dec_seq
    assert depth == 2 and tp % (2 * TM) == 0 and dec_seq % (2 * TM) == 0
    assert 1 + dec_batch <= MOD_ROWS and seq % SSM_CHUNK == 0 and tp % dec_seq == 0

    def row_of_tile(tm):
        npt, tpb = tp // tm, dec_seq // tm
        return lambda i: jnp.where(i < npt, 0, 1 + (i - npt) // tpb)

    cos_t, sin_t = _rope_tables(dec_seq)
    rr = np.arange(TM)
    dims = {"tokens": t, "prompt_tokens": tp, "dec_seq": dec_seq, "row_of_tile": row_of_tile,
            "rope_cos": cos_t, "rope_sin": sin_t,
            "tri_strict": jnp.asarray((rr[:, None] > rr[None, :]).astype(np.float32)).astype(BF16)}

    cc = jnp.zeros((MOD_ROWS, d), F32).at[0].set(c_ctx).at[1:1 + dec_batch].set(c)
    mod = _mod_vectors(cc, w_mod, b_mod)
    modv = mod.reshape(depth, MOD_ROWS, 6, d).transpose(0, 2, 1, 3).reshape(depth * 6 * MOD_ROWS, 1, d)

    x = (x_prompt.reshape(tp, d), x_sample.reshape(dec_batch * dec_seq, d))
    zero_state = jnp.zeros((batch, SSM_HEADS // 2, LANES, SSM_STATE), F32)
    caches = [[] for _ in range(6)]
    y = None
    for l in range(depth):
        lay = _layer_params(l, norm1_g, norm2_g, w_in, q_norm_g, k_norm_g, conv_w, conv_b, a_log,
                            dt_bias, d_skip, ssm_norm_g, sink, w_out)
        (qa, ka, va, qc, kc, vc, z, xbc, dt, cka, cva, ckc, cvc) = _inproj(x, lay, modv, l, dims)
        ctx = lambda a: a[:, l].reshape(dec_batch, past, LANES).astype(BF16)

        oa = _attention(qa, ka, va, None, None, None, None, nb=batch, seq=seq, tq=seq, row_off=0,
                        window=False, name="attn_a_ctx")
        oa = _attention(qa, ka, va, ctx(cache_attn_k), ctx(cache_attn_v), None, oa, nb=dec_batch,
                        seq=dec_seq, tq=TQ_GLOBAL, row_off=tp, window=False, name="attn_a_lat")
        oc = _attention(qc, kc, vc, None, None, lay["sink"], None, nb=batch, seq=seq, tq=seq,
                        row_off=0, window=False, name="attn_c_ctx")
        oc = _attention(qc, kc, vc, ctx(cache_win_k), ctx(cache_win_v), lay["sink"], oc,
                        nb=dec_batch, seq=dec_seq, tq=TQ_WINDOW, row_off=tp, window=True,
                        name="attn_c_lat")
        pair = lambda s: s[:, l].reshape(dec_batch, SSM_HEADS // 2, LANES, SSM_STATE)
        ob, hf_p, hb_p = _ssd(xbc, dt, z, lay, zero_state, zero_state, None, nb=batch, seq=seq,
                              row_off=0)
        ob, _, _ = _ssd(xbc, dt, z, lay, pair(state_ssm_fwd), pair(state_ssm_bwd), ob, nb=dec_batch,
                        seq=dec_seq, row_off=tp)

        kv = lambda a: a[:tp].reshape(batch, seq, ATT_KV_HEADS, HEAD_DIM)
        st = lambda s: s.reshape(batch, SSM_HEADS, SSM_HEAD_DIM, SSM_STATE)
        for lst, val in zip(caches, (kv(cka), kv(cva), kv(ckc), kv(cvc), st(hf_p), st(hb_p))):
            lst.append(val)

        i = l // 2
        if l % 2 == 0:
            xr, h2 = _outproj(oa, ob, oc, x, lay, modv, l, dims, BF16)
            x = (_dense_ffn(h2, xr, ffn_w_gate[i].astype(BF16), ffn_w_up[i].astype(BF16),
                            ffn_w_down[i].astype(BF16), modv, l, dims),)
        else:
            xr, h2 = _outproj(oa, ob, oc, x, lay, modv, l, dims, F32)
            y = _moe_layer(h2, xr, moe_router[i], moe_w_gate[i], moe_w_up[i], moe_w_down[i], modv, l,
                           final_g.reshape(1, d), dims)

    y_prompt = y[0].reshape(batch, seq, d)
    y_sample = y[1].reshape(dec_batch, dec_seq, d)
    return (y_prompt, y_sample) + tuple(jnp.stack(lst, axis=1) for lst in caches)
```

```python
import numpy as np
import jax
import jax.numpy as jnp
from jax import lax
from jax.experimental import pallas as pl
from jax.experimental.pallas import tpu as pltpu

F32 = jnp.float32
BF16 = jnp.bfloat16
HIGHEST = lax.Precision.HIGHEST

GRID_W = 64
HEAD_DIM = 64
ATT_HEADS = 4
ATT_KV_HEADS = 2
WIN_HEADS = 4
WIN_KV_HEADS = 2
WINDOW = 128
ROPE_THETA = 10000.0
SSM_HEADS = 8
SSM_HEAD_DIM = 64
D_SSM = SSM_HEADS * SSM_HEAD_DIM
SSM_GROUPS = 2
SSM_STATE = 128
D_CONV = 5
SSM_CHUNK = 128
XBC_DIM = D_SSM + 2 * SSM_GROUPS * SSM_STATE
N_EXPERTS = 8
TOP_K = 2
NORM_EPS = 1e-6
NEG_BIG = -1e30
LOG2E = 1.4426950408889634

LANES = 128
SUBLANES = 8
MOD_ROWS = 16

OFF_QA, OFF_KA, OFF_VA, OFF_QC, OFF_KC, OFF_VC, OFF_Z, OFF_XBC, OFF_DT = (
    0, 256, 384, 512, 768, 896, 1024, 1536, 2560)

TM = 512
TQ_GLOBAL = 256
TQ_WINDOW = 256
ATTN_KEY_CHUNK = 256
ATTN_KEY_CHUNK_WINDOW = 512
MOE_BM = 1024
MOE_TF = 512
SSD_STEP = 512
FFN_SLICE = 256
DMA_UNROLL = 8


def _cparams(sem, vmem_mb):
    return pltpu.CompilerParams(dimension_semantics=sem, vmem_limit_bytes=vmem_mb * 1024 * 1024)


def _dot(a, b, precision=None):
    return jnp.dot(a, b, preferred_element_type=F32, precision=precision)


def _dot_nt(a, b):
    return lax.dot_general(a, b, (((1,), (1,)), ((), ())), preferred_element_type=F32)


def _silu(x):
    return x / (1.0 + jnp.exp(-x))


def _norm_mod(x, g, shift, scale):
    ms = jnp.mean(x * x, axis=-1, keepdims=True)
    y = x * lax.rsqrt(ms + NORM_EPS) * g
    return y * (1.0 + scale) + shift


def _pick_x(i, npt, xs):
    if len(xs) == 1:
        return xs[0][...]
    return jnp.where(i < npt, xs[0][...], xs[1][...])


def _x_specs(x, tm, npt):
    d = x[0].shape[1]
    if len(x) == 1:
        return [pl.BlockSpec((tm, d), lambda i, *_: (i, 0))]
    return [pl.BlockSpec((tm, d), lambda i, *_: (jnp.minimum(i, npt - 1), 0)),
            pl.BlockSpec((tm, d), lambda i, *_: (jnp.maximum(i - npt, 0), 0))]


def _mod_kernel(c_ref, w_ref, b_ref, o_ref):
    a = _silu(c_ref[...])
    o_ref[...] = _dot(a, w_ref[...], HIGHEST) + b_ref[...]


def _mod_vectors(cc, w_mod, b_mod):
    depth, d, n = w_mod.shape
    tn = 1536
    return pl.pallas_call(
        _mod_kernel,
        grid=(depth, n // tn),
        in_specs=[pl.BlockSpec((MOD_ROWS, d), lambda l, j: (0, 0)),
                  pl.BlockSpec((None, d, tn), lambda l, j: (l, 0, j)),
                  pl.BlockSpec((None, 1, tn), lambda l, j: (l, 0, j))],
        out_specs=pl.BlockSpec((None, MOD_ROWS, tn), lambda l, j: (l, 0, j)),
        out_shape=jax.ShapeDtypeStruct((depth, MOD_ROWS, n), F32),
        compiler_params=_cparams(("arbitrary", "arbitrary"), 40),
        name="mod_vectors",
    )(cc, w_mod, b_mod.reshape(depth, 1, n))


def _mod_spec(layer, k, d, row_of_tile):
    base = (layer * 6 + k) * MOD_ROWS
    return pl.BlockSpec((None, 1, d), lambda i, *_: (base + row_of_tile(i), 0, 0))


def _make_inproj_kernel(n_x, npt):
    def kern(*refs):
        xs = refs[:n_x]
        (g_ref, sh_ref, sc_ref, w_ref, wdt_ref, bd_ref, qg_ref, kg_ref, cos_ref, sin_ref,
         qa_ref, ka_ref, va_ref, qc_ref, kc_ref, vc_ref, z_ref, xbc_ref, dt_ref,
         cka_ref, cva_ref, ckc_ref, cvc_ref) = refs[n_x:]
        x = _pick_x(pl.program_id(0), npt, xs)
        h = _norm_mod(x, g_ref[...], sh_ref[...], sc_ref[...]).astype(BF16)

        def proj(a, b):
            return _dot(h, w_ref[:, a:b])

        cos = cos_ref[...]
        sin = sin_ref[...]
        tm = cos.shape[0]
        lane = lax.broadcasted_iota(jnp.int32, (tm, LANES), 1)
        first = (lane % 32) < 16
        low = lane < HEAD_DIM
        bd = bd_ref[...]

        def rope(v):
            partner = jnp.where(first, pltpu.roll(v, LANES - 16, 1), pltpu.roll(v, 16, 1))
            return v * cos + partner * sin

        def head_norm(v, g):
            sq = v * v
            hi = sq.astype(BF16)
            lo = (sq - hi.astype(F32)).astype(BF16)
            ms = _dot(hi, bd) + _dot(lo, bd)
            return v * lax.rsqrt(ms + NORM_EPS) * g

        def stack_heads(c0, c1, out_ref):
            zero = jnp.zeros_like(c0)
            out_ref[0] = jnp.where(low, c0, zero).astype(out_ref.dtype)
            out_ref[1] = jnp.where(low, pltpu.roll(c0, HEAD_DIM, 1), zero).astype(out_ref.dtype)
            out_ref[2] = jnp.where(low, zero, pltpu.roll(c1, HEAD_DIM, 1)).astype(out_ref.dtype)
            out_ref[3] = jnp.where(low, zero, c1).astype(out_ref.dtype)

        scale = LOG2E * HEAD_DIM ** -0.5
        qg = qg_ref[...]
        q_a = proj(OFF_QA, OFF_KA)
        qa0 = rope(head_norm(q_a[:, 0:LANES], qg)) * scale
        qa1 = rope(head_norm(q_a[:, LANES:2 * LANES], qg)) * scale
        stack_heads(qa0, qa1, qa_ref)
        kv_a = proj(OFF_KA, OFF_QC)
        ka = rope(head_norm(kv_a[:, 0:LANES], kg_ref[...]))
        ka_ref[...] = ka.astype(ka_ref.dtype)
        cka_ref[...] = ka
        va = kv_a[:, LANES:2 * LANES]
        va_ref[...] = va.astype(va_ref.dtype)
        cva_ref[...] = va

        q_c = proj(OFF_QC, OFF_KC)
        qc0 = rope(q_c[:, 0:LANES]) * scale
        qc1 = rope(q_c[:, LANES:2 * LANES]) * scale
        stack_heads(qc0, qc1, qc_ref)
        kv_c = proj(OFF_KC, OFF_Z)
        kc = rope(kv_c[:, 0:LANES])
        kc_ref[...] = kc.astype(kc_ref.dtype)
        ckc_ref[...] = kc
        vc = kv_c[:, LANES:2 * LANES]
        vc_ref[...] = vc.astype(vc_ref.dtype)
        cvc_ref[...] = vc

        z_ref[...] = proj(OFF_Z, OFF_XBC)
        xbc_ref[...] = proj(OFF_XBC, OFF_DT)
        dt_ref[...] = _dot(h, wdt_ref[...])
    return kern


def _inproj(x, lay, modv, layer, dims):
    t, d = dims["tokens"], x[0].shape[1]
    n_tiles = t // TM
    npt = dims["prompt_tokens"] // TM
    tpb = dims["dec_seq"] // TM
    row_of_tile = dims["row_of_tile"](TM)

    def rope_idx(i):
        return jnp.where(i < npt, tpb, (i - npt) % tpb)

    def cache_idx(i):
        return jnp.minimum(i, npt)

    tok = lambda w: pl.BlockSpec((TM, w), lambda i: (i, 0))
    full = lambda a: pl.BlockSpec(a.shape, lambda i: (0,) * a.ndim)
    stack = pl.BlockSpec((4, TM, LANES), lambda i: (0, i, 0))
    cache = pl.BlockSpec((TM, LANES), lambda i: (cache_idx(i), 0))
    rope = pl.BlockSpec((TM, LANES), lambda i: (rope_idx(i), 0))
    cache_shape = jax.ShapeDtypeStruct(((npt + 1) * TM, LANES), F32)
    return pl.pallas_call(
        _make_inproj_kernel(len(x), npt),
        grid=(n_tiles,),
        in_specs=_x_specs(x, TM, npt) + [
            full(lay["norm1_g"]),
            _mod_spec(layer, 0, d, row_of_tile), _mod_spec(layer, 1, d, row_of_tile),
            full(lay["w_main"]), full(lay["w_dt"]), full(lay["bd"]),
            full(lay["q_norm_g"]), full(lay["k_norm_g"]), rope, rope],
        out_specs=[stack, tok(LANES), tok(LANES), stack, tok(LANES), tok(LANES),
                   tok(D_SSM), tok(XBC_DIM), tok(LANES), cache, cache, cache, cache],
        out_shape=[jax.ShapeDtypeStruct((4, t, LANES), BF16),
                   jax.ShapeDtypeStruct((t, LANES), BF16), jax.ShapeDtypeStruct((t, LANES), BF16),
                   jax.ShapeDtypeStruct((4, t, LANES), BF16),
                   jax.ShapeDtypeStruct((t, LANES), BF16), jax.ShapeDtypeStruct((t, LANES), BF16),
                   jax.ShapeDtypeStruct((t, D_SSM), F32), jax.ShapeDtypeStruct((t, XBC_DIM), F32),
                   jax.ShapeDtypeStruct((t, LANES), F32),
                   cache_shape, cache_shape, cache_shape, cache_shape],
        compiler_params=_cparams(("arbitrary",), 48),
        name="inproj",
    )(*x, lay["norm1_g"], modv, modv, lay["w_main"], lay["w_dt"], lay["bd"],
      lay["q_norm_g"], lay["k_norm_g"], dims["rope_cos"], dims["rope_sin"])


def _make_attn_kernel(tq, n_ctx, n_lat, window_len, has_sink):
    ck = ATTN_KEY_CHUNK_WINDOW if window_len else ATTN_KEY_CHUNK

    def kern(*refs):
        refs = list(refs)
        sink_ref = refs.pop(0) if has_sink else None
        q_ref = refs.pop(0)
        kc_ref = refs.pop(0) if n_ctx else None
        vc_ref = refs.pop(0) if n_ctx else None
        kl_ref, vl_ref = refs.pop(0), refs.pop(0)
        refs.pop(0)
        o_ref, vaug_ref = refs

        @pl.when(pl.program_id(1) == 0)
        def _():
            if n_ctx:
                vaug_ref[0:n_ctx, 0:LANES] = vc_ref[...]
            vaug_ref[n_ctx:n_ctx + n_lat, 0:LANES] = vl_ref[...]
            vaug_ref[:, LANES:2 * LANES] = jnp.ones((n_ctx + n_lat, LANES), BF16)

        rows = 4 * tq
        q = q_ref[...].reshape(rows, LANES)
        if has_sink:
            head = lax.broadcasted_iota(jnp.int32, (rows, 1), 0) // tq
            snk = LOG2E * jnp.where(head == 0, sink_ref[0],
                                    jnp.where(head == 1, sink_ref[1],
                                              jnp.where(head == 2, sink_ref[2], sink_ref[3])))
            m = snk
        else:
            m = jnp.full((rows, 1), NEG_BIG, F32)
        acc = jnp.zeros((rows, 2 * LANES), F32)

        chunks = []
        for c0 in range(0, n_ctx, ck):
            n = min(ck, n_ctx - c0)
            chunks.append((kc_ref[c0:c0 + n, :], vaug_ref[c0:c0 + n, :], None))
        if window_len:
            q0 = pl.program_id(1) * tq
            ws = pl.multiple_of(jnp.clip(q0 - WINDOW, 0, n_lat - window_len), LANES)
            qpos = q0 + lax.broadcasted_iota(jnp.int32, (rows, window_len), 0) % tq
            kpos = ws + lax.broadcasted_iota(jnp.int32, (rows, window_len), 1)
            valid = jnp.abs(qpos - kpos) <= WINDOW
            chunks.append((kl_ref[pl.ds(ws, window_len), :],
                           vaug_ref[pl.ds(n_ctx + ws, window_len), :], valid))
        else:
            for c0 in range(0, n_lat, ck):
                n = min(ck, n_lat - c0)
                chunks.append((kl_ref[c0:c0 + n, :], vaug_ref[n_ctx + c0:n_ctx + c0 + n, :], None))

        for keys, vaug, valid in chunks:
            s = _dot_nt(q, keys)
            if valid is not None:
                s = jnp.where(valid, s, NEG_BIG)
            m_new = jnp.maximum(m, jnp.max(s, axis=-1, keepdims=True))
            p = jnp.exp2(s - m_new).astype(BF16)
            acc = acc * jnp.exp2(m - m_new) + _dot(p, vaug)
            m = m_new

        den = acc[:, LANES:2 * LANES]
        if has_sink:
            den = den + jnp.exp2(snk - m)
        o = acc[:, 0:LANES] / den
        low = lax.broadcasted_iota(jnp.int32, (tq, LANES), 1) < HEAD_DIM
        o0, o1, o2, o3 = (o[h * tq:(h + 1) * tq] for h in range(4))
        c0 = jnp.where(low, o0, pltpu.roll(o1, HEAD_DIM, 1))
        c1 = jnp.where(low, pltpu.roll(o2, HEAD_DIM, 1), o3)
        o_ref[:, 0:LANES] = c0.astype(o_ref.dtype)
        o_ref[:, LANES:2 * LANES] = c1.astype(o_ref.dtype)
    return kern


def _attention(q_stack, k_lat, v_lat, k_ctx, v_ctx, sink, prev_out, *, nb, seq, tq, row_off,
               window, name):
    t = q_stack.shape[1]
    nq = seq // tq
    qoff = row_off // tq
    boff = row_off // seq
    n_ctx = 0 if k_ctx is None else k_ctx.shape[1]
    window_len = min(seq, tq + 2 * WINDOW) if window else 0
    kern = _make_attn_kernel(tq, n_ctx, seq, window_len, sink is not None)
    in_specs, args = [], []
    if sink is not None:
        in_specs.append(pl.BlockSpec(memory_space=pltpu.SMEM))
        args.append(sink)
    in_specs.append(pl.BlockSpec((4, tq, LANES), lambda b, j: (0, qoff + b * nq + j, 0)))
    args.append(q_stack)
    if n_ctx:
        ctx_spec = pl.BlockSpec((None, n_ctx, LANES), lambda b, j: (b, 0, 0))
        in_specs += [ctx_spec, ctx_spec]
        args += [k_ctx, v_ctx]
    lat_spec = pl.BlockSpec((seq, LANES), lambda b, j: (boff + b, 0))
    in_specs += [lat_spec, lat_spec]
    args += [k_lat, v_lat]
    if prev_out is None:
        prev_out = jnp.zeros((t, 2 * LANES), BF16)
    in_specs.append(pl.BlockSpec(memory_space=pl.ANY))
    args.append(prev_out)
    aliases = {len(args) - 1: 0}
    return pl.pallas_call(
        kern,
        grid=(nb, nq),
        in_specs=in_specs,
        out_specs=pl.BlockSpec((tq, 2 * LANES), lambda b, j: (qoff + b * nq + j, 0)),
        out_shape=jax.ShapeDtypeStruct((t, 2 * LANES), BF16),
        scratch_shapes=[pltpu.VMEM((n_ctx + seq, 2 * LANES), BF16)],
        input_output_aliases=aliases,
        compiler_params=_cparams(("arbitrary", "arbitrary"), 48),
        name=name,
    )(*args)


def _split_bf16(v, n):
    parts = []
    for _ in range(n):
        part = v.astype(BF16)
        parts.append(part)
        v = v - part.astype(F32)
    return parts


def _dot_parts(a, parts):
    out = _dot(a, parts[0])
    for part in parts[1:]:
        out = out + _dot(a, part)
    return out


def _parts_dot(parts, b):
    out = _dot(parts[0], b)
    for part in parts[1:]:
        out = out + _dot(part, b)
    return out


def _ssd_chunk(direction, xc, dt_raw, dtb_ref, alog_ref, tri_ref, ewide_ref, epair_ref, h_scr):
    q = SSM_CHUNK
    xs = xc[:, 0:D_SSM]
    bm = xc[:, D_SSM:D_SSM + SSM_GROUPS * SSM_STATE]
    cm = xc[:, D_SSM + SSM_GROUPS * SSM_STATE:]

    dtx = dt_raw + dtb_ref[...]
    dt = jnp.maximum(dtx, 0.0) + jnp.log1p(jnp.exp(-jnp.abs(dtx)))
    dta = dt * (-LOG2E * jnp.exp(alog_ref[...]))
    a_col = _dot_parts(tri_ref[...], _split_bf16(dta, 3))
    a_row = a_col.T
    a_wide = _parts_dot(_split_bf16(a_col, 3), ewide_ref[...])
    dt_lanes = _parts_dot(_split_bf16(dt, 2), epair_ref[...])
    row = lax.broadcasted_iota(jnp.int32, (q, q), 0)
    col = lax.broadcasted_iota(jnp.int32, (q, q), 1)
    live = (row >= col) if direction == 0 else (col >= row)
    low_lane = col < SSM_HEAD_DIM
    low_row = row < SSM_HEAD_DIM
    edge = q - 1 if direction == 0 else 0

    group = lambda v, g: v[:, g * SSM_STATE:(g + 1) * SSM_STATE]
    cm_b = [group(cm, g).astype(BF16) for g in range(SSM_GROUPS)]
    bm_b = [group(bm, g).astype(BF16) for g in range(SSM_GROUPS)]
    cbs = [_dot_nt(cm_b[g], bm_b[g]) for g in range(SSM_GROUPS)]

    rep = SSM_HEADS // SSM_GROUPS
    y_pairs = []
    for p in range(SSM_HEADS // 2):
        g = (2 * p) // rep
        hl = [direction * SSM_HEADS + 2 * p, direction * SSM_HEADS + 2 * p + 1]
        a_head = [a_wide[:, (2 * p + k) * LANES:(2 * p + k + 1) * LANES] for k in range(2)]
        a_pair = jnp.where(low_lane, a_head[0], a_head[1])
        a_edge = a_pair[edge:edge + 1, :]
        xdt = xs[:, p * LANES:(p + 1) * LANES] * dt_lanes[:, p * LANES:(p + 1) * LANES]
        hs = h_scr[p]
        m_both = jnp.concatenate(
            [(cbs[g] * jnp.exp2(jnp.where(live, a_head[k] - a_row[hl[k]:hl[k] + 1, :], NEG_BIG))
              ).astype(BF16) for k in range(2)], axis=1)
        x_both = jnp.concatenate([jnp.where(low_lane, xdt, 0.0).astype(BF16),
                                  jnp.where(low_lane, 0.0, xdt).astype(BF16)], axis=0)
        yd = _dot(m_both, x_both)
        yo = _dot_nt(cm_b[g], hs.astype(BF16)) * jnp.exp2(a_pair)
        y_pairs.append(yd + yo)
        st = _dot((xdt * jnp.exp2(a_edge - a_pair)).T.astype(BF16), bm_b[g])
        carry = [jnp.exp2(a_col[edge:edge + 1, l:l + 1]) for l in hl]
        h_scr[p] = hs * jnp.where(low_row, carry[0], carry[1]) + st
    return jnp.concatenate(y_pairs, axis=1), xs


def _make_ssd_kernel(ns, step):
    q = SSM_CHUNK
    halo = SUBLANES
    n_sub = step // q

    def kern(xp_ref, xm_ref, xn_ref, dt_ref, z_ref, cw_ref, cb_ref, dtb_ref, alog_ref,
             trif_ref, trib_ref, ewf_ref, ewb_ref, epf_ref, epb_ref,
             h0f_ref, h0b_ref, dsk_ref, ng_ref, _alias_ref,
             y_ref, hf_ref, hb_ref, ext_scr, xc_scr, yf_scr, h_scr):
        s = pl.program_id(1)

        @pl.when(s == 0)
        def _():
            h_scr[...] = h0f_ref[...]

        @pl.when(s == ns)
        def _():
            h_scr[...] = h0b_ref[...]

        @pl.when(s < ns)
        def _():
            ext_scr[0:halo, :] = jnp.where(s > 0, xp_ref[...], 0.0)
            ext_scr[halo:halo + step, :] = xm_ref[...]
            ext_scr[halo + step:2 * halo + step, :] = jnp.where(s < ns - 1, xn_ref[...], 0.0)
            ext = ext_scr[...]
            acc = cb_ref[...] + cw_ref[D_CONV // 2:D_CONV // 2 + 1, :] * ext[halo:halo + step]
            for k in range(D_CONV):
                if k != D_CONV // 2:
                    shifted = pltpu.roll(ext, (D_CONV // 2 - k) % (step + 2 * halo), 0)
                    acc = acc + cw_ref[k:k + 1, :] * shifted[halo:halo + step]
            xc = _silu(acc)
            r0 = pl.multiple_of(s * step, step)
            xc_scr[pl.ds(r0, step), :] = xc
            for c in range(n_sub):
                rows = slice(c * q, (c + 1) * q)
                y, xs = _ssd_chunk(0, xc[rows], dt_ref[rows, :], dtb_ref, alog_ref, trif_ref,
                                   ewf_ref, epf_ref, h_scr)
                yf_scr[pl.ds(r0 + c * q, q), :] = y + dsk_ref[...] * xs

            @pl.when(s == ns - 1)
            def _():
                hf_ref[...] = h_scr[...]

        @pl.when(s >= ns)
        def _():
            r0 = pl.multiple_of((2 * ns - 1 - s) * step, step)
            gw = D_SSM // SSM_GROUPS
            for c in reversed(range(n_sub)):
                rows = slice(c * q, (c + 1) * q)
                y, _ = _ssd_chunk(1, xc_scr[pl.ds(r0 + c * q, q), :], dt_ref[rows, :], dtb_ref,
                                  alog_ref, trib_ref, ewb_ref, epb_ref, h_scr)
                gated = (yf_scr[pl.ds(r0 + c * q, q), :] + y) * _silu(z_ref[rows, :])
                for g in range(SSM_GROUPS):
                    part = gated[:, g * gw:(g + 1) * gw]
                    ms = jnp.mean(part * part, axis=-1, keepdims=True)
                    y_ref[rows, g * gw:(g + 1) * gw] = (
                        part * lax.rsqrt(ms + NORM_EPS) * ng_ref[:, g * gw:(g + 1) * gw]).astype(y_ref.dtype)

            @pl.when(s == 2 * ns - 1)
            def _():
                hb_ref[...] = h_scr[...]
    return kern


def _ssd(xbc, dt, z, lay, h0f, h0b, prev_out, *, nb, seq, row_off):
    t = xbc.shape[0]
    step = min(seq, SSD_STEP)
    ns = seq // step
    soff = row_off // step
    last_halo = t // SUBLANES - 1
    per = step // SUBLANES

    fwd_blk = lambda b, s: soff + b * ns + jnp.minimum(s, ns - 1)
    any_blk = lambda b, s: soff + b * ns + jnp.where(s < ns, s, 2 * ns - 1 - s)
    bwd_blk = lambda b, s: soff + b * ns + jnp.where(s < ns, ns - 1, 2 * ns - 1 - s)

    full = lambda a: pl.BlockSpec(a.shape, lambda b, s: (0,) * a.ndim)
    prev = pl.BlockSpec((SUBLANES, XBC_DIM), lambda b, s: (jnp.maximum(fwd_blk(b, s) * per - 1, 0), 0))
    nxt = pl.BlockSpec((SUBLANES, XBC_DIM),
                       lambda b, s: (jnp.minimum((fwd_blk(b, s) + 1) * per, last_halo), 0))
    state = pl.BlockSpec((None, SSM_HEADS // 2, LANES, SSM_STATE), lambda b, s: (b, 0, 0, 0))
    out_spec = pl.BlockSpec((step, D_SSM), lambda b, s: (bwd_blk(b, s), 0))
    consts = [lay["conv_w"], lay["conv_b"], lay["dt_bias"], lay["a_log"], lay["tri_fwd"], lay["tri_bwd"],
              lay["wide_fwd"], lay["wide_bwd"], lay["pair_fwd"], lay["pair_bwd"]]
    if prev_out is None:
        prev_out = jnp.zeros((t, D_SSM), BF16)
    state_shape = jax.ShapeDtypeStruct((nb, SSM_HEADS // 2, LANES, SSM_STATE), F32)
    return pl.pallas_call(
        _make_ssd_kernel(ns, step),
        grid=(nb, 2 * ns),
        in_specs=[prev, pl.BlockSpec((step, XBC_DIM), lambda b, s: (fwd_blk(b, s), 0)), nxt,
                  pl.BlockSpec((step, LANES), lambda b, s: (any_blk(b, s), 0)), out_spec]
                 + [full(a) for a in consts]
                 + [state, state, full(lay["d_skip"]), full(lay["ssm_norm_g"]),
                    pl.BlockSpec(memory_space=pl.ANY)],
        out_specs=[out_spec, state, state],
        out_shape=[jax.ShapeDtypeStruct((t, D_SSM), BF16), state_shape, state_shape],
        scratch_shapes=[pltpu.VMEM((step + 2 * SUBLANES, XBC_DIM), F32),
                        pltpu.VMEM((seq, XBC_DIM), F32),
                        pltpu.VMEM((seq, D_SSM), F32),
                        pltpu.VMEM((SSM_HEADS // 2, LANES, SSM_STATE), F32)],
        input_output_aliases={5 + len(consts) + 4: 0},
        compiler_params=_cparams(("arbitrary", "arbitrary"), 48),
        name="ssd",
    )(xbc, xbc, xbc, dt, z, *consts, h0f, h0b, lay["d_skip"], lay["ssm_norm_g"], prev_out)


def _make_outproj_kernel(n_x, npt):
    def kern(*refs):
        oa_ref, ob_ref, oc_ref = refs[:3]
        xs = refs[3:3 + n_x]
        gate_ref, w_ref, g_ref, sh_ref, sc_ref, xo_ref, h_ref = refs[3 + n_x:]
        na = ATT_HEADS * HEAD_DIM
        o = (_dot(oa_ref[...], w_ref[0:na, :]) + _dot(ob_ref[...], w_ref[na:na + D_SSM, :])
             + _dot(oc_ref[...], w_ref[na + D_SSM:, :]))
        xn = _pick_x(pl.program_id(0), npt, xs) + gate_ref[...] * o
        xo_ref[...] = xn
        h_ref[...] = _norm_mod(xn, g_ref[...], sh_ref[...], sc_ref[...]).astype(h_ref.dtype)
    return kern


def _outproj(oa, ob, oc, x, lay, modv, layer, dims, h_dtype):
    t, d = dims["tokens"], x[0].shape[1]
    npt = dims["prompt_tokens"] // TM
    row_of_tile = dims["row_of_tile"](TM)
    tok = lambda w: pl.BlockSpec((TM, w), lambda i: (i, 0))
    full = lambda a: pl.BlockSpec(a.shape, lambda i: (0,) * a.ndim)
    return pl.pallas_call(
        _make_outproj_kernel(len(x), npt),
        grid=(t // TM,),
        in_specs=[tok(oa.shape[1]), tok(ob.shape[1]), tok(oc.shape[1])] + _x_specs(x, TM, npt) + [
            _mod_spec(layer, 2, d, row_of_tile), full(lay["w_out"]), full(lay["norm2_g"]),
            _mod_spec(layer, 3, d, row_of_tile), _mod_spec(layer, 4, d, row_of_tile)],
        out_specs=[tok(d), tok(d)],
        out_shape=[jax.ShapeDtypeStruct((t, d), F32), jax.ShapeDtypeStruct((t, d), h_dtype)],
        compiler_params=_cparams(("arbitrary",), 40),
        name="outproj",
    )(oa, ob, oc, *x, modv, lay["w_out"], lay["norm2_g"], modv, modv)


def _swiglu_tile(x, wg_ref, wu_ref, wd_ref):
    f = wg_ref.shape[1]
    part = None
    for c0 in range(0, f, FFN_SLICE):
        c1 = min(f, c0 + FFN_SLICE)
        g = _dot(x, wg_ref[:, c0:c1].astype(BF16))
        u = _dot(x, wu_ref[:, c0:c1].astype(BF16))
        d = _dot((_silu(g) * u).astype(BF16), wd_ref[c0:c1, :].astype(BF16))
        part = d if part is None else part + d
    return part


def _ffn_kernel(h_ref, wg_ref, wu_ref, wd_ref, x_ref, gate_ref, o_ref):
    o_ref[...] = x_ref[...] + gate_ref[...] * _swiglu_tile(h_ref[...], wg_ref, wu_ref, wd_ref)


def _dense_ffn(h, x, wg, wu, wd, modv, layer, dims):
    t, d = x.shape
    f = wg.shape[1]
    row_of_tile = dims["row_of_tile"](TM)
    gate_base = (layer * 6 + 5) * MOD_ROWS
    resident = lambda shape: pl.BlockSpec(shape, lambda i: (0, 0), pipeline_mode=pl.Buffered(1))
    return pl.pallas_call(
        _ffn_kernel,
        grid=(t // TM,),
        in_specs=[pl.BlockSpec((TM, d), lambda i: (i, 0)),
                  resident((d, f)), resident((d, f)), resident((f, d)),
                  pl.BlockSpec((TM, d), lambda i: (i, 0)),
                  pl.BlockSpec((None, 1, d), lambda i: (gate_base + row_of_tile(i), 0, 0))],
        out_specs=pl.BlockSpec((TM, d), lambda i: (i, 0)),
        out_shape=jax.ShapeDtypeStruct((t, d), F32),
        compiler_params=_cparams(("arbitrary",), 56),
        name="dense_ffn",
    )(h, wg, wu, wd, x, modv)


def _route_kernel(h_ref, r_ref, tri_ref, info_ref, w_ref, cnt_ref, carry_ref):
    i = pl.program_id(0)

    @pl.when(i == 0)
    def _():
        carry_ref[...] = jnp.zeros_like(carry_ref)

    h_hi, h_lo = _split_bf16(h_ref[...], 2)
    r_hi, r_lo = _split_bf16(r_ref[...], 2)
    logits = _dot(h_hi, r_hi) + (_dot(h_lo, r_hi) + _dot(h_hi, r_lo))
    tm = logits.shape[0]
    lane = lax.broadcasted_iota(jnp.int32, (tm, LANES), 1)
    logits = jnp.where(lane < N_EXPERTS, logits, NEG_BIG)
    l1 = jnp.max(logits, axis=-1, keepdims=True)
    e1 = jnp.min(jnp.where(logits == l1, lane, LANES), axis=-1, keepdims=True)
    rest = jnp.where(lane == e1, NEG_BIG, logits)
    l2 = jnp.max(rest, axis=-1, keepdims=True)
    e2 = jnp.min(jnp.where(rest == l2, lane, LANES), axis=-1, keepdims=True)
    ex = jnp.exp(l2 - l1)
    w1 = 1.0 / (1.0 + ex)
    w2 = ex / (1.0 + ex)
    member = jnp.where((lane == e1) | (lane == e2), 1.0, 0.0)
    before = _dot(tri_ref[...], member.astype(BF16)) + carry_ref[...]
    r1 = jnp.sum(jnp.where(lane == e1, before, 0.0), axis=-1, keepdims=True)
    r2 = jnp.sum(jnp.where(lane == e2, before, 0.0), axis=-1, keepdims=True)
    info = jnp.where(lane == 0, e1.astype(F32),
                     jnp.where(lane == 1, e2.astype(F32),
                               jnp.where(lane == 2, r1, jnp.where(lane == 3, r2, 0.0))))
    info_ref[...] = info.astype(jnp.int32)
    w_ref[...] = jnp.where(lane == 0, w1, jnp.where(lane == 1, w2, 0.0))
    carry_ref[...] += jnp.sum(member, axis=0, keepdims=True)
    cnt_ref[...] = carry_ref[...]


def _route(h, router_pad, tri):
    t, d = h.shape
    return pl.pallas_call(
        _route_kernel,
        grid=(t // TM,),
        in_specs=[pl.BlockSpec((TM, d), lambda i: (i, 0)),
                  pl.BlockSpec(router_pad.shape, lambda i: (0, 0)),
                  pl.BlockSpec(tri.shape, lambda i: (0, 0))],
        out_specs=[pl.BlockSpec((TM, LANES), lambda i: (i, 0)),
                   pl.BlockSpec((TM, LANES), lambda i: (i, 0)),
                   pl.BlockSpec((1, LANES), lambda i: (0, 0))],
        out_shape=[jax.ShapeDtypeStruct((t, LANES), jnp.int32),
                   jax.ShapeDtypeStruct((t, LANES), F32),
                   jax.ShapeDtypeStruct((1, LANES), F32)],
        scratch_shapes=[pltpu.VMEM((1, LANES), F32)],
        compiler_params=_cparams(("arbitrary",), 40),
        name="moe_route",
    )(h, router_pad, tri)


def _row_copy(src, src_row, dst, dst_row, sem):
    return pltpu.make_async_copy(src.at[pl.ds(src_row, 1), :], dst.at[pl.ds(dst_row, 1), :], sem)


def _rows_copy(src, dst, n, sem):
    return pltpu.make_async_copy(src.at[pl.ds(0, n), :], dst.at[pl.ds(0, n), :], sem)


def _dispatch_kernel(dest_ref, pad_ref, h_ref, xb_ref, zero_ref, sem, zsem):
    i = pl.program_id(0)
    tm = h_ref.shape[0]
    base = i * tm * TOP_K

    def start(blk, carry):
        for u in range(DMA_UNROLL):
            r = blk * DMA_UNROLL + u
            for k in range(TOP_K):
                _row_copy(h_ref, r, xb_ref, dest_ref[base + r * TOP_K + k], sem).start()
        return carry

    lax.fori_loop(0, tm // DMA_UNROLL, start, 0)

    @pl.when(i == 0)
    def _():
        zero_ref[...] = jnp.zeros_like(zero_ref)
        for e in range(N_EXPERTS + 1):
            lo, hi = pad_ref[2 * e], pad_ref[2 * e + 1]
            lax.fori_loop(lo, hi, lambda r, c: (_row_copy(zero_ref, 0, xb_ref, r, zsem).start(), c)[1], 0)
        for e in range(N_EXPERTS + 1):
            lo, hi = pad_ref[2 * e], pad_ref[2 * e + 1]
            lax.fori_loop(lo, hi, lambda r, c: (_row_copy(zero_ref, 0, xb_ref, 0, zsem).wait(), c)[1], 0)

    for k in range(TOP_K):
        _rows_copy(h_ref, xb_ref, tm, sem).wait()


def _dispatch(h, dest_flat, pad_ranges, n_rows):
    t, d = h.shape
    return pl.pallas_call(
        _dispatch_kernel,
        grid_spec=pltpu.PrefetchScalarGridSpec(
            num_scalar_prefetch=2,
            grid=(t // TM,),
            in_specs=[pl.BlockSpec((TM, d), lambda i, *_: (i, 0))],
            out_specs=pl.BlockSpec(memory_space=pl.ANY),
            scratch_shapes=[pltpu.VMEM((SUBLANES, d), F32),
                            pltpu.SemaphoreType.DMA(()), pltpu.SemaphoreType.DMA(())]),
        out_shape=jax.ShapeDtypeStruct((n_rows, d), F32),
        compiler_params=_cparams(("arbitrary",), 40),
        name="moe_dispatch",
    )(dest_flat, pad_ranges, h)


def _expert_kernel(be_ref, bv_ref, x_ref, wg_ref, wu_ref, wd_ref, y_ref, xb_scr):
    i = pl.program_id(0)
    j = pl.program_id(1)

    @pl.when(bv_ref[i] > 0)
    def _():
        @pl.when(j == 0)
        def _():
            xb_scr[...] = x_ref[...].astype(BF16)
            y_ref[...] = jnp.zeros_like(y_ref)

        y_ref[...] += _swiglu_tile(xb_scr[...], wg_ref, wu_ref, wd_ref)

    @pl.when((bv_ref[i] == 0) & (j == 0))
    def _():
        y_ref[...] = jnp.zeros_like(y_ref)


def _experts(xb, block_e, block_v, wg, wu, wd):
    n_rows, d = xb.shape
    f = wg.shape[2]

    def jj(i, j, bv):
        return jnp.where(bv[i] > 0, j, 0)

    return pl.pallas_call(
        _expert_kernel,
        grid_spec=pltpu.PrefetchScalarGridSpec(
            num_scalar_prefetch=2,
            grid=(n_rows // MOE_BM, f // MOE_TF),
            in_specs=[pl.BlockSpec((MOE_BM, d), lambda i, j, be, bv: (i, 0)),
                      pl.BlockSpec((None, d, MOE_TF), lambda i, j, be, bv: (be[i], 0, jj(i, j, bv))),
                      pl.BlockSpec((None, d, MOE_TF), lambda i, j, be, bv: (be[i], 0, jj(i, j, bv))),
                      pl.BlockSpec((None, MOE_TF, d), lambda i, j, be, bv: (be[i], jj(i, j, bv), 0))],
            out_specs=pl.BlockSpec((MOE_BM, d), lambda i, j, be, bv: (i, 0)),
            scratch_shapes=[pltpu.VMEM((MOE_BM, d), BF16)]),
        out_shape=jax.ShapeDtypeStruct((n_rows, d), F32),
        compiler_params=_cparams(("arbitrary", "arbitrary"), 56),
        name="moe_experts",
    )(block_e, block_v, xb, wg, wu, wd)


def _make_combine_kernel(npt):
    def kern(dest_ref, yb_ref, w_ref, x_ref, gate_ref, fg_ref, op_ref, os_ref, g0_ref, g1_ref, sem):
        i = pl.program_id(0)
        tm = x_ref.shape[0]
        base = i * tm * TOP_K
        bufs = (g0_ref, g1_ref)

        def start(blk, carry):
            for u in range(DMA_UNROLL):
                r = blk * DMA_UNROLL + u
                for k in range(TOP_K):
                    _row_copy(yb_ref, dest_ref[base + r * TOP_K + k], bufs[k], r, sem).start()
            return carry

        lax.fori_loop(0, tm // DMA_UNROLL, start, 0)
        for k in range(TOP_K):
            _rows_copy(yb_ref, bufs[k], tm, sem).wait()
        w = w_ref[...]
        y = w[:, 0:1] * g0_ref[...] + w[:, 1:2] * g1_ref[...]
        xn = x_ref[...] + gate_ref[...] * y
        ms = jnp.mean(xn * xn, axis=-1, keepdims=True)
        out = xn * lax.rsqrt(ms + NORM_EPS) * fg_ref[...]

        @pl.when(i < npt)
        def _():
            op_ref[...] = out

        @pl.when(i >= npt)
        def _():
            os_ref[...] = out
    return kern


def _combine(yb, dest_flat, top_w, x, modv, layer, final_g, dims):
    t, d = x.shape
    tm = 256
    tp = dims["prompt_tokens"]
    npt = tp // tm
    row_of_tile = dims["row_of_tile"](tm)
    gate_base = (layer * 6 + 5) * MOD_ROWS
    return pl.pallas_call(
        _make_combine_kernel(npt),
        grid_spec=pltpu.PrefetchScalarGridSpec(
            num_scalar_prefetch=1,
            grid=(t // tm,),
            in_specs=[pl.BlockSpec(memory_space=pl.ANY),
                      pl.BlockSpec((tm, LANES), lambda i, *_: (i, 0)),
                      pl.BlockSpec((tm, d), lambda i, *_: (i, 0)),
                      pl.BlockSpec((None, 1, d), lambda i, *_: (gate_base + row_of_tile(i), 0, 0)),
                      pl.BlockSpec((1, d), lambda i, *_: (0, 0))],
            out_specs=[pl.BlockSpec((tm, d), lambda i, *_: (jnp.minimum(i, npt - 1), 0)),
                       pl.BlockSpec((tm, d), lambda i, *_: (jnp.maximum(i - npt, 0), 0))],
            scratch_shapes=[pltpu.VMEM((tm, d), F32), pltpu.VMEM((tm, d), F32),
                            pltpu.SemaphoreType.DMA(())]),
        out_shape=[jax.ShapeDtypeStruct((tp, d), F32), jax.ShapeDtypeStruct((t - tp, d), F32)],
        compiler_params=_cparams(("arbitrary",), 40),
        name="moe_combine",
    )(dest_flat, yb, top_w, x, modv, final_g)


def _moe_layer(h, x, router, wg, wu, wd, modv, layer, final_g, dims):
    t, d = x.shape
    router_pad = jnp.zeros((d, LANES), F32).at[:, :N_EXPERTS].set(router)
    info, top_w, counts = _route(h, router_pad, dims["tri_strict"])
    counts = counts[0, :N_EXPERTS].astype(jnp.int32)
    padded = (counts + MOE_BM - 1) // MOE_BM * MOE_BM
    pad_end = jnp.cumsum(padded)
    pad_start = pad_end - padded
    n_rows = t * TOP_K + N_EXPERTS * MOE_BM
    dest = pad_start[info[:, 0:TOP_K]] + info[:, TOP_K:2 * TOP_K]
    dest_flat = dest.reshape(t * TOP_K).astype(jnp.int32)
    lo = jnp.concatenate([pad_start + counts, pad_end[-1:]])
    hi = jnp.concatenate([pad_end, jnp.array([n_rows], jnp.int32)])
    pad_ranges = jnp.stack([lo, hi], axis=1).reshape(-1).astype(jnp.int32)
    block_start = jnp.arange(n_rows // MOE_BM, dtype=jnp.int32) * MOE_BM
    block_e = jnp.sum(pad_end[None, :] <= block_start[:, None], axis=1).astype(jnp.int32)
    block_v = (block_e < N_EXPERTS).astype(jnp.int32)
    last_e = jnp.max(jnp.where(padded > 0, jnp.arange(N_EXPERTS), 0)).astype(jnp.int32)
    block_e = jnp.where(block_v > 0, block_e, last_e)
    xb = _dispatch(h, dest_flat, pad_ranges, n_rows)
    yb = _experts(xb, block_e, block_v, wg, wu, wd)
    return _combine(yb, dest_flat, top_w, x, modv, layer, final_g, dims)


def _rope_tables(dec_seq):
    n_rows = dec_seq // GRID_W
    row = jnp.repeat(jnp.arange(n_rows, dtype=F32), GRID_W)
    col = (jnp.arange(dec_seq) % GRID_W).astype(F32)
    nf = HEAD_DIM // 4
    inv = ROPE_THETA ** (-jnp.arange(nf, dtype=F32) / nf)
    ang = jnp.stack([row[:, None] * inv, col[:, None] * inv], axis=1)
    cos = jnp.cos(ang)
    sin = jnp.sin(ang)
    cos_h = jnp.concatenate([cos, cos], axis=-1).reshape(dec_seq, HEAD_DIM)
    sin_h = jnp.concatenate([-sin, sin], axis=-1).reshape(dec_seq, HEAD_DIM)
    cos_t = jnp.concatenate([cos_h, cos_h], axis=-1)
    sin_t = jnp.concatenate([sin_h, sin_h], axis=-1)
    cos_t = jnp.concatenate([cos_t, jnp.ones((TM, LANES), F32)], axis=0)
    sin_t = jnp.concatenate([sin_t, jnp.zeros((TM, LANES), F32)], axis=0)
    return cos_t, sin_t


def _expansion(direction, width):
    src = direction * SSM_HEADS + np.arange(SSM_HEADS * width) // width
    return jnp.asarray((np.arange(LANES)[:, None] == src[None, :]).astype(np.float32)).astype(BF16)


def _layer_params(l, norm1_g, norm2_g, w_in, q_norm_g, k_norm_g, conv_w, conv_b, a_log, dt_bias,
                  d_skip, ssm_norm_g, sink, w_out):
    d = w_in.shape[1]
    pad = lambda v: jnp.zeros((1, LANES), F32).at[0, :v.size].set(v.reshape(-1))
    seg = np.arange(LANES) // HEAD_DIM
    bd = jnp.asarray((seg[:, None] == seg[None, :]).astype(np.float32) / HEAD_DIM).astype(BF16)
    r = np.arange(SSM_CHUNK)
    return {
        "norm1_g": norm1_g[l].reshape(1, d), "norm2_g": norm2_g[l].reshape(1, d),
        "w_main": w_in[l, :, :OFF_DT].astype(BF16),
        "w_dt": jnp.zeros((d, LANES), F32).at[:, :2 * SSM_HEADS].set(w_in[l, :, OFF_DT:]).astype(BF16),
        "bd": bd,
        "q_norm_g": jnp.tile(q_norm_g[l], 2).reshape(1, LANES),
        "k_norm_g": jnp.tile(k_norm_g[l], 2).reshape(1, LANES),
        "conv_w": jnp.zeros((SUBLANES, XBC_DIM), F32).at[:D_CONV].set(conv_w[l]),
        "conv_b": conv_b[l].reshape(1, XBC_DIM),
        "a_log": pad(a_log[l]), "dt_bias": pad(dt_bias[l]),
        "d_skip": jnp.repeat(d_skip[l], SSM_HEAD_DIM).reshape(1, D_SSM),
        "ssm_norm_g": ssm_norm_g[l].reshape(1, D_SSM),
        "sink": sink[l],
        "w_out": w_out[l].astype(BF16),
        "tri_fwd": jnp.asarray((r[:, None] >= r[None, :]).astype(np.float32)).astype(BF16),
        "tri_bwd": jnp.asarray((r[:, None] <= r[None, :]).astype(np.float32)).astype(BF16),
        "wide_fwd": _expansion(0, LANES), "wide_bwd": _expansion(1, LANES),
        "pair_fwd": _expansion(0, SSM_HEAD_DIM), "pair_bwd": _expansion(1, SSM_HEAD_DIM),
    }


def kernel(x_prompt, x_sample, cache_attn_k, cache_attn_v, cache_win_k, cache_win_v, state_ssm_fwd, state_ssm_bwd, c, c_ctx, norm1_g, norm2_g, w_mod, b_mod, w_in, q_norm_g, k_norm_g, conv_w, conv_b, a_log, dt_bias, d_skip, ssm_norm_g, sink, w_out, ffn_w_gate, ffn_w_up, ffn_w_down, moe_router, moe_w_gate, moe_w_up, moe_w_down, final_g):
    batch, seq, d = x_prompt.shape
    dec_batch, dec_seq, _ = x_sample.shape
    depth = w_in.shape[0]
    past = cache_attn_k.shape[2]
    tp = batch * seq
    t = tp + dec_batch * dec_seq
    assert depth == 2 and tp % (2 * TM) == 0 and dec_seq % (2 * TM) == 0
    assert 1 + dec_batch <= MOD_ROWS and seq % SSM_CHUNK == 0 and tp % dec_seq == 0

    def row_of_tile(tm):
        npt, tpb = tp // tm, dec_seq // tm
        return lambda i: jnp.where(i < npt, 0, 1 + (i - npt) // tpb)

    cos_t, sin_t = _rope_tables(dec_seq)
    rr = np.arange(TM)
    dims = {"tokens": t, "prompt_tokens": tp, "dec_seq": dec_seq, "row_of_tile": row_of_tile,
            "rope_cos": cos_t, "rope_sin": sin_t,
            "tri_strict": jnp.asarray((rr[:, None] > rr[None, :]).astype(np.float32)).astype(BF16)}

    cc = jnp.zeros((MOD_ROWS, d), F32).at[0].set(c_ctx).at[1:1 + dec_batch].set(c)
    mod = _mod_vectors(cc, w_mod, b_mod)
    modv = mod.reshape(depth, MOD_ROWS, 6, d).transpose(0, 2, 1, 3).reshape(depth * 6 * MOD_ROWS, 1, d)

    x = (x_prompt.reshape(tp, d), x_sample.reshape(dec_batch * dec_seq, d))
    zero_state = jnp.zeros((batch, SSM_HEADS // 2, LANES, SSM_STATE), F32)
    caches = [[] for _ in range(6)]
    y = None
    for l in range(depth):
        lay = _layer_params(l, norm1_g, norm2_g, w_in, q_norm_g, k_norm_g, conv_w, conv_b, a_log,
                            dt_bias, d_skip, ssm_norm_g, sink, w_out)
        (qa, ka, va, qc, kc, vc, z, xbc, dt, cka, cva, ckc, cvc) = _inproj(x, lay, modv, l, dims)
        ctx = lambda a: a[:, l].reshape(dec_batch, past, LANES).astype(BF16)

        oa = _attention(qa, ka, va, None, None, None, None, nb=batch, seq=seq, tq=seq, row_off=0,
                        window=False, name="attn_a_ctx")
        oa = _attention(qa, ka, va, ctx(cache_attn_k), ctx(cache_attn_v), None, oa, nb=dec_batch,
                        seq=dec_seq, tq=TQ_GLOBAL, row_off=tp, window=False, name="attn_a_lat")
        oc = _attention(qc, kc, vc, None, None, lay["sink"], None, nb=batch, seq=seq, tq=seq,
                        row_off=0, window=False, name="attn_c_ctx")
        oc = _attention(qc, kc, vc, ctx(cache_win_k), ctx(cache_win_v), lay["sink"], oc,
                        nb=dec_batch, seq=dec_seq, tq=TQ_WINDOW, row_off=tp, window=True,
                        name="attn_c_lat")
        pair = lambda s: s[:, l].reshape(dec_batch, SSM_HEADS // 2, LANES, SSM_STATE)
        ob, hf_p, hb_p = _ssd(xbc, dt, z, lay, zero_state, zero_state, None, nb=batch, seq=seq,
                              row_off=0)
        ob, _, _ = _ssd(xbc, dt, z, lay, pair(state_ssm_fwd), pair(state_ssm_bwd), ob, nb=dec_batch,
                        seq=dec_seq, row_off=tp)

        kv = lambda a: a[:tp].reshape(batch, seq, ATT_KV_HEADS, HEAD_DIM)
        st = lambda s: s.reshape(batch, SSM_HEADS, SSM_HEAD_DIM, SSM_STATE)
        for lst, val in zip(caches, (kv(cka), kv(cva), kv(ckc), kv(cvc), st(hf_p), st(hb_p))):
            lst.append(val)

        i = l // 2
        if l % 2 == 0:
            xr, h2 = _outproj(oa, ob, oc, x, lay, modv, l, dims, BF16)
            x = (_dense_ffn(h2, xr, ffn_w_gate[i].astype(BF16), ffn_w_up[i].astype(BF16),
                            ffn_w_down[i].astype(BF16), modv, l, dims),)
        else:
            xr, h2 = _outproj(oa, ob, oc, x, lay, modv, l, dims, F32)
            y = _moe_layer(h2, xr, moe_router[i], moe_w_gate[i], moe_w_up[i], moe_w_down[i], modv, l,
                           final_g.reshape(1, d), dims)

    y_prompt = y[0].reshape(batch, seq, d)
    y_sample = y[1].reshape(dec_batch, dec_seq, d)
    return (y_prompt, y_sample) + tuple(jnp.stack(lst, axis=1) for lst in caches)
```

```python
import numpy as np
import jax
import jax.numpy as jnp
from jax import lax
from jax.experimental import pallas as pl
from jax.experimental.pallas import tpu as pltpu

F32 = jnp.float32
BF16 = jnp.bfloat16
HIGHEST = lax.Precision.HIGHEST

GRID_W = 64
HEAD_DIM = 64
ATT_HEADS = 4
ATT_KV_HEADS = 2
WIN_HEADS = 4
WIN_KV_HEADS = 2
WINDOW = 128
ROPE_THETA = 10000.0
SSM_HEADS = 8
SSM_HEAD_DIM = 64
D_SSM = SSM_HEADS * SSM_HEAD_DIM
SSM_GROUPS = 2
SSM_STATE = 128
D_CONV = 5
SSM_CHUNK = 128
XBC_DIM = D_SSM + 2 * SSM_GROUPS * SSM_STATE
N_EXPERTS = 8
TOP_K = 2
NORM_EPS = 1e-6
NEG_BIG = -1e30
LOG2E = 1.4426950408889634

LANES = 128
SUBLANES = 8
MOD_ROWS = 16

OFF_QA, OFF_KA, OFF_VA, OFF_QC, OFF_KC, OFF_VC, OFF_Z, OFF_XBC, OFF_DT = (
    0, 256, 384, 512, 768, 896, 1024, 1536, 2560)

TM = 512
TQ_GLOBAL = 256
TQ_WINDOW = 256
ATTN_KEY_CHUNK = 256
ATTN_KEY_CHUNK_WINDOW = 512
MOE_BM = 1024
MOE_TF = 512
SSD_STEP = 512
FFN_SLICE = 256


def _cparams(sem, vmem_mb):
    return pltpu.CompilerParams(dimension_semantics=sem, vmem_limit_bytes=vmem_mb * 1024 * 1024)


def _dot(a, b, precision=None):
    return jnp.dot(a, b, preferred_element_type=F32, precision=precision)


def _dot_nt(a, b):
    return lax.dot_general(a, b, (((1,), (1,)), ((), ())), preferred_element_type=F32)


def _silu(x):
    return x / (1.0 + jnp.exp(-x))


def _norm_mod(x, g, shift, scale):
    ms = jnp.mean(x * x, axis=-1, keepdims=True)
    y = x * lax.rsqrt(ms + NORM_EPS) * g
    return y * (1.0 + scale) + shift


def _pick_x(i, npt, xs):
    if len(xs) == 1:
        return xs[0][...]
    return jnp.where(i < npt, xs[0][...], xs[1][...])


def _x_specs(x, tm, npt):
    d = x[0].shape[1]
    if len(x) == 1:
        return [pl.BlockSpec((tm, d), lambda i, *_: (i, 0))]
    return [pl.BlockSpec((tm, d), lambda i, *_: (jnp.minimum(i, npt - 1), 0)),
            pl.BlockSpec((tm, d), lambda i, *_: (jnp.maximum(i - npt, 0), 0))]


def _mod_kernel(c_ref, w_ref, b_ref, o_ref):
    a = _silu(c_ref[...])
    o_ref[...] = _dot(a, w_ref[...], HIGHEST) + b_ref[...]


def _mod_vectors(cc, w_mod, b_mod):
    depth, d, n = w_mod.shape
    tn = 1536
    return pl.pallas_call(
        _mod_kernel,
        grid=(depth, n // tn),
        in_specs=[pl.BlockSpec((MOD_ROWS, d), lambda l, j: (0, 0)),
                  pl.BlockSpec((None, d, tn), lambda l, j: (l, 0, j)),
                  pl.BlockSpec((None, 1, tn), lambda l, j: (l, 0, j))],
        out_specs=pl.BlockSpec((None, MOD_ROWS, tn), lambda l, j: (l, 0, j)),
        out_shape=jax.ShapeDtypeStruct((depth, MOD_ROWS, n), F32),
        compiler_params=_cparams(("arbitrary", "arbitrary"), 40),
        name="mod_vectors",
    )(cc, w_mod, b_mod.reshape(depth, 1, n))


def _mod_spec(layer, k, d, row_of_tile):
    base = (layer * 6 + k) * MOD_ROWS
    return pl.BlockSpec((None, 1, d), lambda i, *_: (base + row_of_tile(i), 0, 0))


def _make_inproj_kernel(n_x, npt):
    def kern(*refs):
        xs = refs[:n_x]
        (g_ref, sh_ref, sc_ref, w_ref, wdt_ref, bd_ref, qg_ref, kg_ref, cos_ref, sin_ref,
         qa_ref, ka_ref, va_ref, qc_ref, kc_ref, vc_ref, z_ref, xbc_ref, dt_ref,
         cka_ref, cva_ref, ckc_ref, cvc_ref) = refs[n_x:]
        x = _pick_x(pl.program_id(0), npt, xs)
        h = _norm_mod(x, g_ref[...], sh_ref[...], sc_ref[...]).astype(BF16)

        def proj(a, b):
            return _dot(h, w_ref[:, a:b])

        cos = cos_ref[...]
        sin = sin_ref[...]
        tm = cos.shape[0]
        lane = lax.broadcasted_iota(jnp.int32, (tm, LANES), 1)
        first = (lane % 32) < 16
        low = lane < HEAD_DIM
        bd = bd_ref[...]

        def rope(v):
            partner = jnp.where(first, pltpu.roll(v, LANES - 16, 1), pltpu.roll(v, 16, 1))
            return v * cos + partner * sin

        def head_norm(v, g):
            sq = v * v
            hi = sq.astype(BF16)
            lo = (sq - hi.astype(F32)).astype(BF16)
            ms = _dot(hi, bd) + _dot(lo, bd)
            return v * lax.rsqrt(ms + NORM_EPS) * g

        def stack_heads(c0, c1, out_ref):
            zero = jnp.zeros_like(c0)
            out_ref[0] = jnp.where(low, c0, zero).astype(out_ref.dtype)
            out_ref[1] = jnp.where(low, pltpu.roll(c0, HEAD_DIM, 1), zero).astype(out_ref.dtype)
            out_ref[2] = jnp.where(low, zero, pltpu.roll(c1, HEAD_DIM, 1)).astype(out_ref.dtype)
            out_ref[3] = jnp.where(low, zero, c1).astype(out_ref.dtype)

        scale = LOG2E * HEAD_DIM ** -0.5
        qg = qg_ref[...]
        q_a = proj(OFF_QA, OFF_KA)
        qa0 = rope(head_norm(q_a[:, 0:LANES], qg)) * scale
        qa1 = rope(head_norm(q_a[:, LANES:2 * LANES], qg)) * scale
        stack_heads(qa0, qa1, qa_ref)
        kv_a = proj(OFF_KA, OFF_QC)
        ka = rope(head_norm(kv_a[:, 0:LANES], kg_ref[...]))
        ka_ref[...] = ka.astype(ka_ref.dtype)
        cka_ref[...] = ka
        va = kv_a[:, LANES:2 * LANES]
        va_ref[...] = va.astype(va_ref.dtype)
        cva_ref[...] = va

        q_c = proj(OFF_QC, OFF_KC)
        qc0 = rope(q_c[:, 0:LANES]) * scale
        qc1 = rope(q_c[:, LANES:2 * LANES]) * scale
        stack_heads(qc0, qc1, qc_ref)
        kv_c = proj(OFF_KC, OFF_Z)
        kc = rope(kv_c[:, 0:LANES])
        kc_ref[...] = kc.astype(kc_ref.dtype)
        ckc_ref[...] = kc
        vc = kv_c[:, LANES:2 * LANES]
        vc_ref[...] = vc.astype(vc_ref.dtype)
        cvc_ref[...] = vc

        z_ref[...] = proj(OFF_Z, OFF_XBC)
        xbc_ref[...] = proj(OFF_XBC, OFF_DT)
        dt_ref[...] = _dot(h, wdt_ref[...])
    return kern


def _inproj(x, lay, modv, layer, dims):
    t, d = dims["tokens"], x[0].shape[1]
    n_tiles = t // TM
    npt = dims["prompt_tokens"] // TM
    tpb = dims["dec_seq"] // TM
    row_of_tile = dims["row_of_tile"](TM)

    def rope_idx(i):
        return jnp.where(i < npt, tpb, (i - npt) % tpb)

    def cache_idx(i):
        return jnp.minimum(i, npt)

    tok = lambda w: pl.BlockSpec((TM, w), lambda i: (i, 0))
    full = lambda a: pl.BlockSpec(a.shape, lambda i: (0,) * a.ndim)
    stack = pl.BlockSpec((4, TM, LANES), lambda i: (0, i, 0))
    cache = pl.BlockSpec((TM, LANES), lambda i: (cache_idx(i), 0))
    rope = pl.BlockSpec((TM, LANES), lambda i: (rope_idx(i), 0))
    cache_shape = jax.ShapeDtypeStruct(((npt + 1) * TM, LANES), F32)
    return pl.pallas_call(
        _make_inproj_kernel(len(x), npt),
        grid=(n_tiles,),
        in_specs=_x_specs(x, TM, npt) + [
            full(lay["norm1_g"]),
            _mod_spec(layer, 0, d, row_of_tile), _mod_spec(layer, 1, d, row_of_tile),
            full(lay["w_main"]), full(lay["w_dt"]), full(lay["bd"]),
            full(lay["q_norm_g"]), full(lay["k_norm_g"]), rope, rope],
        out_specs=[stack, tok(LANES), tok(LANES), stack, tok(LANES), tok(LANES),
                   tok(D_SSM), tok(XBC_DIM), tok(LANES), cache, cache, cache, cache],
        out_shape=[jax.ShapeDtypeStruct((4, t, LANES), BF16),
                   jax.ShapeDtypeStruct((t, LANES), BF16), jax.ShapeDtypeStruct((t, LANES), BF16),
                   jax.ShapeDtypeStruct((4, t, LANES), BF16),
                   jax.ShapeDtypeStruct((t, LANES), BF16), jax.ShapeDtypeStruct((t, LANES), BF16),
                   jax.ShapeDtypeStruct((t, D_SSM), F32), jax.ShapeDtypeStruct((t, XBC_DIM), F32),
                   jax.ShapeDtypeStruct((t, LANES), F32),
                   cache_shape, cache_shape, cache_shape, cache_shape],
        compiler_params=_cparams(("arbitrary",), 48),
        name="inproj",
    )(*x, lay["norm1_g"], modv, modv, lay["w_main"], lay["w_dt"], lay["bd"],
      lay["q_norm_g"], lay["k_norm_g"], dims["rope_cos"], dims["rope_sin"])


def _make_attn_kernel(tq, n_ctx, n_lat, window_len, has_sink):
    ck = ATTN_KEY_CHUNK_WINDOW if window_len else ATTN_KEY_CHUNK

    def kern(*refs):
        refs = list(refs)
        sink_ref = refs.pop(0) if has_sink else None
        q_ref = refs.pop(0)
        kc_ref = refs.pop(0) if n_ctx else None
        vc_ref = refs.pop(0) if n_ctx else None
        kl_ref, vl_ref = refs.pop(0), refs.pop(0)
        refs.pop(0)
        o_ref, vaug_ref = refs

        @pl.when(pl.program_id(1) == 0)
        def _():
            if n_ctx:
                vaug_ref[0:n_ctx, 0:LANES] = vc_ref[...]
            vaug_ref[n_ctx:n_ctx + n_lat, 0:LANES] = vl_ref[...]
            vaug_ref[:, LANES:2 * LANES] = jnp.ones((n_ctx + n_lat, LANES), BF16)

        rows = 4 * tq
        q = q_ref[...].reshape(rows, LANES)
        if has_sink:
            head = lax.broadcasted_iota(jnp.int32, (rows, 1), 0) // tq
            snk = LOG2E * jnp.where(head == 0, sink_ref[0],
                                    jnp.where(head == 1, sink_ref[1],
                                              jnp.where(head == 2, sink_ref[2], sink_ref[3])))
            m = snk
        else:
            m = jnp.full((rows, 1), NEG_BIG, F32)
        acc = jnp.zeros((rows, 2 * LANES), F32)

        chunks = []
        for c0 in range(0, n_ctx, ck):
            n = min(ck, n_ctx - c0)
            chunks.append((kc_ref[c0:c0 + n, :], vaug_ref[c0:c0 + n, :], None))
        if window_len:
            q0 = pl.program_id(1) * tq
            ws = pl.multiple_of(jnp.clip(q0 - WINDOW, 0, n_lat - window_len), LANES)
            qpos = q0 + lax.broadcasted_iota(jnp.int32, (rows, window_len), 0) % tq
            kpos = ws + lax.broadcasted_iota(jnp.int32, (rows, window_len), 1)
            valid = jnp.abs(qpos - kpos) <= WINDOW
            chunks.append((kl_ref[pl.ds(ws, window_len), :],
                           vaug_ref[pl.ds(n_ctx + ws, window_len), :], valid))
        else:
            for c0 in range(0, n_lat, ck):
                n = min(ck, n_lat - c0)
                chunks.append((kl_ref[c0:c0 + n, :], vaug_ref[n_ctx + c0:n_ctx + c0 + n, :], None))

        for keys, vaug, valid in chunks:
            s = _dot_nt(q, keys)
            if valid is not None:
                s = jnp.where(valid, s, NEG_BIG)
            m_new = jnp.maximum(m, jnp.max(s, axis=-1, keepdims=True))
            p = jnp.exp2(s - m_new).astype(BF16)
            acc = acc * jnp.exp2(m - m_new) + _dot(p, vaug)
            m = m_new

        den = acc[:, LANES:2 * LANES]
        if has_sink:
            den = den + jnp.exp2(snk - m)
        o = acc[:, 0:LANES] / den
        low = lax.broadcasted_iota(jnp.int32, (tq, LANES), 1) < HEAD_DIM
        o0, o1, o2, o3 = (o[h * tq:(h + 1) * tq] for h in range(4))
        c0 = jnp.where(low, o0, pltpu.roll(o1, HEAD_DIM, 1))
        c1 = jnp.where(low, pltpu.roll(o2, HEAD_DIM, 1), o3)
        o_ref[:, 0:LANES] = c0.astype(o_ref.dtype)
        o_ref[:, LANES:2 * LANES] = c1.astype(o_ref.dtype)
    return kern


def _attention(q_stack, k_lat, v_lat, k_ctx, v_ctx, sink, prev_out, *, nb, seq, tq, row_off,
               window, name):
    t = q_stack.shape[1]
    nq = seq // tq
    qoff = row_off // tq
    boff = row_off // seq
    n_ctx = 0 if k_ctx is None else k_ctx.shape[1]
    window_len = min(seq, tq + 2 * WINDOW) if window else 0
    kern = _make_attn_kernel(tq, n_ctx, seq, window_len, sink is not None)
    in_specs, args = [], []
    if sink is not None:
        in_specs.append(pl.BlockSpec(memory_space=pltpu.SMEM))
        args.append(sink)
    in_specs.append(pl.BlockSpec((4, tq, LANES), lambda b, j: (0, qoff + b * nq + j, 0)))
    args.append(q_stack)
    if n_ctx:
        ctx_spec = pl.BlockSpec((None, n_ctx, LANES), lambda b, j: (b, 0, 0))
        in_specs += [ctx_spec, ctx_spec]
        args += [k_ctx, v_ctx]
    lat_spec = pl.BlockSpec((seq, LANES), lambda b, j: (boff + b, 0))
    in_specs += [lat_spec, lat_spec]
    args += [k_lat, v_lat]
    if prev_out is None:
        prev_out = jnp.zeros((t, 2 * LANES), BF16)
    in_specs.append(pl.BlockSpec(memory_space=pl.ANY))
    args.append(prev_out)
    aliases = {len(args) - 1: 0}
    return pl.pallas_call(
        kern,
        grid=(nb, nq),
        in_specs=in_specs,
        out_specs=pl.BlockSpec((tq, 2 * LANES), lambda b, j: (qoff + b * nq + j, 0)),
        out_shape=jax.ShapeDtypeStruct((t, 2 * LANES), BF16),
        scratch_shapes=[pltpu.VMEM((n_ctx + seq, 2 * LANES), BF16)],
        input_output_aliases=aliases,
        compiler_params=_cparams(("arbitrary", "arbitrary"), 48),
        name=name,
    )(*args)


def _split_bf16(v, n):
    parts = []
    for _ in range(n):
        part = v.astype(BF16)
        parts.append(part)
        v = v - part.astype(F32)
    return parts


def _dot_parts(a, parts):
    out = _dot(a, parts[0])
    for part in parts[1:]:
        out = out + _dot(a, part)
    return out


def _parts_dot(parts, b):
    out = _dot(parts[0], b)
    for part in parts[1:]:
        out = out + _dot(part, b)
    return out


def _ssd_chunk(direction, xc, dt_raw, dtb_ref, alog_ref, tri_ref, ewide_ref, epair_ref, h_scr):
    q = SSM_CHUNK
    xs = xc[:, 0:D_SSM]
    bm = xc[:, D_SSM:D_SSM + SSM_GROUPS * SSM_STATE]
    cm = xc[:, D_SSM + SSM_GROUPS * SSM_STATE:]

    dtx = dt_raw + dtb_ref[...]
    dt = jnp.maximum(dtx, 0.0) + jnp.log1p(jnp.exp(-jnp.abs(dtx)))
    dta = dt * (-LOG2E * jnp.exp(alog_ref[...]))
    a_col = _dot_parts(tri_ref[...], _split_bf16(dta, 3))
    a_row = a_col.T
    a_wide = _parts_dot(_split_bf16(a_col, 3), ewide_ref[...])
    dt_lanes = _parts_dot(_split_bf16(dt, 2), epair_ref[...])
    row = lax.broadcasted_iota(jnp.int32, (q, q), 0)
    col = lax.broadcasted_iota(jnp.int32, (q, q), 1)
    live = (row >= col) if direction == 0 else (col >= row)
    low_lane = col < SSM_HEAD_DIM
    low_row = row < SSM_HEAD_DIM
    edge = q - 1 if direction == 0 else 0

    group = lambda v, g: v[:, g * SSM_STATE:(g + 1) * SSM_STATE]
    cm_b = [group(cm, g).astype(BF16) for g in range(SSM_GROUPS)]
    bm_b = [group(bm, g).astype(BF16) for g in range(SSM_GROUPS)]
    cbs = [_dot_nt(cm_b[g], bm_b[g]) for g in range(SSM_GROUPS)]

    rep = SSM_HEADS // SSM_GROUPS
    y_pairs = []
    for p in range(SSM_HEADS // 2):
        g = (2 * p) // rep
        hl = [direction * SSM_HEADS + 2 * p, direction * SSM_HEADS + 2 * p + 1]
        a_head = [a_wide[:, (2 * p + k) * LANES:(2 * p + k + 1) * LANES] for k in range(2)]
        a_pair = jnp.where(low_lane, a_head[0], a_head[1])
        a_edge = a_pair[edge:edge + 1, :]
        xdt = xs[:, p * LANES:(p + 1) * LANES] * dt_lanes[:, p * LANES:(p + 1) * LANES]
        hs = h_scr[p]
        m_both = jnp.concatenate(
            [(cbs[g] * jnp.exp2(jnp.where(live, a_head[k] - a_row[hl[k]:hl[k] + 1, :], NEG_BIG))
              ).astype(BF16) for k in range(2)], axis=1)
        x_both = jnp.concatenate([jnp.where(low_lane, xdt, 0.0).astype(BF16),
                                  jnp.where(low_lane, 0.0, xdt).astype(BF16)], axis=0)
        yd = _dot(m_both, x_both)
        yo = _dot_nt(cm_b[g], hs.astype(BF16)) * jnp.exp2(a_pair)
        y_pairs.append(yd + yo)
        st = _dot((xdt * jnp.exp2(a_edge - a_pair)).T.astype(BF16), bm_b[g])
        carry = [jnp.exp2(a_col[edge:edge + 1, l:l + 1]) for l in hl]
        h_scr[p] = hs * jnp.where(low_row, carry[0], carry[1]) + st
    return jnp.concatenate(y_pairs, axis=1), xs


def _make_ssd_kernel(ns, step):
    q = SSM_CHUNK
    halo = SUBLANES
    n_sub = step // q

    def kern(xp_ref, xm_ref, xn_ref, dt_ref, z_ref, cw_ref, cb_ref, dtb_ref, alog_ref,
             trif_ref, trib_ref, ewf_ref, ewb_ref, epf_ref, epb_ref,
             h0f_ref, h0b_ref, dsk_ref, ng_ref, _alias_ref,
             y_ref, hf_ref, hb_ref, ext_scr, xc_scr, yf_scr, h_scr):
        s = pl.program_id(1)

        @pl.when(s == 0)
        def _():
            h_scr[...] = h0f_ref[...]

        @pl.when(s == ns)
        def _():
            h_scr[...] = h0b_ref[...]

        @pl.when(s < ns)
        def _():
            ext_scr[0:halo, :] = jnp.where(s > 0, xp_ref[...], 0.0)
            ext_scr[halo:halo + step, :] = xm_ref[...]
            ext_scr[halo + step:2 * halo + step, :] = jnp.where(s < ns - 1, xn_ref[...], 0.0)
            ext = ext_scr[...]
            acc = cb_ref[...] + cw_ref[D_CONV // 2:D_CONV // 2 + 1, :] * ext[halo:halo + step]
            for k in range(D_CONV):
                if k != D_CONV // 2:
                    shifted = pltpu.roll(ext, (D_CONV // 2 - k) % (step + 2 * halo), 0)
                    acc = acc + cw_ref[k:k + 1, :] * shifted[halo:halo + step]
            xc = _silu(acc)
            r0 = pl.multiple_of(s * step, step)
            xc_scr[pl.ds(r0, step), :] = xc
            for c in range(n_sub):
                rows = slice(c * q, (c + 1) * q)
                y, xs = _ssd_chunk(0, xc[rows], dt_ref[rows, :], dtb_ref, alog_ref, trif_ref,
                                   ewf_ref, epf_ref, h_scr)
                yf_scr[pl.ds(r0 + c * q, q), :] = y + dsk_ref[...] * xs

            @pl.when(s == ns - 1)
            def _():
                hf_ref[...] = h_scr[...]

        @pl.when(s >= ns)
        def _():
            r0 = pl.multiple_of((2 * ns - 1 - s) * step, step)
            gw = D_SSM // SSM_GROUPS
            for c in reversed(range(n_sub)):
                rows = slice(c * q, (c + 1) * q)
                y, _ = _ssd_chunk(1, xc_scr[pl.ds(r0 + c * q, q), :], dt_ref[rows, :], dtb_ref,
                                  alog_ref, trib_ref, ewb_ref, epb_ref, h_scr)
                gated = (yf_scr[pl.ds(r0 + c * q, q), :] + y) * _silu(z_ref[rows, :])
                for g in range(SSM_GROUPS):
                    part = gated[:, g * gw:(g + 1) * gw]
                    ms = jnp.mean(part * part, axis=-1, keepdims=True)
                    y_ref[rows, g * gw:(g + 1) * gw] = (
                        part * lax.rsqrt(ms + NORM_EPS) * ng_ref[:, g * gw:(g + 1) * gw]).astype(y_ref.dtype)

            @pl.when(s == 2 * ns - 1)
            def _():
                hb_ref[...] = h_scr[...]
    return kern


def _ssd(xbc, dt, z, lay, h0f, h0b, prev_out, *, nb, seq, row_off):
    t = xbc.shape[0]
    step = min(seq, SSD_STEP)
    ns = seq // step
    soff = row_off // step
    last_halo = t // SUBLANES - 1
    per = step // SUBLANES

    fwd_blk = lambda b, s: soff + b * ns + jnp.minimum(s, ns - 1)
    any_blk = lambda b, s: soff + b * ns + jnp.where(s < ns, s, 2 * ns - 1 - s)
    bwd_blk = lambda b, s: soff + b * ns + jnp.where(s < ns, ns - 1, 2 * ns - 1 - s)

    full = lambda a: pl.BlockSpec(a.shape, lambda b, s: (0,) * a.ndim)
    prev = pl.BlockSpec((SUBLANES, XBC_DIM), lambda b, s: (jnp.maximum(fwd_blk(b, s) * per - 1, 0), 0))
    nxt = pl.BlockSpec((SUBLANES, XBC_DIM),
                       lambda b, s: (jnp.minimum((fwd_blk(b, s) + 1) * per, last_halo), 0))
    state = pl.BlockSpec((None, SSM_HEADS // 2, LANES, SSM_STATE), lambda b, s: (b, 0, 0, 0))
    out_spec = pl.BlockSpec((step, D_SSM), lambda b, s: (bwd_blk(b, s), 0))
    consts = [lay["conv_w"], lay["conv_b"], lay["dt_bias"], lay["a_log"], lay["tri_fwd"], lay["tri_bwd"],
              lay["wide_fwd"], lay["wide_bwd"], lay["pair_fwd"], lay["pair_bwd"]]
    if prev_out is None:
        prev_out = jnp.zeros((t, D_SSM), BF16)
    state_shape = jax.ShapeDtypeStruct((nb, SSM_HEADS // 2, LANES, SSM_STATE), F32)
    return pl.pallas_call(
        _make_ssd_kernel(ns, step),
        grid=(nb, 2 * ns),
        in_specs=[prev, pl.BlockSpec((step, XBC_DIM), lambda b, s: (fwd_blk(b, s), 0)), nxt,
                  pl.BlockSpec((step, LANES), lambda b, s: (any_blk(b, s), 0)), out_spec]
                 + [full(a) for a in consts]
                 + [state, state, full(lay["d_skip"]), full(lay["ssm_norm_g"]),
                    pl.BlockSpec(memory_space=pl.ANY)],
        out_specs=[out_spec, state, state],
        out_shape=[jax.ShapeDtypeStruct((t, D_SSM), BF16), state_shape, state_shape],
        scratch_shapes=[pltpu.VMEM((step + 2 * SUBLANES, XBC_DIM), F32),
                        pltpu.VMEM((seq, XBC_DIM), F32),
                        pltpu.VMEM((seq, D_SSM), F32),
                        pltpu.VMEM((SSM_HEADS // 2, LANES, SSM_STATE), F32)],
        input_output_aliases={5 + len(consts) + 4: 0},
        compiler_params=_cparams(("arbitrary", "arbitrary"), 48),
        name="ssd",
    )(xbc, xbc, xbc, dt, z, *consts, h0f, h0b, lay["d_skip"], lay["ssm_norm_g"], prev_out)


def _make_outproj_kernel(n_x, npt):
    def kern(*refs):
        oa_ref, ob_ref, oc_ref = refs[:3]
        xs = refs[3:3 + n_x]
        gate_ref, w_ref, g_ref, sh_ref, sc_ref, xo_ref, h_ref = refs[3 + n_x:]
        na = ATT_HEADS * HEAD_DIM
        o = (_dot(oa_ref[...], w_ref[0:na, :]) + _dot(ob_ref[...], w_ref[na:na + D_SSM, :])
             + _dot(oc_ref[...], w_ref[na + D_SSM:, :]))
        xn = _pick_x(pl.program_id(0), npt, xs) + gate_ref[...] * o
        xo_ref[...] = xn
        h_ref[...] = _norm_mod(xn, g_ref[...], sh_ref[...], sc_ref[...]).astype(h_ref.dtype)
    return kern


def _outproj(oa, ob, oc, x, lay, modv, layer, dims, h_dtype):
    t, d = dims["tokens"], x[0].shape[1]
    npt = dims["prompt_tokens"] // TM
    row_of_tile = dims["row_of_tile"](TM)
    tok = lambda w: pl.BlockSpec((TM, w), lambda i: (i, 0))
    full = lambda a: pl.BlockSpec(a.shape, lambda i: (0,) * a.ndim)
    return pl.pallas_call(
        _make_outproj_kernel(len(x), npt),
        grid=(t // TM,),
        in_specs=[tok(oa.shape[1]), tok(ob.shape[1]), tok(oc.shape[1])] + _x_specs(x, TM, npt) + [
            _mod_spec(layer, 2, d, row_of_tile), full(lay["w_out"]), full(lay["norm2_g"]),
            _mod_spec(layer, 3, d, row_of_tile), _mod_spec(layer, 4, d, row_of_tile)],
        out_specs=[tok(d), tok(d)],
        out_shape=[jax.ShapeDtypeStruct((t, d), F32), jax.ShapeDtypeStruct((t, d), h_dtype)],
        compiler_params=_cparams(("arbitrary",), 40),
        name="outproj",
    )(oa, ob, oc, *x, modv, lay["w_out"], lay["norm2_g"], modv, modv)


def _swiglu_tile(x, wg_ref, wu_ref, wd_ref):
    f = wg_ref.shape[1]
    part = None
    for c0 in range(0, f, FFN_SLICE):
        c1 = min(f, c0 + FFN_SLICE)
        g = _dot(x, wg_ref[:, c0:c1].astype(BF16))
        u = _dot(x, wu_ref[:, c0:c1].astype(BF16))
        d = _dot((_silu(g) * u).astype(BF16), wd_ref[c0:c1, :].astype(BF16))
        part = d if part is None else part + d
    return part


def _ffn_kernel(h_ref, wg_ref, wu_ref, wd_ref, x_ref, gate_ref, o_ref):
    o_ref[...] = x_ref[...] + gate_ref[...] * _swiglu_tile(h_ref[...], wg_ref, wu_ref, wd_ref)


def _dense_ffn(h, x, wg, wu, wd, modv, layer, dims):
    t, d = x.shape
    f = wg.shape[1]
    row_of_tile = dims["row_of_tile"](TM)
    gate_base = (layer * 6 + 5) * MOD_ROWS
    resident = lambda shape: pl.BlockSpec(shape, lambda i: (0, 0), pipeline_mode=pl.Buffered(1))
    return pl.pallas_call(
        _ffn_kernel,
        grid=(t // TM,),
        in_specs=[pl.BlockSpec((TM, d), lambda i: (i, 0)),
                  resident((d, f)), resident((d, f)), resident((f, d)),
                  pl.BlockSpec((TM, d), lambda i: (i, 0)),
                  pl.BlockSpec((None, 1, d), lambda i: (gate_base + row_of_tile(i), 0, 0))],
        out_specs=pl.BlockSpec((TM, d), lambda i: (i, 0)),
        out_shape=jax.ShapeDtypeStruct((t, d), F32),
        compiler_params=_cparams(("arbitrary",), 56),
        name="dense_ffn",
    )(h, wg, wu, wd, x, modv)


def _route_kernel(h_ref, r_ref, tri_ref, info_ref, w_ref, cnt_ref):
    h_hi, h_lo = _split_bf16(h_ref[...], 2)
    r_hi, r_lo = _split_bf16(r_ref[...], 2)
    logits = _dot(h_hi, r_hi) + (_dot(h_lo, r_hi) + _dot(h_hi, r_lo))
    tm = logits.shape[0]
    lane = lax.broadcasted_iota(jnp.int32, (tm, LANES), 1)
    logits = jnp.where(lane < N_EXPERTS, logits, NEG_BIG)
    l1 = jnp.max(logits, axis=-1, keepdims=True)
    e1 = jnp.min(jnp.where(logits == l1, lane, LANES), axis=-1, keepdims=True)
    rest = jnp.where(lane == e1, NEG_BIG, logits)
    l2 = jnp.max(rest, axis=-1, keepdims=True)
    e2 = jnp.min(jnp.where(rest == l2, lane, LANES), axis=-1, keepdims=True)
    ex = jnp.exp(l2 - l1)
    w1 = 1.0 / (1.0 + ex)
    w2 = ex / (1.0 + ex)
    member = jnp.where((lane == e1) | (lane == e2), 1.0, 0.0)
    before = _dot(tri_ref[...], member.astype(BF16))
    r1 = jnp.sum(jnp.where(lane == e1, before, 0.0), axis=-1, keepdims=True)
    r2 = jnp.sum(jnp.where(lane == e2, before, 0.0), axis=-1, keepdims=True)
    info = jnp.where(lane == 0, e1.astype(F32),
                     jnp.where(lane == 1, e2.astype(F32),
                               jnp.where(lane == 2, r1, jnp.where(lane == 3, r2, 0.0))))
    info_ref[...] = info.astype(jnp.int32)
    w_ref[...] = jnp.where(lane == 0, w1, jnp.where(lane == 1, w2, 0.0))
    cnt_ref[...] = jnp.sum(member, axis=0, keepdims=True)


def _route(h, router_pad, tri):
    t, d = h.shape
    return pl.pallas_call(
        _route_kernel,
        grid=(t // TM,),
        in_specs=[pl.BlockSpec((TM, d), lambda i: (i, 0)),
                  pl.BlockSpec(router_pad.shape, lambda i: (0, 0)),
                  pl.BlockSpec(tri.shape, lambda i: (0, 0))],
        out_specs=[pl.BlockSpec((TM, LANES), lambda i: (i, 0)),
                   pl.BlockSpec((TM, LANES), lambda i: (i, 0)),
                   pl.BlockSpec((None, 1, LANES), lambda i: (i, 0, 0))],
        out_shape=[jax.ShapeDtypeStruct((t, LANES), jnp.int32),
                   jax.ShapeDtypeStruct((t, LANES), F32),
                   jax.ShapeDtypeStruct((t // TM, 1, LANES), F32)],
        compiler_params=_cparams(("arbitrary",), 40),
        name="moe_route",
    )(h, router_pad, tri)


def _group_copies(src, src_row, dst, dst_row, n_groups, sem, wait=False, same_src=False):
    def body(g, carry):
        src_g = 0 if (same_src or wait) else g
        dst_g = 0 if wait else g
        copy = pltpu.make_async_copy(
            src.at[pl.ds(pl.multiple_of(src_row + src_g * SUBLANES, SUBLANES), SUBLANES), :],
            dst.at[pl.ds(pl.multiple_of(dst_row + dst_g * SUBLANES, SUBLANES), SUBLANES), :], sem)
        if wait:
            copy.wait()
        else:
            copy.start()
        return carry

    lax.fori_loop(0, n_groups, body, 0)


def _dispatch_kernel(seg_ref, off_ref, ng_ref, gap_ref, pos_ref, h_ref, xb_ref,
                     sorted_ref, zero_ref, sem, zsem):
    s = pl.program_id(0)
    rows, tm = sorted_ref.shape[0], h_ref.shape[0]
    slot = lax.broadcasted_iota(jnp.int32, (rows, tm), 0)
    place = jnp.where(slot == pos_ref[0:1, :], 1.0, jnp.where(slot == pos_ref[1:2, :], 1.0, 0.0))
    sorted_ref[...] = _dot(place.astype(BF16), h_ref[...].astype(BF16))

    def runs(wait):
        for e in range(N_EXPERTS):
            k = s * N_EXPERTS + e
            _group_copies(sorted_ref, 0 if wait else off_ref[k], xb_ref, 0 if wait else seg_ref[k],
                          ng_ref[k], sem, wait)

    runs(False)

    @pl.when(s == 0)
    def _():
        zero_ref[...] = jnp.zeros_like(zero_ref)
        for wait in (False, True):
            for g in range(N_EXPERTS + 1):
                _group_copies(zero_ref, 0, xb_ref, 0 if wait else gap_ref[2 * g], gap_ref[2 * g + 1],
                              zsem, wait, same_src=True)

    runs(True)


def _dispatch(h, pos_rows, seg, off, ngroups, gaps, n_rows):
    t, d = h.shape
    rows = TM * TOP_K + N_EXPERTS * SUBLANES
    return pl.pallas_call(
        _dispatch_kernel,
        grid_spec=pltpu.PrefetchScalarGridSpec(
            num_scalar_prefetch=4,
            grid=(t // TM,),
            in_specs=[pl.BlockSpec((None, TOP_K, TM), lambda i, *_: (i, 0, 0)),
                      pl.BlockSpec((TM, d), lambda i, *_: (i, 0))],
            out_specs=pl.BlockSpec(memory_space=pl.ANY),
            scratch_shapes=[pltpu.VMEM((rows, d), F32), pltpu.VMEM((SUBLANES, d), F32),
                            pltpu.SemaphoreType.DMA(()), pltpu.SemaphoreType.DMA(())]),
        out_shape=jax.ShapeDtypeStruct((n_rows, d), F32),
        compiler_params=_cparams(("arbitrary",), 48),
        name="moe_dispatch",
    )(seg, off, ngroups, gaps, pos_rows, h)


def _expert_kernel(be_ref, bv_ref, x_ref, wg_ref, wu_ref, wd_ref, y_ref, xb_scr):
    i = pl.program_id(0)
    j = pl.program_id(1)

    @pl.when(bv_ref[i] > 0)
    def _():
        @pl.when(j == 0)
        def _():
            xb_scr[...] = x_ref[...].astype(BF16)
            y_ref[...] = jnp.zeros_like(y_ref)

        y_ref[...] += _swiglu_tile(xb_scr[...], wg_ref, wu_ref, wd_ref)

    @pl.when((bv_ref[i] == 0) & (j == 0))
    def _():
        y_ref[...] = jnp.zeros_like(y_ref)


def _experts(xb, block_e, block_v, wg, wu, wd):
    n_rows, d = xb.shape
    f = wg.shape[2]

    def jj(i, j, bv):
        return jnp.where(bv[i] > 0, j, 0)

    return pl.pallas_call(
        _expert_kernel,
        grid_spec=pltpu.PrefetchScalarGridSpec(
            num_scalar_prefetch=2,
            grid=(n_rows // MOE_BM, f // MOE_TF),
            in_specs=[pl.BlockSpec((MOE_BM, d), lambda i, j, be, bv: (i, 0)),
                      pl.BlockSpec((None, d, MOE_TF), lambda i, j, be, bv: (be[i], 0, jj(i, j, bv))),
                      pl.BlockSpec((None, d, MOE_TF), lambda i, j, be, bv: (be[i], 0, jj(i, j, bv))),
                      pl.BlockSpec((None, MOE_TF, d), lambda i, j, be, bv: (be[i], jj(i, j, bv), 0))],
            out_specs=pl.BlockSpec((MOE_BM, d), lambda i, j, be, bv: (i, 0)),
            scratch_shapes=[pltpu.VMEM((MOE_BM, d), BF16)]),
        out_shape=jax.ShapeDtypeStruct((n_rows, d), F32),
        compiler_params=_cparams(("arbitrary", "arbitrary"), 56),
        name="moe_experts",
    )(block_e, block_v, xb, wg, wu, wd)


def _make_combine_kernel(npt):
    def kern(seg_ref, off_ref, ng_ref, yb_ref, pos_ref, w_ref, x_ref, gate_ref, fg_ref,
             op_ref, os_ref, runs_ref, sems):
        i = pl.program_id(0)
        n = pl.num_programs(0)
        rows = runs_ref.shape[1]
        tm = x_ref.shape[0]

        def fetch(tile, slot, wait=False):
            for e in range(N_EXPERTS):
                k = tile * N_EXPERTS + e
                _group_copies(yb_ref, 0 if wait else seg_ref[k], runs_ref.at[slot],
                              0 if wait else off_ref[k], ng_ref[k], sems.at[slot], wait)

        @pl.when(i == 0)
        def _():
            runs_ref[...] = jnp.zeros_like(runs_ref)
            fetch(0, 0)

        @pl.when(i + 1 < n)
        def _():
            fetch(i + 1, (i + 1) % 2)

        slot = i % 2
        fetch(i, slot, wait=True)
        y_runs = runs_ref[slot].astype(BF16)
        col = lax.broadcasted_iota(jnp.int32, (tm, rows), 1)
        pos = pos_ref[...]
        picked = [_dot(jnp.where(col == pos[:, k:k + 1], 1.0, 0.0).astype(BF16), y_runs)
                  for k in range(TOP_K)]
        w = w_ref[...]
        y = w[:, 0:1] * picked[0] + w[:, 1:2] * picked[1]
        xn = x_ref[...] + gate_ref[...] * y
        ms = jnp.mean(xn * xn, axis=-1, keepdims=True)
        out = xn * lax.rsqrt(ms + NORM_EPS) * fg_ref[...]

        @pl.when(i < npt)
        def _():
            op_ref[...] = out

        @pl.when(i >= npt)
        def _():
            os_ref[...] = out
    return kern


def _combine(yb, pos, seg, off, ngroups, top_w, x, modv, layer, final_g, dims):
    t, d = x.shape
    tp = dims["prompt_tokens"]
    npt = tp // TM
    rows = TM * TOP_K + N_EXPERTS * SUBLANES
    row_of_tile = dims["row_of_tile"](TM)
    gate_base = (layer * 6 + 5) * MOD_ROWS
    return pl.pallas_call(
        _make_combine_kernel(npt),
        grid_spec=pltpu.PrefetchScalarGridSpec(
            num_scalar_prefetch=3,
            grid=(t // TM,),
            in_specs=[pl.BlockSpec(memory_space=pl.ANY),
                      pl.BlockSpec((TM, TOP_K), lambda i, *_: (i, 0)),
                      pl.BlockSpec((TM, LANES), lambda i, *_: (i, 0)),
                      pl.BlockSpec((TM, d), lambda i, *_: (i, 0)),
                      pl.BlockSpec((None, 1, d), lambda i, *_: (gate_base + row_of_tile(i), 0, 0)),
                      pl.BlockSpec((1, d), lambda i, *_: (0, 0))],
            out_specs=[pl.BlockSpec((TM, d), lambda i, *_: (jnp.minimum(i, npt - 1), 0)),
                       pl.BlockSpec((TM, d), lambda i, *_: (jnp.maximum(i - npt, 0), 0))],
            scratch_shapes=[pltpu.VMEM((2, rows, d), F32), pltpu.SemaphoreType.DMA((2,))]),
        out_shape=[jax.ShapeDtypeStruct((tp, d), F32), jax.ShapeDtypeStruct((t - tp, d), F32)],
        compiler_params=_cparams(("arbitrary",), 56),
        name="moe_combine",
    )(seg, off, ngroups, yb, pos, top_w, x, modv, final_g)


def _moe_layer(h, x, router, wg, wu, wd, modv, layer, final_g, dims):
    t, d = x.shape
    n_tiles = t // TM
    router_pad = jnp.zeros((d, LANES), F32).at[:, :N_EXPERTS].set(router)
    info, top_w, counts = _route(h, router_pad, dims["tri_strict"])
    run = (counts[:, 0, :N_EXPERTS].astype(jnp.int32) + SUBLANES - 1) // SUBLANES * SUBLANES
    off = jnp.cumsum(run, axis=1) - run
    before = jnp.cumsum(run, axis=0) - run
    total = jnp.sum(run, axis=0)
    padded = (total + MOE_BM - 1) // MOE_BM * MOE_BM
    pad_end = jnp.cumsum(padded)
    pad_start = pad_end - padded
    seg = pad_start[None, :] + before
    n_rows = (t * TOP_K + n_tiles * N_EXPERTS * (SUBLANES - 1) + MOE_BM - 1) // MOE_BM * MOE_BM \
        + N_EXPERTS * MOE_BM
    tile_of = jnp.arange(t, dtype=jnp.int32)[:, None] // TM
    pos = (off[tile_of, info[:, 0:TOP_K]] + info[:, TOP_K:2 * TOP_K]).astype(jnp.int32)
    pos_rows = pos.reshape(n_tiles, TM, TOP_K).transpose(0, 2, 1)
    flat = lambda a: a.reshape(-1).astype(jnp.int32)
    gap_lo = jnp.concatenate([pad_start + total, pad_end[-1:]])
    gap_hi = jnp.concatenate([pad_end, jnp.array([n_rows], jnp.int32)])
    gaps = jnp.stack([gap_lo, (gap_hi - gap_lo) // SUBLANES], axis=1)
    block_start = jnp.arange(n_rows // MOE_BM, dtype=jnp.int32) * MOE_BM
    block_e = jnp.sum(pad_end[None, :] <= block_start[:, None], axis=1).astype(jnp.int32)
    block_v = (block_e < N_EXPERTS).astype(jnp.int32)
    last_e = jnp.max(jnp.where(padded > 0, jnp.arange(N_EXPERTS), 0)).astype(jnp.int32)
    block_e = jnp.where(block_v > 0, block_e, last_e)
    xb = _dispatch(h, pos_rows, flat(seg), flat(off), flat(run // SUBLANES), flat(gaps), n_rows)
    yb = _experts(xb, block_e, block_v, wg, wu, wd)
    return _combine(yb, pos, flat(seg), flat(off), flat(run // SUBLANES), top_w, x, modv, layer,
                    final_g, dims)


def _rope_tables(dec_seq):
    n_rows = dec_seq // GRID_W
    row = jnp.repeat(jnp.arange(n_rows, dtype=F32), GRID_W)
    col = (jnp.arange(dec_seq) % GRID_W).astype(F32)
    nf = HEAD_DIM // 4
    inv = ROPE_THETA ** (-jnp.arange(nf, dtype=F32) / nf)
    ang = jnp.stack([row[:, None] * inv, col[:, None] * inv], axis=1)
    cos = jnp.cos(ang)
    sin = jnp.sin(ang)
    cos_h = jnp.concatenate([cos, cos], axis=-1).reshape(dec_seq, HEAD_DIM)
    sin_h = jnp.concatenate([-sin, sin], axis=-1).reshape(dec_seq, HEAD_DIM)
    cos_t = jnp.concatenate([cos_h, cos_h], axis=-1)
    sin_t = jnp.concatenate([sin_h, sin_h], axis=-1)
    cos_t = jnp.concatenate([cos_t, jnp.ones((TM, LANES), F32)], axis=0)
    sin_t = jnp.concatenate([sin_t, jnp.zeros((TM, LANES), F32)], axis=0)
    return cos_t, sin_t


def _expansion(direction, width):
    src = direction * SSM_HEADS + np.arange(SSM_HEADS * width) // width
    return jnp.asarray((np.arange(LANES)[:, None] == src[None, :]).astype(np.float32)).astype(BF16)


def _layer_params(l, norm1_g, norm2_g, w_in, q_norm_g, k_norm_g, conv_w, conv_b, a_log, dt_bias,
                  d_skip, ssm_norm_g, sink, w_out):
    d = w_in.shape[1]
    pad = lambda v: jnp.zeros((1, LANES), F32).at[0, :v.size].set(v.reshape(-1))
    seg = np.arange(LANES) // HEAD_DIM
    bd = jnp.asarray((seg[:, None] == seg[None, :]).astype(np.float32) / HEAD_DIM).astype(BF16)
    r = np.arange(SSM_CHUNK)
    return {
        "norm1_g": norm1_g[l].reshape(1, d), "norm2_g": norm2_g[l].reshape(1, d),
        "w_main": w_in[l, :, :OFF_DT].astype(BF16),
        "w_dt": jnp.zeros((d, LANES), F32).at[:, :2 * SSM_HEADS].set(w_in[l, :, OFF_DT:]).astype(BF16),
        "bd": bd,
        "q_norm_g": jnp.tile(q_norm_g[l], 2).reshape(1, LANES),
        "k_norm_g": jnp.tile(k_norm_g[l], 2).reshape(1, LANES),
        "conv_w": jnp.zeros((SUBLANES, XBC_DIM), F32).at[:D_CONV].set(conv_w[l]),
        "conv_b": conv_b[l].reshape(1, XBC_DIM),
        "a_log": pad(a_log[l]), "dt_bias": pad(dt_bias[l]),
        "d_skip": jnp.repeat(d_skip[l], SSM_HEAD_DIM).reshape(1, D_SSM),
        "ssm_norm_g": ssm_norm_g[l].reshape(1, D_SSM),
        "sink": sink[l],
        "w_out": w_out[l].astype(BF16),
        "tri_fwd": jnp.asarray((r[:, None] >= r[None, :]).astype(np.float32)).astype(BF16),
        "tri_bwd": jnp.asarray((r[:, None] <= r[None, :]).astype(np.float32)).astype(BF16),
        "wide_fwd": _expansion(0, LANES), "wide_bwd": _expansion(1, LANES),
        "pair_fwd": _expansion(0, SSM_HEAD_DIM), "pair_bwd": _expansion(1, SSM_HEAD_DIM),
    }


def kernel(x_prompt, x_sample, cache_attn_k, cache_attn_v, cache_win_k, cache_win_v, state_ssm_fwd, state_ssm_bwd, c, c_ctx, norm1_g, norm2_g, w_mod, b_mod, w_in, q_norm_g, k_norm_g, conv_w, conv_b, a_log, dt_bias, d_skip, ssm_norm_g, sink, w_out, ffn_w_gate, ffn_w_up, ffn_w_down, moe_router, moe_w_gate, moe_w_up, moe_w_down, final_g):
    batch, seq, d = x_prompt.shape
    dec_batch, dec_seq, _ = x_sample.shape
    depth = w_in.shape[0]
    past = cache_attn_k.shape[2]
    tp = batch * seq
    t = tp + dec_batch * dec_seq
    assert depth == 2 and tp % (2 * TM) == 0 and dec_seq % (2 * TM) == 0
    assert 1 + dec_batch <= MOD_ROWS and seq % SSM_CHUNK == 0 and tp % dec_seq == 0

    def row_of_tile(tm):
        npt, tpb = tp // tm, dec_seq // tm
        return lambda i: jnp.where(i < npt, 0, 1 + (i - npt) // tpb)

    cos_t, sin_t = _rope_tables(dec_seq)
    rr = np.arange(TM)
    dims = {"tokens": t, "prompt_tokens": tp, "dec_seq": dec_seq, "row_of_tile": row_of_tile,
            "rope_cos": cos_t, "rope_sin": sin_t,
            "tri_strict": jnp.asarray((rr[:, None] > rr[None, :]).astype(np.float32)).astype(BF16)}

    cc = jnp.zeros((MOD_ROWS, d), F32).at[0].set(c_ctx).at[1:1 + dec_batch].set(c)
    mod = _mod_vectors(cc, w_mod, b_mod)
    modv = mod.reshape(depth, MOD_ROWS, 6, d).transpose(0, 2, 1, 3).reshape(depth * 6 * MOD_ROWS, 1, d)

    x = (x_prompt.reshape(tp, d), x_sample.reshape(dec_batch * dec_seq, d))
    zero_state = jnp.zeros((batch, SSM_HEADS // 2, LANES, SSM_STATE), F32)
    caches = [[] for _ in range(6)]
    y = None
    for l in range(depth):
        lay = _layer_params(l, norm1_g, norm2_g, w_in, q_norm_g, k_norm_g, conv_w, conv_b, a_log,
                            dt_bias, d_skip, ssm_norm_g, sink, w_out)
        (qa, ka, va, qc, kc, vc, z, xbc, dt, cka, cva, ckc, cvc) = _inproj(x, lay, modv, l, dims)
        ctx = lambda a: a[:, l].reshape(dec_batch, past, LANES).astype(BF16)

        oa = _attention(qa, ka, va, None, None, None, None, nb=batch, seq=seq, tq=seq, row_off=0,
                        window=False, name="attn_a_ctx")
        oa = _attention(qa, ka, va, ctx(cache_attn_k), ctx(cache_attn_v), None, oa, nb=dec_batch,
                        seq=dec_seq, tq=TQ_GLOBAL, row_off=tp, window=False, name="attn_a_lat")
        oc = _attention(qc, kc, vc, None, None, lay["sink"], None, nb=batch, seq=seq, tq=seq,
                        row_off=0, window=False, name="attn_c_ctx")
        oc = _attention(qc, kc, vc, ctx(cache_win_k), ctx(cache_win_v), lay["sink"], oc,
                        nb=dec_batch, seq=dec_seq, tq=TQ_WINDOW, row_off=tp, window=True,
                        name="attn_c_lat")
        pair = lambda s: s[:, l].reshape(dec_batch, SSM_HEADS // 2, LANES, SSM_STATE)
        ob, hf_p, hb_p = _ssd(xbc, dt, z, lay, zero_state, zero_state, None, nb=batch, seq=seq,
                              row_off=0)
        ob, _, _ = _ssd(xbc, dt, z, lay, pair(state_ssm_fwd), pair(state_ssm_bwd), ob, nb=dec_batch,
                        seq=dec_seq, row_off=tp)

        kv = lambda a: a[:tp].reshape(batch, seq, ATT_KV_HEADS, HEAD_DIM)
        st = lambda s: s.reshape(batch, SSM_HEADS, SSM_HEAD_DIM, SSM_STATE)
        for lst, val in zip(caches, (kv(cka), kv(cva), kv(ckc), kv(cvc), st(hf_p), st(hb_p))):
            lst.append(val)

        i = l // 2
        if l % 2 == 0:
            xr, h2 = _outproj(oa, ob, oc, x, lay, modv, l, dims, BF16)
            x = (_dense_ffn(h2, xr, ffn_w_gate[i].astype(BF16), ffn_w_up[i].astype(BF16),
                            ffn_w_down[i].astype(BF16), modv, l, dims),)
        else:
            xr, h2 = _outproj(oa, ob, oc, x, lay, modv, l, dims, F32)
            y = _moe_layer(h2, xr, moe_router[i], moe_w_gate[i], moe_w_up[i], moe_w_down[i], modv, l,
                           final_g.reshape(1, d), dims)

    y_prompt = y[0].reshape(batch, seq, d)
    y_sample = y[1].reshape(dec_batch, dec_seq, d)
    return (y_prompt, y_sample) + tuple(jnp.stack(lst, axis=1) for lst in caches)
```

```python
import numpy as np
import jax
import jax.numpy as jnp
from jax import lax
from jax.experimental import pallas as pl
from jax.experimental.pallas import tpu as pltpu

F32 = jnp.float32
BF16 = jnp.bfloat16
HIGHEST = lax.Precision.HIGHEST

GRID_W = 64
HEAD_DIM = 64
ATT_HEADS = 4
ATT_KV_HEADS = 2
WIN_HEADS = 4
WIN_KV_HEADS = 2
WINDOW = 128
ROPE_THETA = 10000.0
SSM_HEADS = 8
SSM_HEAD_DIM = 64
D_SSM = SSM_HEADS * SSM_HEAD_DIM
SSM_GROUPS = 2
SSM_STATE = 128
D_CONV = 5
SSM_CHUNK = 128
XBC_DIM = D_SSM + 2 * SSM_GROUPS * SSM_STATE
N_EXPERTS = 8
TOP_K = 2
NORM_EPS = 1e-6
NEG_BIG = -1e30
LOG2E = 1.4426950408889634

LANES = 128
SUBLANES = 8
MOD_ROWS = 16

OFF_QA, OFF_KA, OFF_VA, OFF_QC, OFF_KC, OFF_VC, OFF_Z, OFF_XBC, OFF_DT = (
    0, 256, 384, 512, 768, 896, 1024, 1536, 2560)

TM = 512
TQ_GLOBAL = 256
TQ_WINDOW = 256
ATTN_KEY_CHUNK = 256
ATTN_KEY_CHUNK_WINDOW = 512
MOE_BM = 1024
MOE_TF = 512
SSD_STEP = 512
FFN_SLICE = 256


def _cparams(sem, vmem_mb):
    return pltpu.CompilerParams(dimension_semantics=sem, vmem_limit_bytes=vmem_mb * 1024 * 1024)


def _dot(a, b, precision=None):
    return jnp.dot(a, b, preferred_element_type=F32, precision=precision)


def _dot_nt(a, b):
    return lax.dot_general(a, b, (((1,), (1,)), ((), ())), preferred_element_type=F32)


def _silu(x):
    return x / (1.0 + jnp.exp(-x))


def _norm_mod(x, g, shift, scale):
    ms = jnp.mean(x * x, axis=-1, keepdims=True)
    y = x * lax.rsqrt(ms + NORM_EPS) * g
    return y * (1.0 + scale) + shift


def _pick_x(i, npt, xs):
    if len(xs) == 1:
        return xs[0][...]
    return jnp.where(i < npt, xs[0][...], xs[1][...])


def _x_specs(x, tm, npt):
    d = x[0].shape[1]
    if len(x) == 1:
        return [pl.BlockSpec((tm, d), lambda i, *_: (i, 0))]
    return [pl.BlockSpec((tm, d), lambda i, *_: (jnp.minimum(i, npt - 1), 0)),
            pl.BlockSpec((tm, d), lambda i, *_: (jnp.maximum(i - npt, 0), 0))]


def _mod_kernel(c_ref, w_ref, b_ref, o_ref):
    a = _silu(c_ref[...])
    o_ref[...] = _dot(a, w_ref[...], HIGHEST) + b_ref[...]


def _mod_vectors(cc, w_mod, b_mod):
    depth, d, n = w_mod.shape
    tn = 1536
    return pl.pallas_call(
        _mod_kernel,
        grid=(depth, n // tn),
        in_specs=[pl.BlockSpec((MOD_ROWS, d), lambda l, j: (0, 0)),
                  pl.BlockSpec((None, d, tn), lambda l, j: (l, 0, j)),
                  pl.BlockSpec((None, 1, tn), lambda l, j: (l, 0, j))],
        out_specs=pl.BlockSpec((None, MOD_ROWS, tn), lambda l, j: (l, 0, j)),
        out_shape=jax.ShapeDtypeStruct((depth, MOD_ROWS, n), F32),
        compiler_params=_cparams(("arbitrary", "arbitrary"), 40),
        name="mod_vectors",
    )(cc, w_mod, b_mod.reshape(depth, 1, n))


def _mod_spec(layer, k, d, row_of_tile):
    base = (layer * 6 + k) * MOD_ROWS
    return pl.BlockSpec((None, 1, d), lambda i, *_: (base + row_of_tile(i), 0, 0))


def _make_inproj_kernel(n_x, npt):
    def kern(*refs):
        xs = refs[:n_x]
        (g_ref, sh_ref, sc_ref, w_ref, wdt_ref, bd_ref, qg_ref, kg_ref, cos_ref, sin_ref,
         qa_ref, ka_ref, va_ref, qc_ref, kc_ref, vc_ref, z_ref, xbc_ref, dt_ref,
         cka_ref, cva_ref, ckc_ref, cvc_ref) = refs[n_x:]
        x = _pick_x(pl.program_id(0), npt, xs)
        h = _norm_mod(x, g_ref[...], sh_ref[...], sc_ref[...]).astype(BF16)

        def proj(a, b):
            return _dot(h, w_ref[:, a:b])

        cos = cos_ref[...]
        sin = sin_ref[...]
        tm = cos.shape[0]
        lane = lax.broadcasted_iota(jnp.int32, (tm, LANES), 1)
        first = (lane % 32) < 16
        low = lane < HEAD_DIM
        bd = bd_ref[...]

        def rope(v):
            partner = jnp.where(first, pltpu.roll(v, LANES - 16, 1), pltpu.roll(v, 16, 1))
            return v * cos + partner * sin

        def head_norm(v, g):
            sq = v * v
            hi = sq.astype(BF16)
            lo = (sq - hi.astype(F32)).astype(BF16)
            ms = _dot(hi, bd) + _dot(lo, bd)
            return v * lax.rsqrt(ms + NORM_EPS) * g

        def stack_heads(c0, c1, out_ref):
            zero = jnp.zeros_like(c0)
            out_ref[0] = jnp.where(low, c0, zero).astype(out_ref.dtype)
            out_ref[1] = jnp.where(low, pltpu.roll(c0, HEAD_DIM, 1), zero).astype(out_ref.dtype)
            out_ref[2] = jnp.where(low, zero, pltpu.roll(c1, HEAD_DIM, 1)).astype(out_ref.dtype)
            out_ref[3] = jnp.where(low, zero, c1).astype(out_ref.dtype)

        scale = LOG2E * HEAD_DIM ** -0.5
        qg = qg_ref[...]
        q_a = proj(OFF_QA, OFF_KA)
        qa0 = rope(head_norm(q_a[:, 0:LANES], qg)) * scale
        qa1 = rope(head_norm(q_a[:, LANES:2 * LANES], qg)) * scale
        stack_heads(qa0, qa1, qa_ref)
        kv_a = proj(OFF_KA, OFF_QC)
        ka = rope(head_norm(kv_a[:, 0:LANES], kg_ref[...]))
        ka_ref[...] = ka.astype(ka_ref.dtype)
        cka_ref[...] = ka
        va = kv_a[:, LANES:2 * LANES]
        va_ref[...] = va.astype(va_ref.dtype)
        cva_ref[...] = va

        q_c = proj(OFF_QC, OFF_KC)
        qc0 = rope(q_c[:, 0:LANES]) * scale
        qc1 = rope(q_c[:, LANES:2 * LANES]) * scale
        stack_heads(qc0, qc1, qc_ref)
        kv_c = proj(OFF_KC, OFF_Z)
        kc = rope(kv_c[:, 0:LANES])
        kc_ref[...] = kc.astype(kc_ref.dtype)
        ckc_ref[...] = kc
        vc = kv_c[:, LANES:2 * LANES]
        vc_ref[...] = vc.astype(vc_ref.dtype)
        cvc_ref[...] = vc

        z_ref[...] = proj(OFF_Z, OFF_XBC)
        xbc_ref[...] = proj(OFF_XBC, OFF_DT)
        dt_ref[...] = _dot(h, wdt_ref[...])
    return kern


def _inproj(x, lay, modv, layer, dims):
    t, d = dims["tokens"], x[0].shape[1]
    n_tiles = t // TM
    npt = dims["prompt_tokens"] // TM
    tpb = dims["dec_seq"] // TM
    row_of_tile = dims["row_of_tile"](TM)

    def rope_idx(i):
        return jnp.where(i < npt, tpb, (i - npt) % tpb)

    def cache_idx(i):
        return jnp.minimum(i, npt)

    tok = lambda w: pl.BlockSpec((TM, w), lambda i: (i, 0))
    full = lambda a: pl.BlockSpec(a.shape, lambda i: (0,) * a.ndim)
    stack = pl.BlockSpec((4, TM, LANES), lambda i: (0, i, 0))
    cache = pl.BlockSpec((TM, LANES), lambda i: (cache_idx(i), 0))
    rope = pl.BlockSpec((TM, LANES), lambda i: (rope_idx(i), 0))
    cache_shape = jax.ShapeDtypeStruct(((npt + 1) * TM, LANES), F32)
    return pl.pallas_call(
        _make_inproj_kernel(len(x), npt),
        grid=(n_tiles,),
        in_specs=_x_specs(x, TM, npt) + [
            full(lay["norm1_g"]),
            _mod_spec(layer, 0, d, row_of_tile), _mod_spec(layer, 1, d, row_of_tile),
            full(lay["w_main"]), full(lay["w_dt"]), full(lay["bd"]),
            full(lay["q_norm_g"]), full(lay["k_norm_g"]), rope, rope],
        out_specs=[stack, tok(LANES), tok(LANES), stack, tok(LANES), tok(LANES),
                   tok(D_SSM), tok(XBC_DIM), tok(LANES), cache, cache, cache, cache],
        out_shape=[jax.ShapeDtypeStruct((4, t, LANES), BF16),
                   jax.ShapeDtypeStruct((t, LANES), BF16), jax.ShapeDtypeStruct((t, LANES), BF16),
                   jax.ShapeDtypeStruct((4, t, LANES), BF16),
                   jax.ShapeDtypeStruct((t, LANES), BF16), jax.ShapeDtypeStruct((t, LANES), BF16),
                   jax.ShapeDtypeStruct((t, D_SSM), F32), jax.ShapeDtypeStruct((t, XBC_DIM), F32),
                   jax.ShapeDtypeStruct((t, LANES), F32),
                   cache_shape, cache_shape, cache_shape, cache_shape],
        compiler_params=_cparams(("arbitrary",), 48),
        name="inproj",
    )(*x, lay["norm1_g"], modv, modv, lay["w_main"], lay["w_dt"], lay["bd"],
      lay["q_norm_g"], lay["k_norm_g"], dims["rope_cos"], dims["rope_sin"])


def _make_attn_kernel(tq, n_ctx, n_lat, window_len, has_sink):
    ck = ATTN_KEY_CHUNK_WINDOW if window_len else ATTN_KEY_CHUNK

    def kern(*refs):
        refs = list(refs)
        sink_ref = refs.pop(0) if has_sink else None
        q_ref = refs.pop(0)
        kc_ref = refs.pop(0) if n_ctx else None
        vc_ref = refs.pop(0) if n_ctx else None
        kl_ref, vl_ref = refs.pop(0), refs.pop(0)
        refs.pop(0)
        o_ref, vaug_ref = refs

        @pl.when(pl.program_id(1) == 0)
        def _():
            if n_ctx:
                vaug_ref[0:n_ctx, 0:LANES] = vc_ref[...]
            vaug_ref[n_ctx:n_ctx + n_lat, 0:LANES] = vl_ref[...]
            vaug_ref[:, LANES:2 * LANES] = jnp.ones((n_ctx + n_lat, LANES), BF16)

        rows = 4 * tq
        q = q_ref[...].reshape(rows, LANES)
        if has_sink:
            head = lax.broadcasted_iota(jnp.int32, (rows, 1), 0) // tq
            snk = LOG2E * jnp.where(head == 0, sink_ref[0],
                                    jnp.where(head == 1, sink_ref[1],
                                              jnp.where(head == 2, sink_ref[2], sink_ref[3])))
            m = snk
        else:
            m = jnp.full((rows, 1), NEG_BIG, F32)
        acc = jnp.zeros((rows, 2 * LANES), F32)

        chunks = []
        for c0 in range(0, n_ctx, ck):
            n = min(ck, n_ctx - c0)
            chunks.append((kc_ref[c0:c0 + n, :], vaug_ref[c0:c0 + n, :], None))
        if window_len:
            q0 = pl.program_id(1) * tq
            ws = pl.multiple_of(jnp.clip(q0 - WINDOW, 0, n_lat - window_len), LANES)
            qpos = q0 + lax.broadcasted_iota(jnp.int32, (rows, window_len), 0) % tq
            kpos = ws + lax.broadcasted_iota(jnp.int32, (rows, window_len), 1)
            valid = jnp.abs(qpos - kpos) <= WINDOW
            chunks.append((kl_ref[pl.ds(ws, window_len), :],
                           vaug_ref[pl.ds(n_ctx + ws, window_len), :], valid))
        else:
            for c0 in range(0, n_lat, ck):
                n = min(ck, n_lat - c0)
                chunks.append((kl_ref[c0:c0 + n, :], vaug_ref[n_ctx + c0:n_ctx + c0 + n, :], None))

        for keys, vaug, valid in chunks:
            s = _dot_nt(q, keys)
            if valid is not None:
                s = jnp.where(valid, s, NEG_BIG)
            m_new = jnp.maximum(m, jnp.max(s, axis=-1, keepdims=True))
            p = jnp.exp2(s - m_new).astype(BF16)
            acc = acc * jnp.exp2(m - m_new) + _dot(p, vaug)
            m = m_new

        den = acc[:, LANES:2 * LANES]
        if has_sink:
            den = den + jnp.exp2(snk - m)
        o = acc[:, 0:LANES] / den
        low = lax.broadcasted_iota(jnp.int32, (tq, LANES), 1) < HEAD_DIM
        o0, o1, o2, o3 = (o[h * tq:(h + 1) * tq] for h in range(4))
        c0 = jnp.where(low, o0, pltpu.roll(o1, HEAD_DIM, 1))
        c1 = jnp.where(low, pltpu.roll(o2, HEAD_DIM, 1), o3)
        o_ref[:, 0:LANES] = c0.astype(o_ref.dtype)
        o_ref[:, LANES:2 * LANES] = c1.astype(o_ref.dtype)
    return kern


def _attention(q_stack, k_lat, v_lat, k_ctx, v_ctx, sink, prev_out, *, nb, seq, tq, row_off,
               window, name):
    t = q_stack.shape[1]
    nq = seq // tq
    qoff = row_off // tq
    boff = row_off // seq
    n_ctx = 0 if k_ctx is None else k_ctx.shape[1]
    window_len = min(seq, tq + 2 * WINDOW) if window else 0
    kern = _make_attn_kernel(tq, n_ctx, seq, window_len, sink is not None)
    in_specs, args = [], []
    if sink is not None:
        in_specs.append(pl.BlockSpec(memory_space=pltpu.SMEM))
        args.append(sink)
    in_specs.append(pl.BlockSpec((4, tq, LANES), lambda b, j: (0, qoff + b * nq + j, 0)))
    args.append(q_stack)
    if n_ctx:
        ctx_spec = pl.BlockSpec((None, n_ctx, LANES), lambda b, j: (b, 0, 0))
        in_specs += [ctx_spec, ctx_spec]
        args += [k_ctx, v_ctx]
    lat_spec = pl.BlockSpec((seq, LANES), lambda b, j: (boff + b, 0))
    in_specs += [lat_spec, lat_spec]
    args += [k_lat, v_lat]
    if prev_out is None:
        prev_out = jnp.zeros((t, 2 * LANES), BF16)
    in_specs.append(pl.BlockSpec(memory_space=pl.ANY))
    args.append(prev_out)
    aliases = {len(args) - 1: 0}
    return pl.pallas_call(
        kern,
        grid=(nb, nq),
        in_specs=in_specs,
        out_specs=pl.BlockSpec((tq, 2 * LANES), lambda b, j: (qoff + b * nq + j, 0)),
        out_shape=jax.ShapeDtypeStruct((t, 2 * LANES), BF16),
        scratch_shapes=[pltpu.VMEM((n_ctx + seq, 2 * LANES), BF16)],
        input_output_aliases=aliases,
        compiler_params=_cparams(("arbitrary", "arbitrary"), 48),
        name=name,
    )(*args)


def _split_bf16(v, n):
    parts = []
    for _ in range(n):
        part = v.astype(BF16)
        parts.append(part)
        v = v - part.astype(F32)
    return parts


def _dot_parts(a, parts):
    out = _dot(a, parts[0])
    for part in parts[1:]:
        out = out + _dot(a, part)
    return out


def _parts_dot(parts, b):
    out = _dot(parts[0], b)
    for part in parts[1:]:
        out = out + _dot(part, b)
    return out


def _ssd_chunk(direction, xc, dt_raw, dtb_ref, alog_ref, tri_ref, ewide_ref, epair_ref, h_scr):
    q = SSM_CHUNK
    xs = xc[:, 0:D_SSM]
    bm = xc[:, D_SSM:D_SSM + SSM_GROUPS * SSM_STATE]
    cm = xc[:, D_SSM + SSM_GROUPS * SSM_STATE:]

    dtx = dt_raw + dtb_ref[...]
    dt = jnp.maximum(dtx, 0.0) + jnp.log1p(jnp.exp(-jnp.abs(dtx)))
    dta = dt * (-LOG2E * jnp.exp(alog_ref[...]))
    a_col = _dot_parts(tri_ref[...], _split_bf16(dta, 3))
    a_row = a_col.T
    a_wide = _parts_dot(_split_bf16(a_col, 3), ewide_ref[...])
    dt_lanes = _parts_dot(_split_bf16(dt, 2), epair_ref[...])
    row = lax.broadcasted_iota(jnp.int32, (q, q), 0)
    col = lax.broadcasted_iota(jnp.int32, (q, q), 1)
    live = (row >= col) if direction == 0 else (col >= row)
    low_lane = col < SSM_HEAD_DIM
    low_row = row < SSM_HEAD_DIM
    edge = q - 1 if direction == 0 else 0

    group = lambda v, g: v[:, g * SSM_STATE:(g + 1) * SSM_STATE]
    cm_b = [group(cm, g).astype(BF16) for g in range(SSM_GROUPS)]
    bm_b = [group(bm, g).astype(BF16) for g in range(SSM_GROUPS)]
    cbs = [_dot_nt(cm_b[g], bm_b[g]) for g in range(SSM_GROUPS)]

    rep = SSM_HEADS // SSM_GROUPS
    y_pairs = []
    for p in range(SSM_HEADS // 2):
        g = (2 * p) // rep
        hl = [direction * SSM_HEADS + 2 * p, direction * SSM_HEADS + 2 * p + 1]
        a_head = [a_wide[:, (2 * p + k) * LANES:(2 * p + k + 1) * LANES] for k in range(2)]
        a_pair = jnp.where(low_lane, a_head[0], a_head[1])
        a_edge = a_pair[edge:edge + 1, :]
        xdt = xs[:, p * LANES:(p + 1) * LANES] * dt_lanes[:, p * LANES:(p + 1) * LANES]
        hs = h_scr[p]
        m_both = jnp.concatenate(
            [(cbs[g] * jnp.exp2(jnp.where(live, a_head[k] - a_row[hl[k]:hl[k] + 1, :], NEG_BIG))
              ).astype(BF16) for k in range(2)], axis=1)
        x_both = jnp.concatenate([jnp.where(low_lane, xdt, 0.0).astype(BF16),
                                  jnp.where(low_lane, 0.0, xdt).astype(BF16)], axis=0)
        yd = _dot(m_both, x_both)
        yo = _dot_nt(cm_b[g], hs.astype(BF16)) * jnp.exp2(a_pair)
        y_pairs.append(yd + yo)
        st = _dot((xdt * jnp.exp2(a_edge - a_pair)).T.astype(BF16), bm_b[g])
        carry = [jnp.exp2(a_col[edge:edge + 1, l:l + 1]) for l in hl]
        h_scr[p] = hs * jnp.where(low_row, carry[0], carry[1]) + st
    return jnp.concatenate(y_pairs, axis=1), xs


def _make_ssd_kernel(ns, step):
    q = SSM_CHUNK
    halo = SUBLANES
    n_sub = step // q

    def kern(xp_ref, xm_ref, xn_ref, dt_ref, z_ref, cw_ref, cb_ref, dtb_ref, alog_ref,
             trif_ref, trib_ref, ewf_ref, ewb_ref, epf_ref, epb_ref,
             h0f_ref, h0b_ref, dsk_ref, ng_ref, _alias_ref,
             y_ref, hf_ref, hb_ref, ext_scr, xc_scr, yf_scr, h_scr):
        s = pl.program_id(1)

        @pl.when(s == 0)
        def _():
            h_scr[...] = h0f_ref[...]

        @pl.when(s == ns)
        def _():
            h_scr[...] = h0b_ref[...]

        @pl.when(s < ns)
        def _():
            ext_scr[0:halo, :] = jnp.where(s > 0, xp_ref[...], 0.0)
            ext_scr[halo:halo + step, :] = xm_ref[...]
            ext_scr[halo + step:2 * halo + step, :] = jnp.where(s < ns - 1, xn_ref[...], 0.0)
            ext = ext_scr[...]
            acc = cb_ref[...] + cw_ref[D_CONV // 2:D_CONV // 2 + 1, :] * ext[halo:halo + step]
            for k in range(D_CONV):
                if k != D_CONV // 2:
                    shifted = pltpu.roll(ext, (D_CONV // 2 - k) % (step + 2 * halo), 0)
                    acc = acc + cw_ref[k:k + 1, :] * shifted[halo:halo + step]
            xc = _silu(acc)
            r0 = pl.multiple_of(s * step, step)
            xc_scr[pl.ds(r0, step), :] = xc
            for c in range(n_sub):
                rows = slice(c * q, (c + 1) * q)
                y, xs = _ssd_chunk(0, xc[rows], dt_ref[rows, :], dtb_ref, alog_ref, trif_ref,
                                   ewf_ref, epf_ref, h_scr)
                yf_scr[pl.ds(r0 + c * q, q), :] = y + dsk_ref[...] * xs

            @pl.when(s == ns - 1)
            def _():
                hf_ref[...] = h_scr[...]

        @pl.when(s >= ns)
        def _():
            r0 = pl.multiple_of((2 * ns - 1 - s) * step, step)
            gw = D_SSM // SSM_GROUPS
            for c in reversed(range(n_sub)):
                rows = slice(c * q, (c + 1) * q)
                y, _ = _ssd_chunk(1, xc_scr[pl.ds(r0 + c * q, q), :], dt_ref[rows, :], dtb_ref,
                                  alog_ref, trib_ref, ewb_ref, epb_ref, h_scr)
                gated = (yf_scr[pl.ds(r0 + c * q, q), :] + y) * _silu(z_ref[rows, :])
                for g in range(SSM_GROUPS):
                    part = gated[:, g * gw:(g + 1) * gw]
                    ms = jnp.mean(part * part, axis=-1, keepdims=True)
                    y_ref[rows, g * gw:(g + 1) * gw] = (
                        part * lax.rsqrt(ms + NORM_EPS) * ng_ref[:, g * gw:(g + 1) * gw]).astype(y_ref.dtype)

            @pl.when(s == 2 * ns - 1)
            def _():
                hb_ref[...] = h_scr[...]
    return kern


def _ssd(xbc, dt, z, lay, h0f, h0b, prev_out, *, nb, seq, row_off):
    t = xbc.shape[0]
    step = min(seq, SSD_STEP)
    ns = seq // step
    soff = row_off // step
    last_halo = t // SUBLANES - 1
    per = step // SUBLANES

    fwd_blk = lambda b, s: soff + b * ns + jnp.minimum(s, ns - 1)
    any_blk = lambda b, s: soff + b * ns + jnp.where(s < ns, s, 2 * ns - 1 - s)
    bwd_blk = lambda b, s: soff + b * ns + jnp.where(s < ns, ns - 1, 2 * ns - 1 - s)

    full = lambda a: pl.BlockSpec(a.shape, lambda b, s: (0,) * a.ndim)
    prev = pl.BlockSpec((SUBLANES, XBC_DIM), lambda b, s: (jnp.maximum(fwd_blk(b, s) * per - 1, 0), 0))
    nxt = pl.BlockSpec((SUBLANES, XBC_DIM),
                       lambda b, s: (jnp.minimum((fwd_blk(b, s) + 1) * per, last_halo), 0))
    state = pl.BlockSpec((None, SSM_HEADS // 2, LANES, SSM_STATE), lambda b, s: (b, 0, 0, 0))
    out_spec = pl.BlockSpec((step, D_SSM), lambda b, s: (bwd_blk(b, s), 0))
    consts = [lay["conv_w"], lay["conv_b"], lay["dt_bias"], lay["a_log"], lay["tri_fwd"], lay["tri_bwd"],
              lay["wide_fwd"], lay["wide_bwd"], lay["pair_fwd"], lay["pair_bwd"]]
    if prev_out is None:
        prev_out = jnp.zeros((t, D_SSM), BF16)
    state_shape = jax.ShapeDtypeStruct((nb, SSM_HEADS // 2, LANES, SSM_STATE), F32)
    return pl.pallas_call(
        _make_ssd_kernel(ns, step),
        grid=(nb, 2 * ns),
        in_specs=[prev, pl.BlockSpec((step, XBC_DIM), lambda b, s: (fwd_blk(b, s), 0)), nxt,
                  pl.BlockSpec((step, LANES), lambda b, s: (any_blk(b, s), 0)), out_spec]
                 + [full(a) for a in consts]
                 + [state, state, full(lay["d_skip"]), full(lay["ssm_norm_g"]),
                    pl.BlockSpec(memory_space=pl.ANY)],
        out_specs=[out_spec, state, state],
        out_shape=[jax.ShapeDtypeStruct((t, D_SSM), BF16), state_shape, state_shape],
        scratch_shapes=[pltpu.VMEM((step + 2 * SUBLANES, XBC_DIM), F32),
                        pltpu.VMEM((seq, XBC_DIM), F32),
                        pltpu.VMEM((seq, D_SSM), F32),
                        pltpu.VMEM((SSM_HEADS // 2, LANES, SSM_STATE), F32)],
        input_output_aliases={5 + len(consts) + 4: 0},
        compiler_params=_cparams(("arbitrary", "arbitrary"), 48),
        name="ssd",
    )(xbc, xbc, xbc, dt, z, *consts, h0f, h0b, lay["d_skip"], lay["ssm_norm_g"], prev_out)


def _make_outproj_kernel(n_x, npt):
    def kern(*refs):
        oa_ref, ob_ref, oc_ref = refs[:3]
        xs = refs[3:3 + n_x]
        gate_ref, w_ref, g_ref, sh_ref, sc_ref, xo_ref, h_ref = refs[3 + n_x:]
        na = ATT_HEADS * HEAD_DIM
        o = (_dot(oa_ref[...], w_ref[0:na, :]) + _dot(ob_ref[...], w_ref[na:na + D_SSM, :])
             + _dot(oc_ref[...], w_ref[na + D_SSM:, :]))
        xn = _pick_x(pl.program_id(0), npt, xs) + gate_ref[...] * o
        xo_ref[...] = xn
        h_ref[...] = _norm_mod(xn, g_ref[...], sh_ref[...], sc_ref[...]).astype(h_ref.dtype)
    return kern


def _outproj(oa, ob, oc, x, lay, modv, layer, dims, h_dtype):
    t, d = dims["tokens"], x[0].shape[1]
    npt = dims["prompt_tokens"] // TM
    row_of_tile = dims["row_of_tile"](TM)
    tok = lambda w: pl.BlockSpec((TM, w), lambda i: (i, 0))
    full = lambda a: pl.BlockSpec(a.shape, lambda i: (0,) * a.ndim)
    return pl.pallas_call(
        _make_outproj_kernel(len(x), npt),
        grid=(t // TM,),
        in_specs=[tok(oa.shape[1]), tok(ob.shape[1]), tok(oc.shape[1])] + _x_specs(x, TM, npt) + [
            _mod_spec(layer, 2, d, row_of_tile), full(lay["w_out"]), full(lay["norm2_g"]),
            _mod_spec(layer, 3, d, row_of_tile), _mod_spec(layer, 4, d, row_of_tile)],
        out_specs=[tok(d), tok(d)],
        out_shape=[jax.ShapeDtypeStruct((t, d), F32), jax.ShapeDtypeStruct((t, d), h_dtype)],
        compiler_params=_cparams(("arbitrary",), 40),
        name="outproj",
    )(oa, ob, oc, *x, modv, lay["w_out"], lay["norm2_g"], modv, modv)


def _swiglu_tile(x, wg_ref, wu_ref, wd_ref):
    f = wg_ref.shape[1]
    part = None
    for c0 in range(0, f, FFN_SLICE):
        c1 = min(f, c0 + FFN_SLICE)
        g = _dot(x, wg_ref[:, c0:c1].astype(BF16))
        u = _dot(x, wu_ref[:, c0:c1].astype(BF16))
        d = _dot((_silu(g) * u).astype(BF16), wd_ref[c0:c1, :].astype(BF16))
        part = d if part is None else part + d
    return part


def _ffn_kernel(h_ref, wg_ref, wu_ref, wd_ref, x_ref, gate_ref, o_ref):
    o_ref[...] = x_ref[...] + gate_ref[...] * _swiglu_tile(h_ref[...], wg_ref, wu_ref, wd_ref)


def _dense_ffn(h, x, wg, wu, wd, modv, layer, dims):
    t, d = x.shape
    f = wg.shape[1]
    row_of_tile = dims["row_of_tile"](TM)
    gate_base = (layer * 6 + 5) * MOD_ROWS
    resident = lambda shape: pl.BlockSpec(shape, lambda i: (0, 0), pipeline_mode=pl.Buffered(1))
    return pl.pallas_call(
        _ffn_kernel,
        grid=(t // TM,),
        in_specs=[pl.BlockSpec((TM, d), lambda i: (i, 0)),
                  resident((d, f)), resident((d, f)), resident((f, d)),
                  pl.BlockSpec((TM, d), lambda i: (i, 0)),
                  pl.BlockSpec((None, 1, d), lambda i: (gate_base + row_of_tile(i), 0, 0))],
        out_specs=pl.BlockSpec((TM, d), lambda i: (i, 0)),
        out_shape=jax.ShapeDtypeStruct((t, d), F32),
        compiler_params=_cparams(("arbitrary",), 56),
        name="dense_ffn",
    )(h, wg, wu, wd, x, modv)


def _route_kernel(h_ref, r_ref, tri_ref, info_ref, w_ref, cnt_ref):
    h_hi, h_lo = _split_bf16(h_ref[...], 2)
    r_hi, r_lo = _split_bf16(r_ref[...], 2)
    logits = _dot(h_hi, r_hi) + (_dot(h_lo, r_hi) + _dot(h_hi, r_lo))
    tm = logits.shape[0]
    lane = lax.broadcasted_iota(jnp.int32, (tm, LANES), 1)
    logits = jnp.where(lane < N_EXPERTS, logits, NEG_BIG)
    l1 = jnp.max(logits, axis=-1, keepdims=True)
    e1 = jnp.min(jnp.where(logits == l1, lane, LANES), axis=-1, keepdims=True)
    rest = jnp.where(lane == e1, NEG_BIG, logits)
    l2 = jnp.max(rest, axis=-1, keepdims=True)
    e2 = jnp.min(jnp.where(rest == l2, lane, LANES), axis=-1, keepdims=True)
    ex = jnp.exp(l2 - l1)
    w1 = 1.0 / (1.0 + ex)
    w2 = ex / (1.0 + ex)
    member = jnp.where((lane == e1) | (lane == e2), 1.0, 0.0)
    cnt = jnp.sum(member, axis=0, keepdims=True)
    run = jnp.floor((cnt + (SUBLANES - 1)) * (1.0 / SUBLANES)) * SUBLANES
    run_start = _dot_nt(jnp.broadcast_to(run, (SUBLANES, LANES)).astype(BF16),
                        tri_ref[0:LANES, 0:LANES])[0:1, :]
    before = _dot(tri_ref[...], member.astype(BF16)) + run_start
    r1 = jnp.sum(jnp.where(lane == e1, before, 0.0), axis=-1, keepdims=True)
    r2 = jnp.sum(jnp.where(lane == e2, before, 0.0), axis=-1, keepdims=True)
    info = jnp.where(lane == 0, e1.astype(F32),
                     jnp.where(lane == 1, e2.astype(F32),
                               jnp.where(lane == 2, r1, jnp.where(lane == 3, r2, 0.0))))
    info_ref[...] = info.astype(jnp.int32)
    w_ref[...] = jnp.where(lane == 0, w1, jnp.where(lane == 1, w2, 0.0))
    cnt_ref[...] = cnt


def _route(h, router_pad, tri):
    t, d = h.shape
    return pl.pallas_call(
        _route_kernel,
        grid=(t // TM,),
        in_specs=[pl.BlockSpec((TM, d), lambda i: (i, 0)),
                  pl.BlockSpec(router_pad.shape, lambda i: (0, 0)),
                  pl.BlockSpec(tri.shape, lambda i: (0, 0))],
        out_specs=[pl.BlockSpec((TM, LANES), lambda i: (i, 0)),
                   pl.BlockSpec((TM, LANES), lambda i: (i, 0)),
                   pl.BlockSpec((None, 1, LANES), lambda i: (i, 0, 0))],
        out_shape=[jax.ShapeDtypeStruct((t, LANES), jnp.int32),
                   jax.ShapeDtypeStruct((t, LANES), F32),
                   jax.ShapeDtypeStruct((t // TM, 1, LANES), F32)],
        compiler_params=_cparams(("arbitrary",), 40),
        name="moe_route",
    )(h, router_pad, tri)


def _group_copies(src, src_row, dst, dst_row, n_groups, sem, wait=False, same_src=False):
    def body(g, carry):
        src_g = 0 if (same_src or wait) else g
        dst_g = 0 if wait else g
        copy = pltpu.make_async_copy(
            src.at[pl.ds(pl.multiple_of(src_row + src_g * SUBLANES, SUBLANES), SUBLANES), :],
            dst.at[pl.ds(pl.multiple_of(dst_row + dst_g * SUBLANES, SUBLANES), SUBLANES), :], sem)
        if wait:
            copy.wait()
        else:
            copy.start()
        return carry

    lax.fori_loop(0, n_groups, body, 0)


def _dispatch_kernel(seg_ref, off_ref, ng_ref, gap_ref, pos_ref, h_ref, xb_ref,
                     sorted_ref, zero_ref, sem, zsem):
    s = pl.program_id(0)
    rows, tm = sorted_ref.shape[0], h_ref.shape[0]
    slot = lax.broadcasted_iota(jnp.int32, (rows, tm), 0)
    place = jnp.where(slot == pos_ref[0:1, :], 1.0, jnp.where(slot == pos_ref[1:2, :], 1.0, 0.0))
    sorted_ref[...] = _dot(place.astype(BF16), h_ref[...].astype(BF16))

    def runs(wait):
        for e in range(N_EXPERTS):
            k = s * N_EXPERTS + e
            _group_copies(sorted_ref, 0 if wait else off_ref[k], xb_ref, 0 if wait else seg_ref[k],
                          ng_ref[k], sem, wait)

    runs(False)

    @pl.when(s == 0)
    def _():
        zero_ref[...] = jnp.zeros_like(zero_ref)
        for wait in (False, True):
            for g in range(N_EXPERTS + 1):
                _group_copies(zero_ref, 0, xb_ref, 0 if wait else gap_ref[2 * g], gap_ref[2 * g + 1],
                              zsem, wait, same_src=True)

    runs(True)


def _dispatch(h, pos_rows, seg, off, ngroups, gaps, n_rows):
    t, d = h.shape
    rows = TM * TOP_K + N_EXPERTS * SUBLANES
    return pl.pallas_call(
        _dispatch_kernel,
        grid_spec=pltpu.PrefetchScalarGridSpec(
            num_scalar_prefetch=4,
            grid=(t // TM,),
            in_specs=[pl.BlockSpec((None, TOP_K, TM), lambda i, *_: (i, 0, 0)),
                      pl.BlockSpec((TM, d), lambda i, *_: (i, 0))],
            out_specs=pl.BlockSpec(memory_space=pl.ANY),
            scratch_shapes=[pltpu.VMEM((rows, d), F32), pltpu.VMEM((SUBLANES, d), F32),
                            pltpu.SemaphoreType.DMA(()), pltpu.SemaphoreType.DMA(())]),
        out_shape=jax.ShapeDtypeStruct((n_rows, d), F32),
        compiler_params=_cparams(("arbitrary",), 48),
        name="moe_dispatch",
    )(seg, off, ngroups, gaps, pos_rows, h)


def _expert_kernel(be_ref, bv_ref, x_ref, wg_ref, wu_ref, wd_ref, y_ref, xb_scr):
    i = pl.program_id(0)
    j = pl.program_id(1)

    @pl.when(bv_ref[i] > 0)
    def _():
        @pl.when(j == 0)
        def _():
            xb_scr[...] = x_ref[...].astype(BF16)
            y_ref[...] = jnp.zeros_like(y_ref)

        y_ref[...] += _swiglu_tile(xb_scr[...], wg_ref, wu_ref, wd_ref)

    @pl.when((bv_ref[i] == 0) & (j == 0))
    def _():
        y_ref[...] = jnp.zeros_like(y_ref)


def _experts(xb, block_e, block_v, wg, wu, wd):
    n_rows, d = xb.shape
    f = wg.shape[2]

    def jj(i, j, bv):
        return jnp.where(bv[i] > 0, j, 0)

    return pl.pallas_call(
        _expert_kernel,
        grid_spec=pltpu.PrefetchScalarGridSpec(
            num_scalar_prefetch=2,
            grid=(n_rows // MOE_BM, f // MOE_TF),
            in_specs=[pl.BlockSpec((MOE_BM, d), lambda i, j, be, bv: (i, 0)),
                      pl.BlockSpec((None, d, MOE_TF), lambda i, j, be, bv: (be[i], 0, jj(i, j, bv))),
                      pl.BlockSpec((None, d, MOE_TF), lambda i, j, be, bv: (be[i], 0, jj(i, j, bv))),
                      pl.BlockSpec((None, MOE_TF, d), lambda i, j, be, bv: (be[i], jj(i, j, bv), 0))],
            out_specs=pl.BlockSpec((MOE_BM, d), lambda i, j, be, bv: (i, 0)),
            scratch_shapes=[pltpu.VMEM((MOE_BM, d), BF16)]),
        out_shape=jax.ShapeDtypeStruct((n_rows, d), F32),
        compiler_params=_cparams(("arbitrary", "arbitrary"), 56),
        name="moe_experts",
    )(block_e, block_v, xb, wg, wu, wd)


def _make_combine_kernel(npt):
    def kern(seg_ref, off_ref, ng_ref, yb_ref, pos_ref, w_ref, x_ref, gate_ref, fg_ref,
             op_ref, os_ref, runs_ref, sems):
        i = pl.program_id(0)
        n = pl.num_programs(0)
        rows = runs_ref.shape[1]
        tm = x_ref.shape[0]

        def fetch(tile, slot, wait=False):
            for e in range(N_EXPERTS):
                k = tile * N_EXPERTS + e
                _group_copies(yb_ref, 0 if wait else seg_ref[k], runs_ref.at[slot],
                              0 if wait else off_ref[k], ng_ref[k], sems.at[slot], wait)

        @pl.when(i == 0)
        def _():
            runs_ref[...] = jnp.zeros_like(runs_ref)
            fetch(0, 0)

        @pl.when(i + 1 < n)
        def _():
            fetch(i + 1, (i + 1) % 2)

        slot = i % 2
        fetch(i, slot, wait=True)
        y_runs = runs_ref[slot].astype(BF16)
        col = lax.broadcasted_iota(jnp.int32, (tm, rows), 1)
        pos = pos_ref[...]
        picked = [_dot(jnp.where(col == pos[:, k:k + 1], 1.0, 0.0).astype(BF16), y_runs)
                  for k in range(TOP_K)]
        w = w_ref[...]
        y = w[:, 0:1] * picked[0] + w[:, 1:2] * picked[1]
        xn = x_ref[...] + gate_ref[...] * y
        ms = jnp.mean(xn * xn, axis=-1, keepdims=True)
        out = xn * lax.rsqrt(ms + NORM_EPS) * fg_ref[...]

        @pl.when(i < npt)
        def _():
            op_ref[...] = out

        @pl.when(i >= npt)
        def _():
            os_ref[...] = out
    return kern


def _combine(yb, pos, seg, off, ngroups, top_w, x, modv, layer, final_g, dims):
    t, d = x.shape
    tp = dims["prompt_tokens"]
    npt = tp // TM
    rows = TM * TOP_K + N_EXPERTS * SUBLANES
    row_of_tile = dims["row_of_tile"](TM)
    gate_base = (layer * 6 + 5) * MOD_ROWS
    return pl.pallas_call(
        _make_combine_kernel(npt),
        grid_spec=pltpu.PrefetchScalarGridSpec(
            num_scalar_prefetch=3,
            grid=(t // TM,),
            in_specs=[pl.BlockSpec(memory_space=pl.ANY),
                      pl.BlockSpec((TM, TOP_K), lambda i, *_: (i, 0)),
                      pl.BlockSpec((TM, LANES), lambda i, *_: (i, 0)),
                      pl.BlockSpec((TM, d), lambda i, *_: (i, 0)),
                      pl.BlockSpec((None, 1, d), lambda i, *_: (gate_base + row_of_tile(i), 0, 0)),
                      pl.BlockSpec((1, d), lambda i, *_: (0, 0))],
            out_specs=[pl.BlockSpec((TM, d), lambda i, *_: (jnp.minimum(i, npt - 1), 0)),
                       pl.BlockSpec((TM, d), lambda i, *_: (jnp.maximum(i - npt, 0), 0))],
            scratch_shapes=[pltpu.VMEM((2, rows, d), F32), pltpu.SemaphoreType.DMA((2,))]),
        out_shape=[jax.ShapeDtypeStruct((tp, d), F32), jax.ShapeDtypeStruct((t - tp, d), F32)],
        compiler_params=_cparams(("arbitrary",), 56),
        name="moe_combine",
    )(seg, off, ngroups, yb, pos, top_w, x, modv, final_g)


def _moe_layer(h, x, router, wg, wu, wd, modv, layer, final_g, dims):
    t, d = x.shape
    n_tiles = t // TM
    router_pad = jnp.zeros((d, LANES), F32).at[:, :N_EXPERTS].set(router)
    info, top_w, counts = _route(h, router_pad, dims["tri_strict"])
    run = (counts[:, 0, :N_EXPERTS].astype(jnp.int32) + SUBLANES - 1) // SUBLANES * SUBLANES
    off = jnp.cumsum(run, axis=1) - run
    before = jnp.cumsum(run, axis=0) - run
    total = jnp.sum(run, axis=0)
    padded = (total + MOE_BM - 1) // MOE_BM * MOE_BM
    pad_end = jnp.cumsum(padded)
    pad_start = pad_end - padded
    seg = pad_start[None, :] + before
    n_rows = (t * TOP_K + n_tiles * N_EXPERTS * (SUBLANES - 1) + MOE_BM - 1) // MOE_BM * MOE_BM \
        + N_EXPERTS * MOE_BM
    pos = info[:, TOP_K:2 * TOP_K]
    pos_rows = pos.reshape(n_tiles, TM, TOP_K).transpose(0, 2, 1)
    flat = lambda a: a.reshape(-1).astype(jnp.int32)
    gap_lo = jnp.concatenate([pad_start + total, pad_end[-1:]])
    gap_hi = jnp.concatenate([pad_end, jnp.array([n_rows], jnp.int32)])
    gaps = jnp.stack([gap_lo, (gap_hi - gap_lo) // SUBLANES], axis=1)
    block_start = jnp.arange(n_rows // MOE_BM, dtype=jnp.int32) * MOE_BM
    block_e = jnp.sum(pad_end[None, :] <= block_start[:, None], axis=1).astype(jnp.int32)
    block_v = (block_e < N_EXPERTS).astype(jnp.int32)
    last_e = jnp.max(jnp.where(padded > 0, jnp.arange(N_EXPERTS), 0)).astype(jnp.int32)
    block_e = jnp.where(block_v > 0, block_e, last_e)
    xb = _dispatch(h, pos_rows, flat(seg), flat(off), flat(run // SUBLANES), flat(gaps), n_rows)
    yb = _experts(xb, block_e, block_v, wg, wu, wd)
    return _combine(yb, pos, flat(seg), flat(off), flat(run // SUBLANES), top_w, x, modv, layer,
                    final_g, dims)


def _rope_tables(dec_seq):
    n_rows = dec_seq // GRID_W
    row = jnp.repeat(jnp.arange(n_rows, dtype=F32), GRID_W)
    col = (jnp.arange(dec_seq) % GRID_W).astype(F32)
    nf = HEAD_DIM // 4
    inv = ROPE_THETA ** (-jnp.arange(nf, dtype=F32) / nf)
    ang = jnp.stack([row[:, None] * inv, col[:, None] * inv], axis=1)
    cos = jnp.cos(ang)
    sin = jnp.sin(ang)
    cos_h = jnp.concatenate([cos, cos], axis=-1).reshape(dec_seq, HEAD_DIM)
    sin_h = jnp.concatenate([-sin, sin], axis=-1).reshape(dec_seq, HEAD_DIM)
    cos_t = jnp.concatenate([cos_h, cos_h], axis=-1)
    sin_t = jnp.concatenate([sin_h, sin_h], axis=-1)
    cos_t = jnp.concatenate([cos_t, jnp.ones((TM, LANES), F32)], axis=0)
    sin_t = jnp.concatenate([sin_t, jnp.zeros((TM, LANES), F32)], axis=0)
    return cos_t, sin_t


def _expansion(direction, width):
    src = direction * SSM_HEADS + np.arange(SSM_HEADS * width) // width
    return jnp.asarray((np.arange(LANES)[:, None] == src[None, :]).astype(np.float32)).astype(BF16)


def _layer_params(l, norm1_g, norm2_g, w_in, q_norm_g, k_norm_g, conv_w, conv_b, a_log, dt_bias,
                  d_skip, ssm_norm_g, sink, w_out):
    d = w_in.shape[1]
    pad = lambda v: jnp.zeros((1, LANES), F32).at[0, :v.size].set(v.reshape(-1))
    seg = np.arange(LANES) // HEAD_DIM
    bd = jnp.asarray((seg[:, None] == seg[None, :]).astype(np.float32) / HEAD_DIM).astype(BF16)
    r = np.arange(SSM_CHUNK)
    return {
        "norm1_g": norm1_g[l].reshape(1, d), "norm2_g": norm2_g[l].reshape(1, d),
        "w_main": w_in[l, :, :OFF_DT].astype(BF16),
        "w_dt": jnp.zeros((d, LANES), F32).at[:, :2 * SSM_HEADS].set(w_in[l, :, OFF_DT:]).astype(BF16),
        "bd": bd,
        "q_norm_g": jnp.tile(q_norm_g[l], 2).reshape(1, LANES),
        "k_norm_g": jnp.tile(k_norm_g[l], 2).reshape(1, LANES),
        "conv_w": jnp.zeros((SUBLANES, XBC_DIM), F32).at[:D_CONV].set(conv_w[l]),
        "conv_b": conv_b[l].reshape(1, XBC_DIM),
        "a_log": pad(a_log[l]), "dt_bias": pad(dt_bias[l]),
        "d_skip": jnp.repeat(d_skip[l], SSM_HEAD_DIM).reshape(1, D_SSM),
        "ssm_norm_g": ssm_norm_g[l].reshape(1, D_SSM),
        "sink": sink[l],
        "w_out": w_out[l].astype(BF16),
        "tri_fwd": jnp.asarray((r[:, None] >= r[None, :]).astype(np.float32)).astype(BF16),
        "tri_bwd": jnp.asarray((r[:, None] <= r[None, :]).astype(np.float32)).astype(BF16),
        "wide_fwd": _expansion(0, LANES), "wide_bwd": _expansion(1, LANES),
        "pair_fwd": _expansion(0, SSM_HEAD_DIM), "pair_bwd": _expansion(1, SSM_HEAD_DIM),
    }


def kernel(x_prompt, x_sample, cache_attn_k, cache_attn_v, cache_win_k, cache_win_v, state_ssm_fwd, state_ssm_bwd, c, c_ctx, norm1_g, norm2_g, w_mod, b_mod, w_in, q_norm_g, k_norm_g, conv_w, conv_b, a_log, dt_bias, d_skip, ssm_norm_g, sink, w_out, ffn_w_gate, ffn_w_up, ffn_w_down, moe_router, moe_w_gate, moe_w_up, moe_w_down, final_g):
    batch, seq, d = x_prompt.shape
    dec_batch, dec_seq, _ = x_sample.shape
    depth = w_in.shape[0]
    past = cache_attn_k.shape[2]
    tp = batch * seq
    t = tp + dec_batch * dec_seq
    assert depth == 2 and tp % (2 * TM) == 0 and dec_seq % (2 * TM) == 0
    assert 1 + dec_batch <= MOD_ROWS and seq % SSM_CHUNK == 0 and tp % dec_seq == 0

    def row_of_tile(tm):
        npt, tpb = tp // tm, dec_seq // tm
        return lambda i: jnp.where(i < npt, 0, 1 + (i - npt) // tpb)

    cos_t, sin_t = _rope_tables(dec_seq)
    rr = np.arange(TM)
    dims = {"tokens": t, "prompt_tokens": tp, "dec_seq": dec_seq, "row_of_tile": row_of_tile,
            "rope_cos": cos_t, "rope_sin": sin_t,
            "tri_strict": jnp.asarray((rr[:, None] > rr[None, :]).astype(np.float32)).astype(BF16)}

    cc = jnp.zeros((MOD_ROWS, d), F32).at[0].set(c_ctx).at[1:1 + dec_batch].set(c)
    mod = _mod_vectors(cc, w_mod, b_mod)
    modv = mod.reshape(depth, MOD_ROWS, 6, d).transpose(0, 2, 1, 3).reshape(depth * 6 * MOD_ROWS, 1, d)

    x = (x_prompt.reshape(tp, d), x_sample.reshape(dec_batch * dec_seq, d))
    zero_state = jnp.zeros((batch, SSM_HEADS // 2, LANES, SSM_STATE), F32)
    caches = [[] for _ in range(6)]
    y = None
    for l in range(depth):
        lay = _layer_params(l, norm1_g, norm2_g, w_in, q_norm_g, k_norm_g, conv_w, conv_b, a_log,
                            dt_bias, d_skip, ssm_norm_g, sink, w_out)
        (qa, ka, va, qc, kc, vc, z, xbc, dt, cka, cva, ckc, cvc) = _inproj(x, lay, modv, l, dims)
        ctx = lambda a: a[:, l].reshape(dec_batch, past, LANES).astype(BF16)

        oa = _attention(qa, ka, va, None, None, None, None, nb=batch, seq=seq, tq=seq, row_off=0,
                        window=False, name="attn_a_ctx")
        oa = _attention(qa, ka, va, ctx(cache_attn_k), ctx(cache_attn_v), None, oa, nb=dec_batch,
                        seq=dec_seq, tq=TQ_GLOBAL, row_off=tp, window=False, name="attn_a_lat")
        oc = _attention(qc, kc, vc, None, None, lay["sink"], None, nb=batch, seq=seq, tq=seq,
                        row_off=0, window=False, name="attn_c_ctx")
        oc = _attention(qc, kc, vc, ctx(cache_win_k), ctx(cache_win_v), lay["sink"], oc,
                        nb=dec_batch, seq=dec_seq, tq=TQ_WINDOW, row_off=tp, window=True,
                        name="attn_c_lat")
        pair = lambda s: s[:, l].reshape(dec_batch, SSM_HEADS // 2, LANES, SSM_STATE)
        ob, hf_p, hb_p = _ssd(xbc, dt, z, lay, zero_state, zero_state, None, nb=batch, seq=seq,
                              row_off=0)
        ob, _, _ = _ssd(xbc, dt, z, lay, pair(state_ssm_fwd), pair(state_ssm_bwd), ob, nb=dec_batch,
                        seq=dec_seq, row_off=tp)

        kv = lambda a: a[:tp].reshape(batch, seq, ATT_KV_HEADS, HEAD_DIM)
        st = lambda s: s.reshape(batch, SSM_HEADS, SSM_HEAD_DIM, SSM_STATE)
        for lst, val in zip(caches, (kv(cka), kv(cva), kv(ckc), kv(cvc), st(hf_p), st(hb_p))):
            lst.append(val)

        i = l // 2
        if l % 2 == 0:
            xr, h2 = _outproj(oa, ob, oc, x, lay, modv, l, dims, BF16)
            x = (_dense_ffn(h2, xr, ffn_w_gate[i].astype(BF16), ffn_w_up[i].astype(BF16),
                            ffn_w_down[i].astype(BF16), modv, l, dims),)
        else:
            xr, h2 = _outproj(oa, ob, oc, x, lay, modv, l, dims, F32)
            y = _moe_layer(h2, xr, moe_router[i], moe_w_gate[i], moe_w_up[i], moe_w_down[i], modv, l,
                           final_g.reshape(1, d), dims)

    y_prompt = y[0].reshape(batch, seq, d)
    y_sample = y[1].reshape(dec_batch, dec_seq, d)
    return (y_prompt, y_sample) + tuple(jnp.stack(lst, axis=1) for lst in caches)
```

```python
import numpy as np
import jax
import jax.numpy as jnp
from jax import lax
from jax.experimental import pallas as pl
from jax.experimental.pallas import tpu as pltpu

F32 = jnp.float32
BF16 = jnp.bfloat16
HIGHEST = lax.Precision.HIGHEST

GRID_W = 64
HEAD_DIM = 64
ATT_HEADS = 4
ATT_KV_HEADS = 2
WIN_HEADS = 4
WIN_KV_HEADS = 2
WINDOW = 128
ROPE_THETA = 10000.0
SSM_HEADS = 8
SSM_HEAD_DIM = 64
D_SSM = SSM_HEADS * SSM_HEAD_DIM
SSM_GROUPS = 2
SSM_STATE = 128
D_CONV = 5
SSM_CHUNK = 128
XBC_DIM = D_SSM + 2 * SSM_GROUPS * SSM_STATE
N_EXPERTS = 8
TOP_K = 2
NORM_EPS = 1e-6
NEG_BIG = -1e30
LOG2E = 1.4426950408889634

LANES = 128
SUBLANES = 8
MOD_ROWS = 16

OFF_QA, OFF_KA, OFF_VA, OFF_QC, OFF_KC, OFF_VC, OFF_Z, OFF_XBC, OFF_DT = (
    0, 256, 384, 512, 768, 896, 1024, 1536, 2560)

TM = 512
TQ_GLOBAL = 512
TQ_WINDOW = 256
ATTN_KEY_CHUNK = 256
ATTN_KEY_CHUNK_WINDOW = 512
MOE_BM = 1024
MOE_TF = 512
SSD_STEP = 512
FFN_SLICE = 256


def _cparams(sem, vmem_mb):
    return pltpu.CompilerParams(dimension_semantics=sem, vmem_limit_bytes=vmem_mb * 1024 * 1024)


def _dot(a, b, precision=None):
    return jnp.dot(a, b, preferred_element_type=F32, precision=precision)


def _dot_nt(a, b):
    return lax.dot_general(a, b, (((1,), (1,)), ((), ())), preferred_element_type=F32)


def _silu(x):
    return x / (1.0 + jnp.exp(-x))


def _norm_mod(x, g, shift, scale):
    ms = jnp.mean(x * x, axis=-1, keepdims=True)
    y = x * lax.rsqrt(ms + NORM_EPS) * g
    return y * (1.0 + scale) + shift


def _pick_x(i, npt, xs):
    if len(xs) == 1:
        return xs[0][...]
    return jnp.where(i < npt, xs[0][...], xs[1][...])


def _x_specs(x, tm, npt):
    d = x[0].shape[1]
    if len(x) == 1:
        return [pl.BlockSpec((tm, d), lambda i, *_: (i, 0))]
    return [pl.BlockSpec((tm, d), lambda i, *_: (jnp.minimum(i, npt - 1), 0)),
            pl.BlockSpec((tm, d), lambda i, *_: (jnp.maximum(i - npt, 0), 0))]


def _mod_kernel(c_ref, w_ref, b_ref, o_ref):
    a = _silu(c_ref[...])
    o_ref[...] = _dot(a, w_ref[...], HIGHEST) + b_ref[...]


def _mod_vectors(cc, w_mod, b_mod):
    depth, d, n = w_mod.shape
    tn = 1536
    return pl.pallas_call(
        _mod_kernel,
        grid=(depth, n // tn),
        in_specs=[pl.BlockSpec((MOD_ROWS, d), lambda l, j: (0, 0)),
                  pl.BlockSpec((None, d, tn), lambda l, j: (l, 0, j)),
                  pl.BlockSpec((None, 1, tn), lambda l, j: (l, 0, j))],
        out_specs=pl.BlockSpec((None, MOD_ROWS, tn), lambda l, j: (l, 0, j)),
        out_shape=jax.ShapeDtypeStruct((depth, MOD_ROWS, n), F32),
        compiler_params=_cparams(("arbitrary", "arbitrary"), 40),
        name="mod_vectors",
    )(cc, w_mod, b_mod.reshape(depth, 1, n))


def _mod_spec(layer, k, d, row_of_tile):
    base = (layer * 6 + k) * MOD_ROWS
    return pl.BlockSpec((None, 1, d), lambda i, *_: (base + row_of_tile(i), 0, 0))


def _make_inproj_kernel(n_x, npt):
    def kern(*refs):
        xs = refs[:n_x]
        (g_ref, sh_ref, sc_ref, w_ref, wdt_ref, bd_ref, qg_ref, kg_ref, cos_ref, sin_ref,
         qa_ref, ka_ref, va_ref, qc_ref, kc_ref, vc_ref, z_ref, xbc_ref, dt_ref,
         cka_ref, cva_ref, ckc_ref, cvc_ref) = refs[n_x:]
        x = _pick_x(pl.program_id(0), npt, xs)
        h = _norm_mod(x, g_ref[...], sh_ref[...], sc_ref[...]).astype(BF16)

        def proj(a, b):
            return _dot(h, w_ref[:, a:b])

        cos = cos_ref[...]
        sin = sin_ref[...]
        tm = cos.shape[0]
        lane = lax.broadcasted_iota(jnp.int32, (tm, LANES), 1)
        first = (lane % 32) < 16
        low = lane < HEAD_DIM
        bd = bd_ref[...]

        def rope(v):
            partner = jnp.where(first, pltpu.roll(v, LANES - 16, 1), pltpu.roll(v, 16, 1))
            return v * cos + partner * sin

        def head_norm(v, g):
            sq = v * v
            hi = sq.astype(BF16)
            lo = (sq - hi.astype(F32)).astype(BF16)
            ms = _dot(hi, bd) + _dot(lo, bd)
            return v * lax.rsqrt(ms + NORM_EPS) * g

        def stack_heads(c0, c1, out_ref):
            zero = jnp.zeros_like(c0)
            out_ref[0] = jnp.where(low, c0, zero).astype(out_ref.dtype)
            out_ref[1] = jnp.where(low, pltpu.roll(c0, HEAD_DIM, 1), zero).astype(out_ref.dtype)
            out_ref[2] = jnp.where(low, zero, pltpu.roll(c1, HEAD_DIM, 1)).astype(out_ref.dtype)
            out_ref[3] = jnp.where(low, zero, c1).astype(out_ref.dtype)

        scale = LOG2E * HEAD_DIM ** -0.5
        qg = qg_ref[...]
        q_a = proj(OFF_QA, OFF_KA)
        qa0 = rope(head_norm(q_a[:, 0:LANES], qg)) * scale
        qa1 = rope(head_norm(q_a[:, LANES:2 * LANES], qg)) * scale
        stack_heads(qa0, qa1, qa_ref)
        kv_a = proj(OFF_KA, OFF_QC)
        ka = rope(head_norm(kv_a[:, 0:LANES], kg_ref[...]))
        ka_ref[...] = ka.astype(ka_ref.dtype)
        cka_ref[...] = ka
        va = kv_a[:, LANES:2 * LANES]
        va_ref[...] = va.astype(va_ref.dtype)
        cva_ref[...] = va

        q_c = proj(OFF_QC, OFF_KC)
        qc0 = rope(q_c[:, 0:LANES]) * scale
        qc1 = rope(q_c[:, LANES:2 * LANES]) * scale
        stack_heads(qc0, qc1, qc_ref)
        kv_c = proj(OFF_KC, OFF_Z)
        kc = rope(kv_c[:, 0:LANES])
        kc_ref[...] = kc.astype(kc_ref.dtype)
        ckc_ref[...] = kc
        vc = kv_c[:, LANES:2 * LANES]
        vc_ref[...] = vc.astype(vc_ref.dtype)
        cvc_ref[...] = vc

        z_ref[...] = proj(OFF_Z, OFF_XBC)
        xbc_ref[...] = proj(OFF_XBC, OFF_DT)
        dt_ref[...] = _dot(h, wdt_ref[...])
    return kern


def _inproj(x, lay, modv, layer, dims):
    t, d = dims["tokens"], x[0].shape[1]
    n_tiles = t // TM
    npt = dims["prompt_tokens"] // TM
    tpb = dims["dec_seq"] // TM
    row_of_tile = dims["row_of_tile"](TM)

    def rope_idx(i):
        return jnp.where(i < npt, tpb, (i - npt) % tpb)

    def cache_idx(i):
        return jnp.minimum(i, npt)

    tok = lambda w: pl.BlockSpec((TM, w), lambda i: (i, 0))
    full = lambda a: pl.BlockSpec(a.shape, lambda i: (0,) * a.ndim)
    stack = pl.BlockSpec((4, TM, LANES), lambda i: (0, i, 0))
    cache = pl.BlockSpec((TM, LANES), lambda i: (cache_idx(i), 0))
    rope = pl.BlockSpec((TM, LANES), lambda i: (rope_idx(i), 0))
    cache_shape = jax.ShapeDtypeStruct(((npt + 1) * TM, LANES), F32)
    return pl.pallas_call(
        _make_inproj_kernel(len(x), npt),
        grid=(n_tiles,),
        in_specs=_x_specs(x, TM, npt) + [
            full(lay["norm1_g"]),
            _mod_spec(layer, 0, d, row_of_tile), _mod_spec(layer, 1, d, row_of_tile),
            full(lay["w_main"]), full(lay["w_dt"]), full(lay["bd"]),
            full(lay["q_norm_g"]), full(lay["k_norm_g"]), rope, rope],
        out_specs=[stack, tok(LANES), tok(LANES), stack, tok(LANES), tok(LANES),
                   tok(D_SSM), tok(XBC_DIM), tok(LANES), cache, cache, cache, cache],
        out_shape=[jax.ShapeDtypeStruct((4, t, LANES), BF16),
                   jax.ShapeDtypeStruct((t, LANES), BF16), jax.ShapeDtypeStruct((t, LANES), BF16),
                   jax.ShapeDtypeStruct((4, t, LANES), BF16),
                   jax.ShapeDtypeStruct((t, LANES), BF16), jax.ShapeDtypeStruct((t, LANES), BF16),
                   jax.ShapeDtypeStruct((t, D_SSM), F32), jax.ShapeDtypeStruct((t, XBC_DIM), F32),
                   jax.ShapeDtypeStruct((t, LANES), F32),
                   cache_shape, cache_shape, cache_shape, cache_shape],
        compiler_params=_cparams(("arbitrary",), 48),
        name="inproj",
    )(*x, lay["norm1_g"], modv, modv, lay["w_main"], lay["w_dt"], lay["bd"],
      lay["q_norm_g"], lay["k_norm_g"], dims["rope_cos"], dims["rope_sin"])


def _make_attn_kernel(tq, n_ctx, n_lat, window_len, has_sink):
    ck = ATTN_KEY_CHUNK_WINDOW if window_len else ATTN_KEY_CHUNK

    def kern(*refs):
        refs = list(refs)
        sink_ref = refs.pop(0) if has_sink else None
        q_ref = refs.pop(0)
        kc_ref = refs.pop(0) if n_ctx else None
        vc_ref = refs.pop(0) if n_ctx else None
        kl_ref, vl_ref = refs.pop(0), refs.pop(0)
        refs.pop(0)
        o_ref, vaug_ref = refs

        @pl.when(pl.program_id(1) == 0)
        def _():
            if n_ctx:
                vaug_ref[0:n_ctx, 0:LANES] = vc_ref[...]
            vaug_ref[n_ctx:n_ctx + n_lat, 0:LANES] = vl_ref[...]
            vaug_ref[:, LANES:2 * LANES] = jnp.ones((n_ctx + n_lat, LANES), BF16)

        rows = 4 * tq
        q = q_ref[...].reshape(rows, LANES)
        if has_sink:
            head = lax.broadcasted_iota(jnp.int32, (rows, 1), 0) // tq
            snk = LOG2E * jnp.where(head == 0, sink_ref[0],
                                    jnp.where(head == 1, sink_ref[1],
                                              jnp.where(head == 2, sink_ref[2], sink_ref[3])))
            m = snk
        else:
            m = jnp.full((rows, 1), NEG_BIG, F32)
        acc = jnp.zeros((rows, 2 * LANES), F32)

        chunks = []
        for c0 in range(0, n_ctx, ck):
            n = min(ck, n_ctx - c0)
            chunks.append((kc_ref[c0:c0 + n, :], vaug_ref[c0:c0 + n, :], None))
        if window_len:
            q0 = pl.program_id(1) * tq
            ws = pl.multiple_of(jnp.clip(q0 - WINDOW, 0, n_lat - window_len), LANES)
            qpos = q0 + lax.broadcasted_iota(jnp.int32, (rows, window_len), 0) % tq
            kpos = ws + lax.broadcasted_iota(jnp.int32, (rows, window_len), 1)
            valid = jnp.abs(qpos - kpos) <= WINDOW
            chunks.append((kl_ref[pl.ds(ws, window_len), :],
                           vaug_ref[pl.ds(n_ctx + ws, window_len), :], valid))
        else:
            for c0 in range(0, n_lat, ck):
                n = min(ck, n_lat - c0)
                chunks.append((kl_ref[c0:c0 + n, :], vaug_ref[n_ctx + c0:n_ctx + c0 + n, :], None))

        for keys, vaug, valid in chunks:
            s = _dot_nt(q, keys)
            if valid is not None:
                s = jnp.where(valid, s, NEG_BIG)
            m_new = jnp.maximum(m, jnp.max(s, axis=-1, keepdims=True))
            p = jnp.exp2(s - m_new).astype(BF16)
            acc = acc * jnp.exp2(m - m_new) + _dot(p, vaug)
            m = m_new

        den = acc[:, LANES:2 * LANES]
        if has_sink:
            den = den + jnp.exp2(snk - m)
        o = acc[:, 0:LANES] / den
        low = lax.broadcasted_iota(jnp.int32, (tq, LANES), 1) < HEAD_DIM
        o0, o1, o2, o3 = (o[h * tq:(h + 1) * tq] for h in range(4))
        c0 = jnp.where(low, o0, pltpu.roll(o1, HEAD_DIM, 1))
        c1 = jnp.where(low, pltpu.roll(o2, HEAD_DIM, 1), o3)
        o_ref[:, 0:LANES] = c0.astype(o_ref.dtype)
        o_ref[:, LANES:2 * LANES] = c1.astype(o_ref.dtype)
    return kern


def _attention(q_stack, k_lat, v_lat, k_ctx, v_ctx, sink, prev_out, *, nb, seq, tq, row_off,
               window, name):
    t = q_stack.shape[1]
    nq = seq // tq
    qoff = row_off // tq
    boff = row_off // seq
    n_ctx = 0 if k_ctx is None else k_ctx.shape[1]
    window_len = min(seq, tq + 2 * WINDOW) if window else 0
    kern = _make_attn_kernel(tq, n_ctx, seq, window_len, sink is not None)
    in_specs, args = [], []
    if sink is not None:
        in_specs.append(pl.BlockSpec(memory_space=pltpu.SMEM))
        args.append(sink)
    in_specs.append(pl.BlockSpec((4, tq, LANES), lambda b, j: (0, qoff + b * nq + j, 0)))
    args.append(q_stack)
    if n_ctx:
        ctx_spec = pl.BlockSpec((None, n_ctx, LANES), lambda b, j: (b, 0, 0))
        in_specs += [ctx_spec, ctx_spec]
        args += [k_ctx, v_ctx]
    lat_spec = pl.BlockSpec((seq, LANES), lambda b, j: (boff + b, 0))
    in_specs += [lat_spec, lat_spec]
    args += [k_lat, v_lat]
    if prev_out is None:
        prev_out = jnp.zeros((t, 2 * LANES), BF16)
    in_specs.append(pl.BlockSpec(memory_space=pl.ANY))
    args.append(prev_out)
    aliases = {len(args) - 1: 0}
    return pl.pallas_call(
        kern,
        grid=(nb, nq),
        in_specs=in_specs,
        out_specs=pl.BlockSpec((tq, 2 * LANES), lambda b, j: (qoff + b * nq + j, 0)),
        out_shape=jax.ShapeDtypeStruct((t, 2 * LANES), BF16),
        scratch_shapes=[pltpu.VMEM((n_ctx + seq, 2 * LANES), BF16)],
        input_output_aliases=aliases,
        compiler_params=_cparams(("arbitrary", "arbitrary"), 48),
        name=name,
    )(*args)


def _split_bf16(v, n):
    parts = []
    for _ in range(n):
        part = v.astype(BF16)
        parts.append(part)
        v = v - part.astype(F32)
    return parts


def _dot_parts(a, parts):
    out = _dot(a, parts[0])
    for part in parts[1:]:
        out = out + _dot(a, part)
    return out


def _parts_dot(parts, b):
    out = _dot(parts[0], b)
    for part in parts[1:]:
        out = out + _dot(part, b)
    return out


def _ssd_chunk(direction, xc, dt_raw, dtb_ref, alog_ref, tri_ref, ewide_ref, epair_ref, h_scr):
    q = SSM_CHUNK
    xs = xc[:, 0:D_SSM]
    bm = xc[:, D_SSM:D_SSM + SSM_GROUPS * SSM_STATE]
    cm = xc[:, D_SSM + SSM_GROUPS * SSM_STATE:]

    dtx = dt_raw + dtb_ref[...]
    dt = jnp.maximum(dtx, 0.0) + jnp.log1p(jnp.exp(-jnp.abs(dtx)))
    dta = dt * (-LOG2E * jnp.exp(alog_ref[...]))
    a_col = _dot_parts(tri_ref[...], _split_bf16(dta, 3))
    a_row = a_col.T
    a_wide = _parts_dot(_split_bf16(a_col, 3), ewide_ref[...])
    dt_lanes = _parts_dot(_split_bf16(dt, 2), epair_ref[...])
    row = lax.broadcasted_iota(jnp.int32, (q, q), 0)
    col = lax.broadcasted_iota(jnp.int32, (q, q), 1)
    live = (row >= col) if direction == 0 else (col >= row)
    low_lane = col < SSM_HEAD_DIM
    low_row = row < SSM_HEAD_DIM
    edge = q - 1 if direction == 0 else 0

    group = lambda v, g: v[:, g * SSM_STATE:(g + 1) * SSM_STATE]
    cm_b = [group(cm, g).astype(BF16) for g in range(SSM_GROUPS)]
    bm_b = [group(bm, g).astype(BF16) for g in range(SSM_GROUPS)]
    cbs = [_dot_nt(cm_b[g], bm_b[g]) for g in range(SSM_GROUPS)]

    rep = SSM_HEADS // SSM_GROUPS
    y_pairs = []
    for p in range(SSM_HEADS // 2):
        g = (2 * p) // rep
        hl = [direction * SSM_HEADS + 2 * p, direction * SSM_HEADS + 2 * p + 1]
        a_head = [a_wide[:, (2 * p + k) * LANES:(2 * p + k + 1) * LANES] for k in range(2)]
        a_pair = jnp.where(low_lane, a_head[0], a_head[1])
        a_edge = a_pair[edge:edge + 1, :]
        xdt = xs[:, p * LANES:(p + 1) * LANES] * dt_lanes[:, p * LANES:(p + 1) * LANES]
        hs = h_scr[p]
        m_both = jnp.concatenate(
            [(cbs[g] * jnp.exp2(jnp.where(live, a_head[k] - a_row[hl[k]:hl[k] + 1, :], NEG_BIG))
              ).astype(BF16) for k in range(2)], axis=1)
        x_both = jnp.concatenate([jnp.where(low_lane, xdt, 0.0).astype(BF16),
                                  jnp.where(low_lane, 0.0, xdt).astype(BF16)], axis=0)
        yd = _dot(m_both, x_both)
        yo = _dot_nt(cm_b[g], hs.astype(BF16)) * jnp.exp2(a_pair)
        y_pairs.append(yd + yo)
        st = _dot((xdt * jnp.exp2(a_edge - a_pair)).T.astype(BF16), bm_b[g])
        carry = [jnp.exp2(a_col[edge:edge + 1, l:l + 1]) for l in hl]
        h_scr[p] = hs * jnp.where(low_row, carry[0], carry[1]) + st
    return jnp.concatenate(y_pairs, axis=1), xs


def _make_ssd_kernel(ns, step):
    q = SSM_CHUNK
    halo = SUBLANES
    n_sub = step // q

    def kern(xp_ref, xm_ref, xn_ref, dt_ref, z_ref, cw_ref, cb_ref, dtb_ref, alog_ref,
             trif_ref, trib_ref, ewf_ref, ewb_ref, epf_ref, epb_ref,
             h0f_ref, h0b_ref, dsk_ref, ng_ref, _alias_ref,
             y_ref, hf_ref, hb_ref, ext_scr, xc_scr, yf_scr, h_scr):
        s = pl.program_id(1)

        @pl.when(s == 0)
        def _():
            h_scr[...] = h0f_ref[...]

        @pl.when(s == ns)
        def _():
            h_scr[...] = h0b_ref[...]

        @pl.when(s < ns)
        def _():
            ext_scr[0:halo, :] = jnp.where(s > 0, xp_ref[...], 0.0)
            ext_scr[halo:halo + step, :] = xm_ref[...]
            ext_scr[halo + step:2 * halo + step, :] = jnp.where(s < ns - 1, xn_ref[...], 0.0)
            ext = ext_scr[...]
            acc = cb_ref[...] + cw_ref[D_CONV // 2:D_CONV // 2 + 1, :] * ext[halo:halo + step]
            for k in range(D_CONV):
                if k != D_CONV // 2:
                    shifted = pltpu.roll(ext, (D_CONV // 2 - k) % (step + 2 * halo), 0)
                    acc = acc + cw_ref[k:k + 1, :] * shifted[halo:halo + step]
            xc = _silu(acc)
            r0 = pl.multiple_of(s * step, step)
            xc_scr[pl.ds(r0, step), :] = xc
            for c in range(n_sub):
                rows = slice(c * q, (c + 1) * q)
                y, xs = _ssd_chunk(0, xc[rows], dt_ref[rows, :], dtb_ref, alog_ref, trif_ref,
                                   ewf_ref, epf_ref, h_scr)
                yf_scr[pl.ds(r0 + c * q, q), :] = y + dsk_ref[...] * xs

            @pl.when(s == ns - 1)
            def _():
                hf_ref[...] = h_scr[...]

        @pl.when(s >= ns)
        def _():
            r0 = pl.multiple_of((2 * ns - 1 - s) * step, step)
            gw = D_SSM // SSM_GROUPS
            for c in reversed(range(n_sub)):
                rows = slice(c * q, (c + 1) * q)
                y, _ = _ssd_chunk(1, xc_scr[pl.ds(r0 + c * q, q), :], dt_ref[rows, :], dtb_ref,
                                  alog_ref, trib_ref, ewb_ref, epb_ref, h_scr)
                gated = (yf_scr[pl.ds(r0 + c * q, q), :] + y) * _silu(z_ref[rows, :])
                for g in range(SSM_GROUPS):
                    part = gated[:, g * gw:(g + 1) * gw]
                    ms = jnp.mean(part * part, axis=-1, keepdims=True)
                    y_ref[rows, g * gw:(g + 1) * gw] = (
                        part * lax.rsqrt(ms + NORM_EPS) * ng_ref[:, g * gw:(g + 1) * gw]).astype(y_ref.dtype)

            @pl.when(s == 2 * ns - 1)
            def _():
                hb_ref[...] = h_scr[...]
    return kern


def _ssd(xbc, dt, z, lay, h0f, h0b, prev_out, *, nb, seq, row_off):
    t = xbc.shape[0]
    step = min(seq, SSD_STEP)
    ns = seq // step
    soff = row_off // step
    last_halo = t // SUBLANES - 1
    per = step // SUBLANES

    fwd_blk = lambda b, s: soff + b * ns + jnp.minimum(s, ns - 1)
    any_blk = lambda b, s: soff + b * ns + jnp.where(s < ns, s, 2 * ns - 1 - s)
    bwd_blk = lambda b, s: soff + b * ns + jnp.where(s < ns, ns - 1, 2 * ns - 1 - s)

    full = lambda a: pl.BlockSpec(a.shape, lambda b, s: (0,) * a.ndim)
    prev = pl.BlockSpec((SUBLANES, XBC_DIM), lambda b, s: (jnp.maximum(fwd_blk(b, s) * per - 1, 0), 0))
    nxt = pl.BlockSpec((SUBLANES, XBC_DIM),
                       lambda b, s: (jnp.minimum((fwd_blk(b, s) + 1) * per, last_halo), 0))
    state = pl.BlockSpec((None, SSM_HEADS // 2, LANES, SSM_STATE), lambda b, s: (b, 0, 0, 0))
    out_spec = pl.BlockSpec((step, D_SSM), lambda b, s: (bwd_blk(b, s), 0))
    consts = [lay["conv_w"], lay["conv_b"], lay["dt_bias"], lay["a_log"], lay["tri_fwd"], lay["tri_bwd"],
              lay["wide_fwd"], lay["wide_bwd"], lay["pair_fwd"], lay["pair_bwd"]]
    if prev_out is None:
        prev_out = jnp.zeros((t, D_SSM), BF16)
    state_shape = jax.ShapeDtypeStruct((nb, SSM_HEADS // 2, LANES, SSM_STATE), F32)
    return pl.pallas_call(
        _make_ssd_kernel(ns, step),
        grid=(nb, 2 * ns),
        in_specs=[prev, pl.BlockSpec((step, XBC_DIM), lambda b, s: (fwd_blk(b, s), 0)), nxt,
                  pl.BlockSpec((step, LANES), lambda b, s: (any_blk(b, s), 0)), out_spec]
                 + [full(a) for a in consts]
                 + [state, state, full(lay["d_skip"]), full(lay["ssm_norm_g"]),
                    pl.BlockSpec(memory_space=pl.ANY)],
        out_specs=[out_spec, state, state],
        out_shape=[jax.ShapeDtypeStruct((t, D_SSM), BF16), state_shape, state_shape],
        scratch_shapes=[pltpu.VMEM((step + 2 * SUBLANES, XBC_DIM), F32),
                        pltpu.VMEM((seq, XBC_DIM), F32),
                        pltpu.VMEM((seq, D_SSM), F32),
                        pltpu.VMEM((SSM_HEADS // 2, LANES, SSM_STATE), F32)],
        input_output_aliases={5 + len(consts) + 4: 0},
        compiler_params=_cparams(("arbitrary", "arbitrary"), 48),
        name="ssd",
    )(xbc, xbc, xbc, dt, z, *consts, h0f, h0b, lay["d_skip"], lay["ssm_norm_g"], prev_out)


def _make_outproj_kernel(n_x, npt):
    def kern(*refs):
        oa_ref, ob_ref, oc_ref = refs[:3]
        xs = refs[3:3 + n_x]
        gate_ref, w_ref, g_ref, sh_ref, sc_ref, xo_ref, h_ref = refs[3 + n_x:]
        na = ATT_HEADS * HEAD_DIM
        o = (_dot(oa_ref[...], w_ref[0:na, :]) + _dot(ob_ref[...], w_ref[na:na + D_SSM, :])
             + _dot(oc_ref[...], w_ref[na + D_SSM:, :]))
        xn = _pick_x(pl.program_id(0), npt, xs) + gate_ref[...] * o
        xo_ref[...] = xn
        h_ref[...] = _norm_mod(xn, g_ref[...], sh_ref[...], sc_ref[...]).astype(h_ref.dtype)
    return kern


def _outproj(oa, ob, oc, x, lay, modv, layer, dims, h_dtype):
    t, d = dims["tokens"], x[0].shape[1]
    npt = dims["prompt_tokens"] // TM
    row_of_tile = dims["row_of_tile"](TM)
    tok = lambda w: pl.BlockSpec((TM, w), lambda i: (i, 0))
    full = lambda a: pl.BlockSpec(a.shape, lambda i: (0,) * a.ndim)
    return pl.pallas_call(
        _make_outproj_kernel(len(x), npt),
        grid=(t // TM,),
        in_specs=[tok(oa.shape[1]), tok(ob.shape[1]), tok(oc.shape[1])] + _x_specs(x, TM, npt) + [
            _mod_spec(layer, 2, d, row_of_tile), full(lay["w_out"]), full(lay["norm2_g"]),
            _mod_spec(layer, 3, d, row_of_tile), _mod_spec(layer, 4, d, row_of_tile)],
        out_specs=[tok(d), tok(d)],
        out_shape=[jax.ShapeDtypeStruct((t, d), F32), jax.ShapeDtypeStruct((t, d), h_dtype)],
        compiler_params=_cparams(("arbitrary",), 40),
        name="outproj",
    )(oa, ob, oc, *x, modv, lay["w_out"], lay["norm2_g"], modv, modv)


def _swiglu_tile(x, wg_ref, wu_ref, wd_ref):
    f = wg_ref.shape[1]
    part = None
    for c0 in range(0, f, FFN_SLICE):
        c1 = min(f, c0 + FFN_SLICE)
        g = _dot(x, wg_ref[:, c0:c1].astype(BF16))
        u = _dot(x, wu_ref[:, c0:c1].astype(BF16))
        d = _dot((_silu(g) * u).astype(BF16), wd_ref[c0:c1, :].astype(BF16))
        part = d if part is None else part + d
    return part


def _ffn_kernel(h_ref, wg_ref, wu_ref, wd_ref, x_ref, gate_ref, o_ref):
    o_ref[...] = x_ref[...] + gate_ref[...] * _swiglu_tile(h_ref[...], wg_ref, wu_ref, wd_ref)


def _dense_ffn(h, x, wg, wu, wd, modv, layer, dims):
    t, d = x.shape
    f = wg.shape[1]
    row_of_tile = dims["row_of_tile"](TM)
    gate_base = (layer * 6 + 5) * MOD_ROWS
    resident = lambda shape: pl.BlockSpec(shape, lambda i: (0, 0), pipeline_mode=pl.Buffered(1))
    return pl.pallas_call(
        _ffn_kernel,
        grid=(t // TM,),
        in_specs=[pl.BlockSpec((TM, d), lambda i: (i, 0)),
                  resident((d, f)), resident((d, f)), resident((f, d)),
                  pl.BlockSpec((TM, d), lambda i: (i, 0)),
                  pl.BlockSpec((None, 1, d), lambda i: (gate_base + row_of_tile(i), 0, 0))],
        out_specs=pl.BlockSpec((TM, d), lambda i: (i, 0)),
        out_shape=jax.ShapeDtypeStruct((t, d), F32),
        compiler_params=_cparams(("arbitrary",), 56),
        name="dense_ffn",
    )(h, wg, wu, wd, x, modv)


def _route_kernel(h_ref, r_ref, tri_ref, info_ref, w_ref, cnt_ref):
    h_hi, h_lo = _split_bf16(h_ref[...], 2)
    r_hi, r_lo = _split_bf16(r_ref[...], 2)
    logits = _dot(h_hi, r_hi) + (_dot(h_lo, r_hi) + _dot(h_hi, r_lo))
    tm = logits.shape[0]
    lane = lax.broadcasted_iota(jnp.int32, (tm, LANES), 1)
    logits = jnp.where(lane < N_EXPERTS, logits, NEG_BIG)
    l1 = jnp.max(logits, axis=-1, keepdims=True)
    e1 = jnp.min(jnp.where(logits == l1, lane, LANES), axis=-1, keepdims=True)
    rest = jnp.where(lane == e1, NEG_BIG, logits)
    l2 = jnp.max(rest, axis=-1, keepdims=True)
    e2 = jnp.min(jnp.where(rest == l2, lane, LANES), axis=-1, keepdims=True)
    ex = jnp.exp(l2 - l1)
    w1 = 1.0 / (1.0 + ex)
    w2 = ex / (1.0 + ex)
    member = jnp.where((lane == e1) | (lane == e2), 1.0, 0.0)
    cnt = jnp.sum(member, axis=0, keepdims=True)
    run = jnp.floor((cnt + (SUBLANES - 1)) * (1.0 / SUBLANES)) * SUBLANES
    run_start = _dot_nt(jnp.broadcast_to(run, (SUBLANES, LANES)).astype(BF16),
                        tri_ref[0:LANES, 0:LANES])[0:1, :]
    before = _dot(tri_ref[...], member.astype(BF16)) + run_start
    r1 = jnp.sum(jnp.where(lane == e1, before, 0.0), axis=-1, keepdims=True)
    r2 = jnp.sum(jnp.where(lane == e2, before, 0.0), axis=-1, keepdims=True)
    info = jnp.where(lane == 0, e1.astype(F32),
                     jnp.where(lane == 1, e2.astype(F32),
                               jnp.where(lane == 2, r1, jnp.where(lane == 3, r2, 0.0))))
    info_ref[...] = info.astype(jnp.int32)
    w_ref[...] = jnp.where(lane == 0, w1, jnp.where(lane == 1, w2, 0.0))
    cnt_ref[...] = cnt


def _route(h, router_pad, tri):
    t, d = h.shape
    return pl.pallas_call(
        _route_kernel,
        grid=(t // TM,),
        in_specs=[pl.BlockSpec((TM, d), lambda i: (i, 0)),
                  pl.BlockSpec(router_pad.shape, lambda i: (0, 0)),
                  pl.BlockSpec(tri.shape, lambda i: (0, 0))],
        out_specs=[pl.BlockSpec((TM, LANES), lambda i: (i, 0)),
                   pl.BlockSpec((TM, LANES), lambda i: (i, 0)),
                   pl.BlockSpec((None, 1, LANES), lambda i: (i, 0, 0))],
        out_shape=[jax.ShapeDtypeStruct((t, LANES), jnp.int32),
                   jax.ShapeDtypeStruct((t, LANES), F32),
                   jax.ShapeDtypeStruct((t // TM, 1, LANES), F32)],
        compiler_params=_cparams(("arbitrary",), 40),
        name="moe_route",
    )(h, router_pad, tri)


def _group_copies(src, src_row, dst, dst_row, n_groups, sem, wait=False, same_src=False):
    def body(g, carry):
        src_g = 0 if (same_src or wait) else g
        dst_g = 0 if wait else g
        copy = pltpu.make_async_copy(
            src.at[pl.ds(pl.multiple_of(src_row + src_g * SUBLANES, SUBLANES), SUBLANES), :],
            dst.at[pl.ds(pl.multiple_of(dst_row + dst_g * SUBLANES, SUBLANES), SUBLANES), :], sem)
        if wait:
            copy.wait()
        else:
            copy.start()
        return carry

    lax.fori_loop(0, n_groups, body, 0)


def _dispatch_kernel(seg_ref, off_ref, ng_ref, gap_ref, pos_ref, h_ref, xb_ref,
                     sorted_ref, zero_ref, sem, zsem):
    s = pl.program_id(0)
    rows, tm = sorted_ref.shape[0], h_ref.shape[0]
    slot = lax.broadcasted_iota(jnp.int32, (rows, tm), 0)
    place = jnp.where(slot == pos_ref[0:1, :], 1.0, jnp.where(slot == pos_ref[1:2, :], 1.0, 0.0))
    sorted_ref[...] = _dot(place.astype(BF16), h_ref[...].astype(BF16))

    def runs(wait):
        for e in range(N_EXPERTS):
            k = s * N_EXPERTS + e
            _group_copies(sorted_ref, 0 if wait else off_ref[k], xb_ref, 0 if wait else seg_ref[k],
                          ng_ref[k], sem, wait)

    runs(False)

    @pl.when(s == 0)
    def _():
        zero_ref[...] = jnp.zeros_like(zero_ref)
        for wait in (False, True):
            for g in range(N_EXPERTS + 1):
                _group_copies(zero_ref, 0, xb_ref, 0 if wait else gap_ref[2 * g], gap_ref[2 * g + 1],
                              zsem, wait, same_src=True)

    runs(True)


def _dispatch(h, pos_rows, seg, off, ngroups, gaps, n_rows):
    t, d = h.shape
    rows = TM * TOP_K + N_EXPERTS * SUBLANES
    return pl.pallas_call(
        _dispatch_kernel,
        grid_spec=pltpu.PrefetchScalarGridSpec(
            num_scalar_prefetch=4,
            grid=(t // TM,),
            in_specs=[pl.BlockSpec((None, TOP_K, TM), lambda i, *_: (i, 0, 0)),
                      pl.BlockSpec((TM, d), lambda i, *_: (i, 0))],
            out_specs=pl.BlockSpec(memory_space=pl.ANY),
            scratch_shapes=[pltpu.VMEM((rows, d), F32), pltpu.VMEM((SUBLANES, d), F32),
                            pltpu.SemaphoreType.DMA(()), pltpu.SemaphoreType.DMA(())]),
        out_shape=jax.ShapeDtypeStruct((n_rows, d), F32),
        compiler_params=_cparams(("arbitrary",), 48),
        name="moe_dispatch",
    )(seg, off, ngroups, gaps, pos_rows, h)


def _expert_kernel(be_ref, bv_ref, x_ref, wg_ref, wu_ref, wd_ref, y_ref, xb_scr):
    i = pl.program_id(0)
    j = pl.program_id(1)

    @pl.when(bv_ref[i] > 0)
    def _():
        @pl.when(j == 0)
        def _():
            xb_scr[...] = x_ref[...].astype(BF16)
            y_ref[...] = jnp.zeros_like(y_ref)

        y_ref[...] += _swiglu_tile(xb_scr[...], wg_ref, wu_ref, wd_ref)

    @pl.when((bv_ref[i] == 0) & (j == 0))
    def _():
        y_ref[...] = jnp.zeros_like(y_ref)


def _experts(xb, block_e, block_v, wg, wu, wd):
    n_rows, d = xb.shape
    f = wg.shape[2]

    def jj(i, j, bv):
        return jnp.where(bv[i] > 0, j, 0)

    return pl.pallas_call(
        _expert_kernel,
        grid_spec=pltpu.PrefetchScalarGridSpec(
            num_scalar_prefetch=2,
            grid=(n_rows // MOE_BM, f // MOE_TF),
            in_specs=[pl.BlockSpec((MOE_BM, d), lambda i, j, be, bv: (i, 0)),
                      pl.BlockSpec((None, d, MOE_TF), lambda i, j, be, bv: (be[i], 0, jj(i, j, bv))),
                      pl.BlockSpec((None, d, MOE_TF), lambda i, j, be, bv: (be[i], 0, jj(i, j, bv))),
                      pl.BlockSpec((None, MOE_TF, d), lambda i, j, be, bv: (be[i], jj(i, j, bv), 0))],
            out_specs=pl.BlockSpec((MOE_BM, d), lambda i, j, be, bv: (i, 0)),
            scratch_shapes=[pltpu.VMEM((MOE_BM, d), BF16)]),
        out_shape=jax.ShapeDtypeStruct((n_rows, d), F32),
        compiler_params=_cparams(("arbitrary", "arbitrary"), 56),
        name="moe_experts",
    )(block_e, block_v, xb, wg, wu, wd)


def _make_combine_kernel(npt):
    def kern(seg_ref, off_ref, ng_ref, yb_ref, pos_ref, w_ref, x_ref, gate_ref, fg_ref,
             op_ref, os_ref, runs_ref, sems):
        i = pl.program_id(0)
        n = pl.num_programs(0)
        rows = runs_ref.shape[1]
        tm = x_ref.shape[0]

        def fetch(tile, slot, wait=False):
            for e in range(N_EXPERTS):
                k = tile * N_EXPERTS + e
                _group_copies(yb_ref, 0 if wait else seg_ref[k], runs_ref.at[slot],
                              0 if wait else off_ref[k], ng_ref[k], sems.at[slot], wait)

        @pl.when(i == 0)
        def _():
            runs_ref[...] = jnp.zeros_like(runs_ref)
            fetch(0, 0)

        @pl.when(i + 1 < n)
        def _():
            fetch(i + 1, (i + 1) % 2)

        slot = i % 2
        fetch(i, slot, wait=True)
        y_runs = runs_ref[slot].astype(BF16)
        col = lax.broadcasted_iota(jnp.int32, (tm, rows), 1)
        pos = pos_ref[...]
        picked = [_dot(jnp.where(col == pos[:, k:k + 1], 1.0, 0.0).astype(BF16), y_runs)
                  for k in range(TOP_K)]
        w = w_ref[...]
        y = w[:, 0:1] * picked[0] + w[:, 1:2] * picked[1]
        xn = x_ref[...] + gate_ref[...] * y
        ms = jnp.mean(xn * xn, axis=-1, keepdims=True)
        out = xn * lax.rsqrt(ms + NORM_EPS) * fg_ref[...]

        @pl.when(i < npt)
        def _():
            op_ref[...] = out

        @pl.when(i >= npt)
        def _():
            os_ref[...] = out
    return kern


def _combine(yb, pos, seg, off, ngroups, top_w, x, modv, layer, final_g, dims):
    t, d = x.shape
    tp = dims["prompt_tokens"]
    npt = tp // TM
    rows = TM * TOP_K + N_EXPERTS * SUBLANES
    row_of_tile = dims["row_of_tile"](TM)
    gate_base = (layer * 6 + 5) * MOD_ROWS
    return pl.pallas_call(
        _make_combine_kernel(npt),
        grid_spec=pltpu.PrefetchScalarGridSpec(
            num_scalar_prefetch=3,
            grid=(t // TM,),
            in_specs=[pl.BlockSpec(memory_space=pl.ANY),
                      pl.BlockSpec((TM, TOP_K), lambda i, *_: (i, 0)),
                      pl.BlockSpec((TM, LANES), lambda i, *_: (i, 0)),
                      pl.BlockSpec((TM, d), lambda i, *_: (i, 0)),
                      pl.BlockSpec((None, 1, d), lambda i, *_: (gate_base + row_of_tile(i), 0, 0)),
                      pl.BlockSpec((1, d), lambda i, *_: (0, 0))],
            out_specs=[pl.BlockSpec((TM, d), lambda i, *_: (jnp.minimum(i, npt - 1), 0)),
                       pl.BlockSpec((TM, d), lambda i, *_: (jnp.maximum(i - npt, 0), 0))],
            scratch_shapes=[pltpu.VMEM((2, rows, d), F32), pltpu.SemaphoreType.DMA((2,))]),
        out_shape=[jax.ShapeDtypeStruct((tp, d), F32), jax.ShapeDtypeStruct((t - tp, d), F32)],
        compiler_params=_cparams(("arbitrary",), 56),
        name="moe_combine",
    )(seg, off, ngroups, yb, pos, top_w, x, modv, final_g)


def _moe_layer(h, x, router, wg, wu, wd, modv, layer, final_g, dims):
    t, d = x.shape
    n_tiles = t // TM
    router_pad = jnp.zeros((d, LANES), F32).at[:, :N_EXPERTS].set(router)
    info, top_w, counts = _route(h, router_pad, dims["tri_strict"])
    run = (counts[:, 0, :N_EXPERTS].astype(jnp.int32) + SUBLANES - 1) // SUBLANES * SUBLANES
    off = jnp.cumsum(run, axis=1) - run
    before = jnp.cumsum(run, axis=0) - run
    total = jnp.sum(run, axis=0)
    padded = (total + MOE_BM - 1) // MOE_BM * MOE_BM
    pad_end = jnp.cumsum(padded)
    pad_start = pad_end - padded
    seg = pad_start[None, :] + before
    n_rows = (t * TOP_K + n_tiles * N_EXPERTS * (SUBLANES - 1) + MOE_BM - 1) // MOE_BM * MOE_BM \
        + N_EXPERTS * MOE_BM
    pos = info[:, TOP_K:2 * TOP_K]
    pos_rows = pos.reshape(n_tiles, TM, TOP_K).transpose(0, 2, 1)
    flat = lambda a: a.reshape(-1).astype(jnp.int32)
    gap_lo = jnp.concatenate([pad_start + total, pad_end[-1:]])
    gap_hi = jnp.concatenate([pad_end, jnp.array([n_rows], jnp.int32)])
    gaps = jnp.stack([gap_lo, (gap_hi - gap_lo) // SUBLANES], axis=1)
    block_start = jnp.arange(n_rows // MOE_BM, dtype=jnp.int32) * MOE_BM
    block_e = jnp.sum(pad_end[None, :] <= block_start[:, None], axis=1).astype(jnp.int32)
    block_v = (block_e < N_EXPERTS).astype(jnp.int32)
    last_e = jnp.max(jnp.where(padded > 0, jnp.arange(N_EXPERTS), 0)).astype(jnp.int32)
    block_e = jnp.where(block_v > 0, block_e, last_e)
    xb = _dispatch(h, pos_rows, flat(seg), flat(off), flat(run // SUBLANES), flat(gaps), n_rows)
    yb = _experts(xb, block_e, block_v, wg, wu, wd)
    return _combine(yb, pos, flat(seg), flat(off), flat(run // SUBLANES), top_w, x, modv, layer,
                    final_g, dims)


def _rope_tables(dec_seq):
    n_rows = dec_seq // GRID_W
    row = jnp.repeat(jnp.arange(n_rows, dtype=F32), GRID_W)
    col = (jnp.arange(dec_seq) % GRID_W).astype(F32)
    nf = HEAD_DIM // 4
    inv = ROPE_THETA ** (-jnp.arange(nf, dtype=F32) / nf)
    ang = jnp.stack([row[:, None] * inv, col[:, None] * inv], axis=1)
    cos = jnp.cos(ang)
    sin = jnp.sin(ang)
    cos_h = jnp.concatenate([cos, cos], axis=-1).reshape(dec_seq, HEAD_DIM)
    sin_h = jnp.concatenate([-sin, sin], axis=-1).reshape(dec_seq, HEAD_DIM)
    cos_t = jnp.concatenate([cos_h, cos_h], axis=-1)
    sin_t = jnp.concatenate([sin_h, sin_h], axis=-1)
    cos_t = jnp.concatenate([cos_t, jnp.ones((TM, LANES), F32)], axis=0)
    sin_t = jnp.concatenate([sin_t, jnp.zeros((TM, LANES), F32)], axis=0)
    return cos_t, sin_t


def _expansion(direction, width):
    src = direction * SSM_HEADS + np.arange(SSM_HEADS * width) // width
    return jnp.asarray((np.arange(LANES)[:, None] == src[None, :]).astype(np.float32)).astype(BF16)


def _layer_params(l, norm1_g, norm2_g, w_in, q_norm_g, k_norm_g, conv_w, conv_b, a_log, dt_bias,
                  d_skip, ssm_norm_g, sink, w_out):
    d = w_in.shape[1]
    pad = lambda v: jnp.zeros((1, LANES), F32).at[0, :v.size].set(v.reshape(-1))
    seg = np.arange(LANES) // HEAD_DIM
    bd = jnp.asarray((seg[:, None] == seg[None, :]).astype(np.float32) / HEAD_DIM).astype(BF16)
    r = np.arange(SSM_CHUNK)
    return {
        "norm1_g": norm1_g[l].reshape(1, d), "norm2_g": norm2_g[l].reshape(1, d),
        "w_main": w_in[l, :, :OFF_DT].astype(BF16),
        "w_dt": jnp.zeros((d, LANES), F32).at[:, :2 * SSM_HEADS].set(w_in[l, :, OFF_DT:]).astype(BF16),
        "bd": bd,
        "q_norm_g": jnp.tile(q_norm_g[l], 2).reshape(1, LANES),
        "k_norm_g": jnp.tile(k_norm_g[l], 2).reshape(1, LANES),
        "conv_w": jnp.zeros((SUBLANES, XBC_DIM), F32).at[:D_CONV].set(conv_w[l]),
        "conv_b": conv_b[l].reshape(1, XBC_DIM),
        "a_log": pad(a_log[l]), "dt_bias": pad(dt_bias[l]),
        "d_skip": jnp.repeat(d_skip[l], SSM_HEAD_DIM).reshape(1, D_SSM),
        "ssm_norm_g": ssm_norm_g[l].reshape(1, D_SSM),
        "sink": sink[l],
        "w_out": w_out[l].astype(BF16),
        "tri_fwd": jnp.asarray((r[:, None] >= r[None, :]).astype(np.float32)).astype(BF16),
        "tri_bwd": jnp.asarray((r[:, None] <= r[None, :]).astype(np.float32)).astype(BF16),
        "wide_fwd": _expansion(0, LANES), "wide_bwd": _expansion(1, LANES),
        "pair_fwd": _expansion(0, SSM_HEAD_DIM), "pair_bwd": _expansion(1, SSM_HEAD_DIM),
    }


def kernel(x_prompt, x_sample, cache_attn_k, cache_attn_v, cache_win_k, cache_win_v, state_ssm_fwd, state_ssm_bwd, c, c_ctx, norm1_g, norm2_g, w_mod, b_mod, w_in, q_norm_g, k_norm_g, conv_w, conv_b, a_log, dt_bias, d_skip, ssm_norm_g, sink, w_out, ffn_w_gate, ffn_w_up, ffn_w_down, moe_router, moe_w_gate, moe_w_up, moe_w_down, final_g):
    batch, seq, d = x_prompt.shape
    dec_batch, dec_seq, _ = x_sample.shape
    depth = w_in.shape[0]
    past = cache_attn_k.shape[2]
    tp = batch * seq
    t = tp + dec_batch * dec_seq
    assert depth == 2 and tp % (2 * TM) == 0 and dec_seq % (2 * TM) == 0
    assert 1 + dec_batch <= MOD_ROWS and seq % SSM_CHUNK == 0 and tp % dec_seq == 0

    def row_of_tile(tm):
        npt, tpb = tp // tm, dec_seq // tm
        return lambda i: jnp.where(i < npt, 0, 1 + (i - npt) // tpb)

    cos_t, sin_t = _rope_tables(dec_seq)
    rr = np.arange(TM)
    dims = {"tokens": t, "prompt_tokens": tp, "dec_seq": dec_seq, "row_of_tile": row_of_tile,
            "rope_cos": cos_t, "rope_sin": sin_t,
            "tri_strict": jnp.asarray((rr[:, None] > rr[None, :]).astype(np.float32)).astype(BF16)}

    cc = jnp.zeros((MOD_ROWS, d), F32).at[0].set(c_ctx).at[1:1 + dec_batch].set(c)
    mod = _mod_vectors(cc, w_mod, b_mod)
    modv = mod.reshape(depth, MOD_ROWS, 6, d).transpose(0, 2, 1, 3).reshape(depth * 6 * MOD_ROWS, 1, d)

    x = (x_prompt.reshape(tp, d), x_sample.reshape(dec_batch * dec_seq, d))
    zero_state = jnp.zeros((batch, SSM_HEADS // 2, LANES, SSM_STATE), F32)
    caches = [[] for _ in range(6)]
    y = None
    for l in range(depth):
        lay = _layer_params(l, norm1_g, norm2_g, w_in, q_norm_g, k_norm_g, conv_w, conv_b, a_log,
                            dt_bias, d_skip, ssm_norm_g, sink, w_out)
        (qa, ka, va, qc, kc, vc, z, xbc, dt, cka, cva, ckc, cvc) = _inproj(x, lay, modv, l, dims)
        ctx = lambda a: a[:, l].reshape(dec_batch, past, LANES).astype(BF16)

        oa = _attention(qa, ka, va, None, None, None, None, nb=batch, seq=seq, tq=seq, row_off=0,
                        window=False, name="attn_a_ctx")
        oa = _attention(qa, ka, va, ctx(cache_attn_k), ctx(cache_attn_v), None, oa, nb=dec_batch,
                        seq=dec_seq, tq=TQ_GLOBAL, row_off=tp, window=False, name="attn_a_lat")
        oc = _attention(qc, kc, vc, None, None, lay["sink"], None, nb=batch, seq=seq, tq=seq,
                        row_off=0, window=False, name="attn_c_ctx")
        oc = _attention(qc, kc, vc, ctx(cache_win_k), ctx(cache_win_v), lay["sink"], oc,
                        nb=dec_batch, seq=dec_seq, tq=TQ_WINDOW, row_off=tp, window=True,
                        name="attn_c_lat")
        pair = lambda s: s[:, l].reshape(dec_batch, SSM_HEADS // 2, LANES, SSM_STATE)
        ob, hf_p, hb_p = _ssd(xbc, dt, z, lay, zero_state, zero_state, None, nb=batch, seq=seq,
                              row_off=0)
        ob, _, _ = _ssd(xbc, dt, z, lay, pair(state_ssm_fwd), pair(state_ssm_bwd), ob, nb=dec_batch,
                        seq=dec_seq, row_off=tp)

        kv = lambda a: a[:tp].reshape(batch, seq, ATT_KV_HEADS, HEAD_DIM)
        st = lambda s: s.reshape(batch, SSM_HEADS, SSM_HEAD_DIM, SSM_STATE)
        for lst, val in zip(caches, (kv(cka), kv(cva), kv(ckc), kv(cvc), st(hf_p), st(hb_p))):
            lst.append(val)

        i = l // 2
        if l % 2 == 0:
            xr, h2 = _outproj(oa, ob, oc, x, lay, modv, l, dims, BF16)
            x = (_dense_ffn(h2, xr, ffn_w_gate[i].astype(BF16), ffn_w_up[i].astype(BF16),
                            ffn_w_down[i].astype(BF16), modv, l, dims),)
        else:
            xr, h2 = _outproj(oa, ob, oc, x, lay, modv, l, dims, F32)
            y = _moe_layer(h2, xr, moe_router[i], moe_w_gate[i], moe_w_up[i], moe_w_down[i], modv, l,
                           final_g.reshape(1, d), dims)

    y_prompt = y[0].reshape(batch, seq, d)
    y_sample = y[1].reshape(dec_batch, dec_seq, d)
    return (y_prompt, y_sample) + tuple(jnp.stack(lst, axis=1) for lst in caches)
```

```python
import numpy as np
import jax
import jax.numpy as jnp
from jax import lax
from jax.experimental import pallas as pl
from jax.experimental.pallas import tpu as pltpu

F32 = jnp.float32
BF16 = jnp.bfloat16
HIGHEST = lax.Precision.HIGHEST

GRID_W = 64
HEAD_DIM = 64
ATT_HEADS = 4
ATT_KV_HEADS = 2
WIN_HEADS = 4
WIN_KV_HEADS = 2
WINDOW = 128
ROPE_THETA = 10000.0
SSM_HEADS = 8
SSM_HEAD_DIM = 64
D_SSM = SSM_HEADS * SSM_HEAD_DIM
SSM_GROUPS = 2
SSM_STATE = 128
D_CONV = 5
SSM_CHUNK = 128
XBC_DIM = D_SSM + 2 * SSM_GROUPS * SSM_STATE
N_EXPERTS = 8
TOP_K = 2
NORM_EPS = 1e-6
NEG_BIG = -1e30
LOG2E = 1.4426950408889634

LANES = 128
SUBLANES = 8
MOD_ROWS = 16

OFF_QA, OFF_KA, OFF_VA, OFF_QC, OFF_KC, OFF_VC, OFF_Z, OFF_XBC, OFF_DT = (
    0, 256, 384, 512, 768, 896, 1024, 1536, 2560)

TM = 512
TQ_GLOBAL = 512
TQ_WINDOW = 256
ATTN_KEY_CHUNK = 256
ATTN_KEY_CHUNK_WINDOW = 512
MOE_BM = 1024
MOE_TF = 512
SSD_STEP = 1024
FFN_SLICE = 256


def _cparams(sem, vmem_mb):
    return pltpu.CompilerParams(dimension_semantics=sem, vmem_limit_bytes=vmem_mb * 1024 * 1024)


def _dot(a, b, precision=None):
    return jnp.dot(a, b, preferred_element_type=F32, precision=precision)


def _dot_nt(a, b):
    return lax.dot_general(a, b, (((1,), (1,)), ((), ())), preferred_element_type=F32)


def _silu(x):
    return x / (1.0 + jnp.exp(-x))


def _norm_mod(x, g, shift, scale):
    ms = jnp.mean(x * x, axis=-1, keepdims=True)
    y = x * lax.rsqrt(ms + NORM_EPS) * g
    return y * (1.0 + scale) + shift


def _pick_x(i, npt, xs):
    if len(xs) == 1:
        return xs[0][...]
    return jnp.where(i < npt, xs[0][...], xs[1][...])


def _x_specs(x, tm, npt):
    d = x[0].shape[1]
    if len(x) == 1:
        return [pl.BlockSpec((tm, d), lambda i, *_: (i, 0))]
    return [pl.BlockSpec((tm, d), lambda i, *_: (jnp.minimum(i, npt - 1), 0)),
            pl.BlockSpec((tm, d), lambda i, *_: (jnp.maximum(i - npt, 0), 0))]


def _mod_kernel(c_ref, w_ref, b_ref, o_ref):
    a = _silu(c_ref[...])
    o_ref[...] = _dot(a, w_ref[...], HIGHEST) + b_ref[...]


def _mod_vectors(cc, w_mod, b_mod):
    depth, d, n = w_mod.shape
    tn = 1536
    return pl.pallas_call(
        _mod_kernel,
        grid=(depth, n // tn),
        in_specs=[pl.BlockSpec((MOD_ROWS, d), lambda l, j: (0, 0)),
                  pl.BlockSpec((None, d, tn), lambda l, j: (l, 0, j)),
                  pl.BlockSpec((None, 1, tn), lambda l, j: (l, 0, j))],
        out_specs=pl.BlockSpec((None, MOD_ROWS, tn), lambda l, j: (l, 0, j)),
        out_shape=jax.ShapeDtypeStruct((depth, MOD_ROWS, n), F32),
        compiler_params=_cparams(("arbitrary", "arbitrary"), 40),
        name="mod_vectors",
    )(cc, w_mod, b_mod.reshape(depth, 1, n))


def _mod_spec(layer, k, d, row_of_tile):
    base = (layer * 6 + k) * MOD_ROWS
    return pl.BlockSpec((None, 1, d), lambda i, *_: (base + row_of_tile(i), 0, 0))


def _make_inproj_kernel(n_x, npt):
    def kern(*refs):
        xs = refs[:n_x]
        (g_ref, sh_ref, sc_ref, w_ref, wdt_ref, bd_ref, qg_ref, kg_ref, cos_ref, sin_ref,
         qa_ref, ka_ref, va_ref, qc_ref, kc_ref, vc_ref, z_ref, xbc_ref, dt_ref,
         cka_ref, cva_ref, ckc_ref, cvc_ref) = refs[n_x:]
        x = _pick_x(pl.program_id(0), npt, xs)
        h = _norm_mod(x, g_ref[...], sh_ref[...], sc_ref[...]).astype(BF16)

        def proj(a, b):
            return _dot(h, w_ref[:, a:b])

        cos = cos_ref[...]
        sin = sin_ref[...]
        tm = cos.shape[0]
        lane = lax.broadcasted_iota(jnp.int32, (tm, LANES), 1)
        first = (lane % 32) < 16
        low = lane < HEAD_DIM
        bd = bd_ref[...]

        def rope(v):
            partner = jnp.where(first, pltpu.roll(v, LANES - 16, 1), pltpu.roll(v, 16, 1))
            return v * cos + partner * sin

        def head_norm(v, g):
            sq = v * v
            hi = sq.astype(BF16)
            lo = (sq - hi.astype(F32)).astype(BF16)
            ms = _dot(hi, bd) + _dot(lo, bd)
            return v * lax.rsqrt(ms + NORM_EPS) * g

        def stack_heads(c0, c1, out_ref):
            zero = jnp.zeros_like(c0)
            out_ref[0] = jnp.where(low, c0, zero).astype(out_ref.dtype)
            out_ref[1] = jnp.where(low, pltpu.roll(c0, HEAD_DIM, 1), zero).astype(out_ref.dtype)
            out_ref[2] = jnp.where(low, zero, pltpu.roll(c1, HEAD_DIM, 1)).astype(out_ref.dtype)
            out_ref[3] = jnp.where(low, zero, c1).astype(out_ref.dtype)

        scale = LOG2E * HEAD_DIM ** -0.5
        qg = qg_ref[...]
        q_a = proj(OFF_QA, OFF_KA)
        qa0 = rope(head_norm(q_a[:, 0:LANES], qg)) * scale
        qa1 = rope(head_norm(q_a[:, LANES:2 * LANES], qg)) * scale
        stack_heads(qa0, qa1, qa_ref)
        kv_a = proj(OFF_KA, OFF_QC)
        ka = rope(head_norm(kv_a[:, 0:LANES], kg_ref[...]))
        ka_ref[...] = ka.astype(ka_ref.dtype)
        cka_ref[...] = ka
        va = kv_a[:, LANES:2 * LANES]
        va_ref[...] = va.astype(va_ref.dtype)
        cva_ref[...] = va

        q_c = proj(OFF_QC, OFF_KC)
        qc0 = rope(q_c[:, 0:LANES]) * scale
        qc1 = rope(q_c[:, LANES:2 * LANES]) * scale
        stack_heads(qc0, qc1, qc_ref)
        kv_c = proj(OFF_KC, OFF_Z)
        kc = rope(kv_c[:, 0:LANES])
        kc_ref[...] = kc.astype(kc_ref.dtype)
        ckc_ref[...] = kc
        vc = kv_c[:, LANES:2 * LANES]
        vc_ref[...] = vc.astype(vc_ref.dtype)
        cvc_ref[...] = vc

        z_ref[...] = proj(OFF_Z, OFF_XBC)
        xbc_ref[...] = proj(OFF_XBC, OFF_DT)
        dt_ref[...] = _dot(h, wdt_ref[...])
    return kern


def _inproj(x, lay, modv, layer, dims):
    t, d = dims["tokens"], x[0].shape[1]
    n_tiles = t // TM
    npt = dims["prompt_tokens"] // TM
    tpb = dims["dec_seq"] // TM
    row_of_tile = dims["row_of_tile"](TM)

    def rope_idx(i):
        return jnp.where(i < npt, tpb, (i - npt) % tpb)

    def cache_idx(i):
        return jnp.minimum(i, npt)

    tok = lambda w: pl.BlockSpec((TM, w), lambda i: (i, 0))
    full = lambda a: pl.BlockSpec(a.shape, lambda i: (0,) * a.ndim)
    stack = pl.BlockSpec((4, TM, LANES), lambda i: (0, i, 0))
    cache = pl.BlockSpec((TM, LANES), lambda i: (cache_idx(i), 0))
    rope = pl.BlockSpec((TM, LANES), lambda i: (rope_idx(i), 0))
    cache_shape = jax.ShapeDtypeStruct(((npt + 1) * TM, LANES), F32)
    return pl.pallas_call(
        _make_inproj_kernel(len(x), npt),
        grid=(n_tiles,),
        in_specs=_x_specs(x, TM, npt) + [
            full(lay["norm1_g"]),
            _mod_spec(layer, 0, d, row_of_tile), _mod_spec(layer, 1, d, row_of_tile),
            full(lay["w_main"]), full(lay["w_dt"]), full(lay["bd"]),
            full(lay["q_norm_g"]), full(lay["k_norm_g"]), rope, rope],
        out_specs=[stack, tok(LANES), tok(LANES), stack, tok(LANES), tok(LANES),
                   tok(D_SSM), tok(XBC_DIM), tok(LANES), cache, cache, cache, cache],
        out_shape=[jax.ShapeDtypeStruct((4, t, LANES), BF16),
                   jax.ShapeDtypeStruct((t, LANES), BF16), jax.ShapeDtypeStruct((t, LANES), BF16),
                   jax.ShapeDtypeStruct((4, t, LANES), BF16),
                   jax.ShapeDtypeStruct((t, LANES), BF16), jax.ShapeDtypeStruct((t, LANES), BF16),
                   jax.ShapeDtypeStruct((t, D_SSM), F32), jax.ShapeDtypeStruct((t, XBC_DIM), F32),
                   jax.ShapeDtypeStruct((t, LANES), F32),
                   cache_shape, cache_shape, cache_shape, cache_shape],
        compiler_params=_cparams(("arbitrary",), 48),
        name="inproj",
    )(*x, lay["norm1_g"], modv, modv, lay["w_main"], lay["w_dt"], lay["bd"],
      lay["q_norm_g"], lay["k_norm_g"], dims["rope_cos"], dims["rope_sin"])


def _make_attn_kernel(tq, n_ctx, n_lat, window_len, has_sink):
    ck = ATTN_KEY_CHUNK_WINDOW if window_len else ATTN_KEY_CHUNK

    def kern(*refs):
        refs = list(refs)
        sink_ref = refs.pop(0) if has_sink else None
        q_ref = refs.pop(0)
        kc_ref = refs.pop(0) if n_ctx else None
        vc_ref = refs.pop(0) if n_ctx else None
        kl_ref, vl_ref = refs.pop(0), refs.pop(0)
        refs.pop(0)
        o_ref, vaug_ref = refs

        @pl.when(pl.program_id(1) == 0)
        def _():
            if n_ctx:
                vaug_ref[0:n_ctx, 0:LANES] = vc_ref[...]
            vaug_ref[n_ctx:n_ctx + n_lat, 0:LANES] = vl_ref[...]
            vaug_ref[:, LANES:2 * LANES] = jnp.ones((n_ctx + n_lat, LANES), BF16)

        rows = 4 * tq
        q = q_ref[...].reshape(rows, LANES)
        if has_sink:
            head = lax.broadcasted_iota(jnp.int32, (rows, 1), 0) // tq
            snk = LOG2E * jnp.where(head == 0, sink_ref[0],
                                    jnp.where(head == 1, sink_ref[1],
                                              jnp.where(head == 2, sink_ref[2], sink_ref[3])))
            m = snk
        else:
            m = jnp.full((rows, 1), NEG_BIG, F32)
        acc = jnp.zeros((rows, 2 * LANES), F32)

        chunks = []
        for c0 in range(0, n_ctx, ck):
            n = min(ck, n_ctx - c0)
            chunks.append((kc_ref[c0:c0 + n, :], vaug_ref[c0:c0 + n, :], None))
        if window_len:
            q0 = pl.program_id(1) * tq
            ws = pl.multiple_of(jnp.clip(q0 - WINDOW, 0, n_lat - window_len), LANES)
            qpos = q0 + lax.broadcasted_iota(jnp.int32, (rows, window_len), 0) % tq
            kpos = ws + lax.broadcasted_iota(jnp.int32, (rows, window_len), 1)
            valid = jnp.abs(qpos - kpos) <= WINDOW
            chunks.append((kl_ref[pl.ds(ws, window_len), :],
                           vaug_ref[pl.ds(n_ctx + ws, window_len), :], valid))
        else:
            for c0 in range(0, n_lat, ck):
                n = min(ck, n_lat - c0)
                chunks.append((kl_ref[c0:c0 + n, :], vaug_ref[n_ctx + c0:n_ctx + c0 + n, :], None))

        for keys, vaug, valid in chunks:
            s = _dot_nt(q, keys)
            if valid is not None:
                s = jnp.where(valid, s, NEG_BIG)
            m_new = jnp.maximum(m, jnp.max(s, axis=-1, keepdims=True))
            p = jnp.exp2(s - m_new).astype(BF16)
            acc = acc * jnp.exp2(m - m_new) + _dot(p, vaug)
            m = m_new

        den = acc[:, LANES:2 * LANES]
        if has_sink:
            den = den + jnp.exp2(snk - m)
        o = acc[:, 0:LANES] / den
        low = lax.broadcasted_iota(jnp.int32, (tq, LANES), 1) < HEAD_DIM
        o0, o1, o2, o3 = (o[h * tq:(h + 1) * tq] for h in range(4))
        c0 = jnp.where(low, o0, pltpu.roll(o1, HEAD_DIM, 1))
        c1 = jnp.where(low, pltpu.roll(o2, HEAD_DIM, 1), o3)
        o_ref[:, 0:LANES] = c0.astype(o_ref.dtype)
        o_ref[:, LANES:2 * LANES] = c1.astype(o_ref.dtype)
    return kern


def _attention(q_stack, k_lat, v_lat, k_ctx, v_ctx, sink, prev_out, *, nb, seq, tq, row_off,
               window, name):
    t = q_stack.shape[1]
    nq = seq // tq
    qoff = row_off // tq
    boff = row_off // seq
    n_ctx = 0 if k_ctx is None else k_ctx.shape[1]
    window_len = min(seq, tq + 2 * WINDOW) if window else 0
    kern = _make_attn_kernel(tq, n_ctx, seq, window_len, sink is not None)
    in_specs, args = [], []
    if sink is not None:
        in_specs.append(pl.BlockSpec(memory_space=pltpu.SMEM))
        args.append(sink)
    in_specs.append(pl.BlockSpec((4, tq, LANES), lambda b, j: (0, qoff + b * nq + j, 0)))
    args.append(q_stack)
    if n_ctx:
        ctx_spec = pl.BlockSpec((None, n_ctx, LANES), lambda b, j: (b, 0, 0))
        in_specs += [ctx_spec, ctx_spec]
        args += [k_ctx, v_ctx]
    lat_spec = pl.BlockSpec((seq, LANES), lambda b, j: (boff + b, 0))
    in_specs += [lat_spec, lat_spec]
    args += [k_lat, v_lat]
    if prev_out is None:
        prev_out = jnp.zeros((t, 2 * LANES), BF16)
    in_specs.append(pl.BlockSpec(memory_space=pl.ANY))
    args.append(prev_out)
    aliases = {len(args) - 1: 0}
    return pl.pallas_call(
        kern,
        grid=(nb, nq),
        in_specs=in_specs,
        out_specs=pl.BlockSpec((tq, 2 * LANES), lambda b, j: (qoff + b * nq + j, 0)),
        out_shape=jax.ShapeDtypeStruct((t, 2 * LANES), BF16),
        scratch_shapes=[pltpu.VMEM((n_ctx + seq, 2 * LANES), BF16)],
        input_output_aliases=aliases,
        compiler_params=_cparams(("arbitrary", "arbitrary"), 48),
        name=name,
    )(*args)


def _split_bf16(v, n):
    parts = []
    for _ in range(n):
        part = v.astype(BF16)
        parts.append(part)
        v = v - part.astype(F32)
    return parts


def _dot_parts(a, parts):
    out = _dot(a, parts[0])
    for part in parts[1:]:
        out = out + _dot(a, part)
    return out


def _parts_dot(parts, b):
    out = _dot(parts[0], b)
    for part in parts[1:]:
        out = out + _dot(part, b)
    return out


def _ssd_chunk(direction, xc, dt_raw, dtb_ref, alog_ref, tri_ref, ewide_ref, epair_ref, h_scr):
    q = SSM_CHUNK
    xs = xc[:, 0:D_SSM]
    bm = xc[:, D_SSM:D_SSM + SSM_GROUPS * SSM_STATE]
    cm = xc[:, D_SSM + SSM_GROUPS * SSM_STATE:]

    dtx = dt_raw + dtb_ref[...]
    dt = jnp.maximum(dtx, 0.0) + jnp.log1p(jnp.exp(-jnp.abs(dtx)))
    dta = dt * (-LOG2E * jnp.exp(alog_ref[...]))
    a_col = _dot_parts(tri_ref[...], _split_bf16(dta, 3))
    a_row = a_col.T
    a_wide = _parts_dot(_split_bf16(a_col, 3), ewide_ref[...])
    dt_lanes = _parts_dot(_split_bf16(dt, 2), epair_ref[...])
    row = lax.broadcasted_iota(jnp.int32, (q, q), 0)
    col = lax.broadcasted_iota(jnp.int32, (q, q), 1)
    live = (row >= col) if direction == 0 else (col >= row)
    low_lane = col < SSM_HEAD_DIM
    low_row = row < SSM_HEAD_DIM
    edge = q - 1 if direction == 0 else 0

    group = lambda v, g: v[:, g * SSM_STATE:(g + 1) * SSM_STATE]
    cm_b = [group(cm, g).astype(BF16) for g in range(SSM_GROUPS)]
    bm_b = [group(bm, g).astype(BF16) for g in range(SSM_GROUPS)]
    cbs = [_dot_nt(cm_b[g], bm_b[g]) for g in range(SSM_GROUPS)]

    rep = SSM_HEADS // SSM_GROUPS
    y_pairs = []
    for p in range(SSM_HEADS // 2):
        g = (2 * p) // rep
        hl = [direction * SSM_HEADS + 2 * p, direction * SSM_HEADS + 2 * p + 1]
        a_head = [a_wide[:, (2 * p + k) * LANES:(2 * p + k + 1) * LANES] for k in range(2)]
        a_pair = jnp.where(low_lane, a_head[0], a_head[1])
        a_edge = a_pair[edge:edge + 1, :]
        xdt = xs[:, p * LANES:(p + 1) * LANES] * dt_lanes[:, p * LANES:(p + 1) * LANES]
        hs = h_scr[p]
        m_both = jnp.concatenate(
            [(cbs[g] * jnp.exp2(jnp.where(live, a_head[k] - a_row[hl[k]:hl[k] + 1, :], NEG_BIG))
              ).astype(BF16) for k in range(2)], axis=1)
        x_both = jnp.concatenate([jnp.where(low_lane, xdt, 0.0).astype(BF16),
                                  jnp.where(low_lane, 0.0, xdt).astype(BF16)], axis=0)
        yd = _dot(m_both, x_both)
        yo = _dot_nt(cm_b[g], hs.astype(BF16)) * jnp.exp2(a_pair)
        y_pairs.append(yd + yo)
        st = _dot((xdt * jnp.exp2(a_edge - a_pair)).T.astype(BF16), bm_b[g])
        carry = [jnp.exp2(a_col[edge:edge + 1, l:l + 1]) for l in hl]
        h_scr[p] = hs * jnp.where(low_row, carry[0], carry[1]) + st
    return jnp.concatenate(y_pairs, axis=1), xs


def _make_ssd_kernel(ns, step):
    q = SSM_CHUNK
    halo = SUBLANES
    n_sub = step // q

    def kern(xp_ref, xm_ref, xn_ref, dt_ref, z_ref, cw_ref, cb_ref, dtb_ref, alog_ref,
             trif_ref, trib_ref, ewf_ref, ewb_ref, epf_ref, epb_ref,
             h0f_ref, h0b_ref, dsk_ref, ng_ref, _alias_ref,
             y_ref, hf_ref, hb_ref, ext_scr, xc_scr, yf_scr, h_scr):
        s = pl.program_id(1)

        @pl.when(s == 0)
        def _():
            h_scr[...] = h0f_ref[...]

        @pl.when(s == ns)
        def _():
            h_scr[...] = h0b_ref[...]

        @pl.when(s < ns)
        def _():
            ext_scr[0:halo, :] = jnp.where(s > 0, xp_ref[...], 0.0)
            ext_scr[halo:halo + step, :] = xm_ref[...]
            ext_scr[halo + step:2 * halo + step, :] = jnp.where(s < ns - 1, xn_ref[...], 0.0)
            ext = ext_scr[...]
            acc = cb_ref[...] + cw_ref[D_CONV // 2:D_CONV // 2 + 1, :] * ext[halo:halo + step]
            for k in range(D_CONV):
                if k != D_CONV // 2:
                    shifted = pltpu.roll(ext, (D_CONV // 2 - k) % (step + 2 * halo), 0)
                    acc = acc + cw_ref[k:k + 1, :] * shifted[halo:halo + step]
            xc = _silu(acc)
            r0 = pl.multiple_of(s * step, step)
            xc_scr[pl.ds(r0, step), :] = xc
            for c in range(n_sub):
                rows = slice(c * q, (c + 1) * q)
                y, xs = _ssd_chunk(0, xc[rows], dt_ref[rows, :], dtb_ref, alog_ref, trif_ref,
                                   ewf_ref, epf_ref, h_scr)
                yf_scr[pl.ds(r0 + c * q, q), :] = y + dsk_ref[...] * xs

            @pl.when(s == ns - 1)
            def _():
                hf_ref[...] = h_scr[...]

        @pl.when(s >= ns)
        def _():
            r0 = pl.multiple_of((2 * ns - 1 - s) * step, step)
            gw = D_SSM // SSM_GROUPS
            for c in reversed(range(n_sub)):
                rows = slice(c * q, (c + 1) * q)
                y, _ = _ssd_chunk(1, xc_scr[pl.ds(r0 + c * q, q), :], dt_ref[rows, :], dtb_ref,
                                  alog_ref, trib_ref, ewb_ref, epb_ref, h_scr)
                gated = (yf_scr[pl.ds(r0 + c * q, q), :] + y) * _silu(z_ref[rows, :])
                for g in range(SSM_GROUPS):
                    part = gated[:, g * gw:(g + 1) * gw]
                    ms = jnp.mean(part * part, axis=-1, keepdims=True)
                    y_ref[rows, g * gw:(g + 1) * gw] = (
                        part * lax.rsqrt(ms + NORM_EPS) * ng_ref[:, g * gw:(g + 1) * gw]).astype(y_ref.dtype)

            @pl.when(s == 2 * ns - 1)
            def _():
                hb_ref[...] = h_scr[...]
    return kern


def _ssd(xbc, dt, z, lay, h0f, h0b, prev_out, *, nb, seq, row_off):
    t = xbc.shape[0]
    step = min(seq, SSD_STEP)
    ns = seq // step
    soff = row_off // step
    last_halo = t // SUBLANES - 1
    per = step // SUBLANES

    fwd_blk = lambda b, s: soff + b * ns + jnp.minimum(s, ns - 1)
    any_blk = lambda b, s: soff + b * ns + jnp.where(s < ns, s, 2 * ns - 1 - s)
    bwd_blk = lambda b, s: soff + b * ns + jnp.where(s < ns, ns - 1, 2 * ns - 1 - s)

    full = lambda a: pl.BlockSpec(a.shape, lambda b, s: (0,) * a.ndim)
    prev = pl.BlockSpec((SUBLANES, XBC_DIM), lambda b, s: (jnp.maximum(fwd_blk(b, s) * per - 1, 0), 0))
    nxt = pl.BlockSpec((SUBLANES, XBC_DIM),
                       lambda b, s: (jnp.minimum((fwd_blk(b, s) + 1) * per, last_halo), 0))
    state = pl.BlockSpec((None, SSM_HEADS // 2, LANES, SSM_STATE), lambda b, s: (b, 0, 0, 0))
    out_spec = pl.BlockSpec((step, D_SSM), lambda b, s: (bwd_blk(b, s), 0))
    consts = [lay["conv_w"], lay["conv_b"], lay["dt_bias"], lay["a_log"], lay["tri_fwd"], lay["tri_bwd"],
              lay["wide_fwd"], lay["wide_bwd"], lay["pair_fwd"], lay["pair_bwd"]]
    if prev_out is None:
        prev_out = jnp.zeros((t, D_SSM), BF16)
    state_shape = jax.ShapeDtypeStruct((nb, SSM_HEADS // 2, LANES, SSM_STATE), F32)
    return pl.pallas_call(
        _make_ssd_kernel(ns, step),
        grid=(nb, 2 * ns),
        in_specs=[prev, pl.BlockSpec((step, XBC_DIM), lambda b, s: (fwd_blk(b, s), 0)), nxt,
                  pl.BlockSpec((step, LANES), lambda b, s: (any_blk(b, s), 0)), out_spec]
                 + [full(a) for a in consts]
                 + [state, state, full(lay["d_skip"]), full(lay["ssm_norm_g"]),
                    pl.BlockSpec(memory_space=pl.ANY)],
        out_specs=[out_spec, state, state],
        out_shape=[jax.ShapeDtypeStruct((t, D_SSM), BF16), state_shape, state_shape],
        scratch_shapes=[pltpu.VMEM((step + 2 * SUBLANES, XBC_DIM), F32),
                        pltpu.VMEM((seq, XBC_DIM), F32),
                        pltpu.VMEM((seq, D_SSM), F32),
                        pltpu.VMEM((SSM_HEADS // 2, LANES, SSM_STATE), F32)],
        input_output_aliases={5 + len(consts) + 4: 0},
        compiler_params=_cparams(("arbitrary", "arbitrary"), 48),
        name="ssd",
    )(xbc, xbc, xbc, dt, z, *consts, h0f, h0b, lay["d_skip"], lay["ssm_norm_g"], prev_out)


def _make_outproj_kernel(n_x, npt):
    def kern(*refs):
        oa_ref, ob_ref, oc_ref = refs[:3]
        xs = refs[3:3 + n_x]
        gate_ref, w_ref, g_ref, sh_ref, sc_ref, xo_ref, h_ref = refs[3 + n_x:]
        na = ATT_HEADS * HEAD_DIM
        o = (_dot(oa_ref[...], w_ref[0:na, :]) + _dot(ob_ref[...], w_ref[na:na + D_SSM, :])
             + _dot(oc_ref[...], w_ref[na + D_SSM:, :]))
        xn = _pick_x(pl.program_id(0), npt, xs) + gate_ref[...] * o
        xo_ref[...] = xn
        h_ref[...] = _norm_mod(xn, g_ref[...], sh_ref[...], sc_ref[...]).astype(h_ref.dtype)
    return kern


def _outproj(oa, ob, oc, x, lay, modv, layer, dims, h_dtype):
    t, d = dims["tokens"], x[0].shape[1]
    npt = dims["prompt_tokens"] // TM
    row_of_tile = dims["row_of_tile"](TM)
    tok = lambda w: pl.BlockSpec((TM, w), lambda i: (i, 0))
    full = lambda a: pl.BlockSpec(a.shape, lambda i: (0,) * a.ndim)
    return pl.pallas_call(
        _make_outproj_kernel(len(x), npt),
        grid=(t // TM,),
        in_specs=[tok(oa.shape[1]), tok(ob.shape[1]), tok(oc.shape[1])] + _x_specs(x, TM, npt) + [
            _mod_spec(layer, 2, d, row_of_tile), full(lay["w_out"]), full(lay["norm2_g"]),
            _mod_spec(layer, 3, d, row_of_tile), _mod_spec(layer, 4, d, row_of_tile)],
        out_specs=[tok(d), tok(d)],
        out_shape=[jax.ShapeDtypeStruct((t, d), F32), jax.ShapeDtypeStruct((t, d), h_dtype)],
        compiler_params=_cparams(("arbitrary",), 40),
        name="outproj",
    )(oa, ob, oc, *x, modv, lay["w_out"], lay["norm2_g"], modv, modv)


def _swiglu_tile(x, wg_ref, wu_ref, wd_ref):
    f = wg_ref.shape[1]
    part = None
    for c0 in range(0, f, FFN_SLICE):
        c1 = min(f, c0 + FFN_SLICE)
        g = _dot(x, wg_ref[:, c0:c1].astype(BF16))
        u = _dot(x, wu_ref[:, c0:c1].astype(BF16))
        d = _dot((_silu(g) * u).astype(BF16), wd_ref[c0:c1, :].astype(BF16))
        part = d if part is None else part + d
    return part


def _ffn_kernel(h_ref, wg_ref, wu_ref, wd_ref, x_ref, gate_ref, o_ref):
    o_ref[...] = x_ref[...] + gate_ref[...] * _swiglu_tile(h_ref[...], wg_ref, wu_ref, wd_ref)


def _dense_ffn(h, x, wg, wu, wd, modv, layer, dims):
    t, d = x.shape
    f = wg.shape[1]
    row_of_tile = dims["row_of_tile"](TM)
    gate_base = (layer * 6 + 5) * MOD_ROWS
    resident = lambda shape: pl.BlockSpec(shape, lambda i: (0, 0), pipeline_mode=pl.Buffered(1))
    return pl.pallas_call(
        _ffn_kernel,
        grid=(t // TM,),
        in_specs=[pl.BlockSpec((TM, d), lambda i: (i, 0)),
                  resident((d, f)), resident((d, f)), resident((f, d)),
                  pl.BlockSpec((TM, d), lambda i: (i, 0)),
                  pl.BlockSpec((None, 1, d), lambda i: (gate_base + row_of_tile(i), 0, 0))],
        out_specs=pl.BlockSpec((TM, d), lambda i: (i, 0)),
        out_shape=jax.ShapeDtypeStruct((t, d), F32),
        compiler_params=_cparams(("arbitrary",), 56),
        name="dense_ffn",
    )(h, wg, wu, wd, x, modv)


def _route_kernel(h_ref, r_ref, tri_ref, info_ref, w_ref, cnt_ref):
    h_hi, h_lo = _split_bf16(h_ref[...], 2)
    r_hi, r_lo = _split_bf16(r_ref[...], 2)
    logits = _dot(h_hi, r_hi) + (_dot(h_lo, r_hi) + _dot(h_hi, r_lo))
    tm = logits.shape[0]
    lane = lax.broadcasted_iota(jnp.int32, (tm, LANES), 1)
    logits = jnp.where(lane < N_EXPERTS, logits, NEG_BIG)
    l1 = jnp.max(logits, axis=-1, keepdims=True)
    e1 = jnp.min(jnp.where(logits == l1, lane, LANES), axis=-1, keepdims=True)
    rest = jnp.where(lane == e1, NEG_BIG, logits)
    l2 = jnp.max(rest, axis=-1, keepdims=True)
    e2 = jnp.min(jnp.where(rest == l2, lane, LANES), axis=-1, keepdims=True)
    ex = jnp.exp(l2 - l1)
    w1 = 1.0 / (1.0 + ex)
    w2 = ex / (1.0 + ex)
    member = jnp.where((lane == e1) | (lane == e2), 1.0, 0.0)
    cnt = jnp.sum(member, axis=0, keepdims=True)
    run = jnp.floor((cnt + (SUBLANES - 1)) * (1.0 / SUBLANES)) * SUBLANES
    run_start = _dot_nt(jnp.broadcast_to(run, (SUBLANES, LANES)).astype(BF16),
                        tri_ref[0:LANES, 0:LANES])[0:1, :]
    before = _dot(tri_ref[...], member.astype(BF16)) + run_start
    r1 = jnp.sum(jnp.where(lane == e1, before, 0.0), axis=-1, keepdims=True)
    r2 = jnp.sum(jnp.where(lane == e2, before, 0.0), axis=-1, keepdims=True)
    info = jnp.where(lane == 0, e1.astype(F32),
                     jnp.where(lane == 1, e2.astype(F32),
                               jnp.where(lane == 2, r1, jnp.where(lane == 3, r2, 0.0))))
    info_ref[...] = info.astype(jnp.int32)
    w_ref[...] = jnp.where(lane == 0, w1, jnp.where(lane == 1, w2, 0.0))
    cnt_ref[...] = cnt


def _route(h, router_pad, tri):
    t, d = h.shape
    return pl.pallas_call(
        _route_kernel,
        grid=(t // TM,),
        in_specs=[pl.BlockSpec((TM, d), lambda i: (i, 0)),
                  pl.BlockSpec(router_pad.shape, lambda i: (0, 0)),
                  pl.BlockSpec(tri.shape, lambda i: (0, 0))],
        out_specs=[pl.BlockSpec((TM, LANES), lambda i: (i, 0)),
                   pl.BlockSpec((TM, LANES), lambda i: (i, 0)),
                   pl.BlockSpec((None, 1, LANES), lambda i: (i, 0, 0))],
        out_shape=[jax.ShapeDtypeStruct((t, LANES), jnp.int32),
                   jax.ShapeDtypeStruct((t, LANES), F32),
                   jax.ShapeDtypeStruct((t // TM, 1, LANES), F32)],
        compiler_params=_cparams(("arbitrary",), 40),
        name="moe_route",
    )(h, router_pad, tri)


def _group_copies(src, src_row, dst, dst_row, n_groups, sem, wait=False, same_src=False):
    def body(g, carry):
        src_g = 0 if (same_src or wait) else g
        dst_g = 0 if wait else g
        copy = pltpu.make_async_copy(
            src.at[pl.ds(pl.multiple_of(src_row + src_g * SUBLANES, SUBLANES), SUBLANES), :],
            dst.at[pl.ds(pl.multiple_of(dst_row + dst_g * SUBLANES, SUBLANES), SUBLANES), :], sem)
        if wait:
            copy.wait()
        else:
            copy.start()
        return carry

    lax.fori_loop(0, n_groups, body, 0)


def _dispatch_kernel(seg_ref, off_ref, ng_ref, gap_ref, pos_ref, h_ref, xb_ref,
                     sorted_ref, zero_ref, sem, zsem):
    s = pl.program_id(0)
    rows, tm = sorted_ref.shape[0], h_ref.shape[0]
    slot = lax.broadcasted_iota(jnp.int32, (rows, tm), 0)
    place = jnp.where(slot == pos_ref[0:1, :], 1.0, jnp.where(slot == pos_ref[1:2, :], 1.0, 0.0))
    sorted_ref[...] = _dot(place.astype(BF16), h_ref[...].astype(BF16))

    def runs(wait):
        for e in range(N_EXPERTS):
            k = s * N_EXPERTS + e
            _group_copies(sorted_ref, 0 if wait else off_ref[k], xb_ref, 0 if wait else seg_ref[k],
                          ng_ref[k], sem, wait)

    runs(False)

    @pl.when(s == 0)
    def _():
        zero_ref[...] = jnp.zeros_like(zero_ref)
        for wait in (False, True):
            for g in range(N_EXPERTS + 1):
                _group_copies(zero_ref, 0, xb_ref, 0 if wait else gap_ref[2 * g], gap_ref[2 * g + 1],
                              zsem, wait, same_src=True)

    runs(True)


def _dispatch(h, pos_rows, seg, off, ngroups, gaps, n_rows):
    t, d = h.shape
    rows = TM * TOP_K + N_EXPERTS * SUBLANES
    return pl.pallas_call(
        _dispatch_kernel,
        grid_spec=pltpu.PrefetchScalarGridSpec(
            num_scalar_prefetch=4,
            grid=(t // TM,),
            in_specs=[pl.BlockSpec((None, TOP_K, TM), lambda i, *_: (i, 0, 0)),
                      pl.BlockSpec((TM, d), lambda i, *_: (i, 0))],
            out_specs=pl.BlockSpec(memory_space=pl.ANY),
            scratch_shapes=[pltpu.VMEM((rows, d), F32), pltpu.VMEM((SUBLANES, d), F32),
                            pltpu.SemaphoreType.DMA(()), pltpu.SemaphoreType.DMA(())]),
        out_shape=jax.ShapeDtypeStruct((n_rows, d), F32),
        compiler_params=_cparams(("arbitrary",), 48),
        name="moe_dispatch",
    )(seg, off, ngroups, gaps, pos_rows, h)


def _expert_kernel(be_ref, bv_ref, x_ref, wg_ref, wu_ref, wd_ref, y_ref, xb_scr):
    i = pl.program_id(0)
    j = pl.program_id(1)

    @pl.when(bv_ref[i] > 0)
    def _():
        @pl.when(j == 0)
        def _():
            xb_scr[...] = x_ref[...].astype(BF16)
            y_ref[...] = jnp.zeros_like(y_ref)

        y_ref[...] += _swiglu_tile(xb_scr[...], wg_ref, wu_ref, wd_ref)

    @pl.when((bv_ref[i] == 0) & (j == 0))
    def _():
        y_ref[...] = jnp.zeros_like(y_ref)


def _experts(xb, block_e, block_v, wg, wu, wd):
    n_rows, d = xb.shape
    f = wg.shape[2]

    def jj(i, j, bv):
        return jnp.where(bv[i] > 0, j, 0)

    return pl.pallas_call(
        _expert_kernel,
        grid_spec=pltpu.PrefetchScalarGridSpec(
            num_scalar_prefetch=2,
            grid=(n_rows // MOE_BM, f // MOE_TF),
            in_specs=[pl.BlockSpec((MOE_BM, d), lambda i, j, be, bv: (i, 0)),
                      pl.BlockSpec((None, d, MOE_TF), lambda i, j, be, bv: (be[i], 0, jj(i, j, bv))),
                      pl.BlockSpec((None, d, MOE_TF), lambda i, j, be, bv: (be[i], 0, jj(i, j, bv))),
                      pl.BlockSpec((None, MOE_TF, d), lambda i, j, be, bv: (be[i], jj(i, j, bv), 0))],
            out_specs=pl.BlockSpec((MOE_BM, d), lambda i, j, be, bv: (i, 0)),
            scratch_shapes=[pltpu.VMEM((MOE_BM, d), BF16)]),
        out_shape=jax.ShapeDtypeStruct((n_rows, d), F32),
        compiler_params=_cparams(("arbitrary", "arbitrary"), 56),
        name="moe_experts",
    )(block_e, block_v, xb, wg, wu, wd)


def _make_combine_kernel(npt):
    def kern(seg_ref, off_ref, ng_ref, yb_ref, pos_ref, w_ref, x_ref, gate_ref, fg_ref,
             op_ref, os_ref, runs_ref, sems):
        i = pl.program_id(0)
        n = pl.num_programs(0)
        rows = runs_ref.shape[1]
        tm = x_ref.shape[0]

        def fetch(tile, slot, wait=False):
            for e in range(N_EXPERTS):
                k = tile * N_EXPERTS + e
                _group_copies(yb_ref, 0 if wait else seg_ref[k], runs_ref.at[slot],
                              0 if wait else off_ref[k], ng_ref[k], sems.at[slot], wait)

        @pl.when(i == 0)
        def _():
            runs_ref[...] = jnp.zeros_like(runs_ref)
            fetch(0, 0)

        @pl.when(i + 1 < n)
        def _():
            fetch(i + 1, (i + 1) % 2)

        slot = i % 2
        fetch(i, slot, wait=True)
        y_runs = runs_ref[slot].astype(BF16)
        col = lax.broadcasted_iota(jnp.int32, (tm, rows), 1)
        pos = pos_ref[...]
        picked = [_dot(jnp.where(col == pos[:, k:k + 1], 1.0, 0.0).astype(BF16), y_runs)
                  for k in range(TOP_K)]
        w = w_ref[...]
        y = w[:, 0:1] * picked[0] + w[:, 1:2] * picked[1]
        xn = x_ref[...] + gate_ref[...] * y
        ms = jnp.mean(xn * xn, axis=-1, keepdims=True)
        out = xn * lax.rsqrt(ms + NORM_EPS) * fg_ref[...]

        @pl.when(i < npt)
        def _():
            op_ref[...] = out

        @pl.when(i >= npt)
        def _():
            os_ref[...] = out
    return kern


def _combine(yb, pos, seg, off, ngroups, top_w, x, modv, layer, final_g, dims):
    t, d = x.shape
    tp = dims["prompt_tokens"]
    npt = tp // TM
    rows = TM * TOP_K + N_EXPERTS * SUBLANES
    row_of_tile = dims["row_of_tile"](TM)
    gate_base = (layer * 6 + 5) * MOD_ROWS
    return pl.pallas_call(
        _make_combine_kernel(npt),
        grid_spec=pltpu.PrefetchScalarGridSpec(
            num_scalar_prefetch=3,
            grid=(t // TM,),
            in_specs=[pl.BlockSpec(memory_space=pl.ANY),
                      pl.BlockSpec((TM, TOP_K), lambda i, *_: (i, 0)),
                      pl.BlockSpec((TM, LANES), lambda i, *_: (i, 0)),
                      pl.BlockSpec((TM, d), lambda i, *_: (i, 0)),
                      pl.BlockSpec((None, 1, d), lambda i, *_: (gate_base + row_of_tile(i), 0, 0)),
                      pl.BlockSpec((1, d), lambda i, *_: (0, 0))],
            out_specs=[pl.BlockSpec((TM, d), lambda i, *_: (jnp.minimum(i, npt - 1), 0)),
                       pl.BlockSpec((TM, d), lambda i, *_: (jnp.maximum(i - npt, 0), 0))],
            scratch_shapes=[pltpu.VMEM((2, rows, d), F32), pltpu.SemaphoreType.DMA((2,))]),
        out_shape=[jax.ShapeDtypeStruct((tp, d), F32), jax.ShapeDtypeStruct((t - tp, d), F32)],
        compiler_params=_cparams(("arbitrary",), 56),
        name="moe_combine",
    )(seg, off, ngroups, yb, pos, top_w, x, modv, final_g)


def _moe_layer(h, x, router, wg, wu, wd, modv, layer, final_g, dims):
    t, d = x.shape
    n_tiles = t // TM
    router_pad = jnp.zeros((d, LANES), F32).at[:, :N_EXPERTS].set(router)
    info, top_w, counts = _route(h, router_pad, dims["tri_strict"])
    run = (counts[:, 0, :N_EXPERTS].astype(jnp.int32) + SUBLANES - 1) // SUBLANES * SUBLANES
    off = jnp.cumsum(run, axis=1) - run
    before = jnp.cumsum(run, axis=0) - run
    total = jnp.sum(run, axis=0)
    padded = (total + MOE_BM - 1) // MOE_BM * MOE_BM
    pad_end = jnp.cumsum(padded)
    pad_start = pad_end - padded
    seg = pad_start[None, :] + before
    n_rows = (t * TOP_K + n_tiles * N_EXPERTS * (SUBLANES - 1) + MOE_BM - 1) // MOE_BM * MOE_BM \
        + N_EXPERTS * MOE_BM
    pos = info[:, TOP_K:2 * TOP_K]
    pos_rows = pos.reshape(n_tiles, TM, TOP_K).transpose(0, 2, 1)
    flat = lambda a: a.reshape(-1).astype(jnp.int32)
    gap_lo = jnp.concatenate([pad_start + total, pad_end[-1:]])
    gap_hi = jnp.concatenate([pad_end, jnp.array([n_rows], jnp.int32)])
    gaps = jnp.stack([gap_lo, (gap_hi - gap_lo) // SUBLANES], axis=1)
    block_start = jnp.arange(n_rows // MOE_BM, dtype=jnp.int32) * MOE_BM
    block_e = jnp.sum(pad_end[None, :] <= block_start[:, None], axis=1).astype(jnp.int32)
    block_v = (block_e < N_EXPERTS).astype(jnp.int32)
    last_e = jnp.max(jnp.where(padded > 0, jnp.arange(N_EXPERTS), 0)).astype(jnp.int32)
    block_e = jnp.where(block_v > 0, block_e, last_e)
    xb = _dispatch(h, pos_rows, flat(seg), flat(off), flat(run // SUBLANES), flat(gaps), n_rows)
    yb = _experts(xb, block_e, block_v, wg, wu, wd)
    return _combine(yb, pos, flat(seg), flat(off), flat(run // SUBLANES), top_w, x, modv, layer,
                    final_g, dims)


def _rope_tables(dec_seq):
    n_rows = dec_seq // GRID_W
    row = jnp.repeat(jnp.arange(n_rows, dtype=F32), GRID_W)
    col = (jnp.arange(dec_seq) % GRID_W).astype(F32)
    nf = HEAD_DIM // 4
    inv = ROPE_THETA ** (-jnp.arange(nf, dtype=F32) / nf)
    ang = jnp.stack([row[:, None] * inv, col[:, None] * inv], axis=1)
    cos = jnp.cos(ang)
    sin = jnp.sin(ang)
    cos_h = jnp.concatenate([cos, cos], axis=-1).reshape(dec_seq, HEAD_DIM)
    sin_h = jnp.concatenate([-sin, sin], axis=-1).reshape(dec_seq, HEAD_DIM)
    cos_t = jnp.concatenate([cos_h, cos_h], axis=-1)
    sin_t = jnp.concatenate([sin_h, sin_h], axis=-1)
    cos_t = jnp.concatenate([cos_t, jnp.ones((TM, LANES), F32)], axis=0)
    sin_t = jnp.concatenate([sin_t, jnp.zeros((TM, LANES), F32)], axis=0)
    return cos_t, sin_t


def _expansion(direction, width):
    src = direction * SSM_HEADS + np.arange(SSM_HEADS * width) // width
    return jnp.asarray((np.arange(LANES)[:, None] == src[None, :]).astype(np.float32)).astype(BF16)


def _layer_params(l, norm1_g, norm2_g, w_in, q_norm_g, k_norm_g, conv_w, conv_b, a_log, dt_bias,
                  d_skip, ssm_norm_g, sink, w_out):
    d = w_in.shape[1]
    pad = lambda v: jnp.zeros((1, LANES), F32).at[0, :v.size].set(v.reshape(-1))
    seg = np.arange(LANES) // HEAD_DIM
    bd = jnp.asarray((seg[:, None] == seg[None, :]).astype(np.float32) / HEAD_DIM).astype(BF16)
    r = np.arange(SSM_CHUNK)
    return {
        "norm1_g": norm1_g[l].reshape(1, d), "norm2_g": norm2_g[l].reshape(1, d),
        "w_main": w_in[l, :, :OFF_DT].astype(BF16),
        "w_dt": jnp.zeros((d, LANES), F32).at[:, :2 * SSM_HEADS].set(w_in[l, :, OFF_DT:]).astype(BF16),
        "bd": bd,
        "q_norm_g": jnp.tile(q_norm_g[l], 2).reshape(1, LANES),
        "k_norm_g": jnp.tile(k_norm_g[l], 2).reshape(1, LANES),
        "conv_w": jnp.zeros((SUBLANES, XBC_DIM), F32).at[:D_CONV].set(conv_w[l]),
        "conv_b": conv_b[l].reshape(1, XBC_DIM),
        "a_log": pad(a_log[l]), "dt_bias": pad(dt_bias[l]),
        "d_skip": jnp.repeat(d_skip[l], SSM_HEAD_DIM).reshape(1, D_SSM),
        "ssm_norm_g": ssm_norm_g[l].reshape(1, D_SSM),
        "sink": sink[l],
        "w_out": w_out[l].astype(BF16),
        "tri_fwd": jnp.asarray((r[:, None] >= r[None, :]).astype(np.float32)).astype(BF16),
        "tri_bwd": jnp.asarray((r[:, None] <= r[None, :]).astype(np.float32)).astype(BF16),
        "wide_fwd": _expansion(0, LANES), "wide_bwd": _expansion(1, LANES),
        "pair_fwd": _expansion(0, SSM_HEAD_DIM), "pair_bwd": _expansion(1, SSM_HEAD_DIM),
    }


def kernel(x_prompt, x_sample, cache_attn_k, cache_attn_v, cache_win_k, cache_win_v, state_ssm_fwd, state_ssm_bwd, c, c_ctx, norm1_g, norm2_g, w_mod, b_mod, w_in, q_norm_g, k_norm_g, conv_w, conv_b, a_log, dt_bias, d_skip, ssm_norm_g, sink, w_out, ffn_w_gate, ffn_w_up, ffn_w_down, moe_router, moe_w_gate, moe_w_up, moe_w_down, final_g):
    batch, seq, d = x_prompt.shape
    dec_batch, dec_seq, _ = x_sample.shape
    depth = w_in.shape[0]
    past = cache_attn_k.shape[2]
    tp = batch * seq
    t = tp + dec_batch * dec_seq
    assert depth == 2 and tp % (2 * TM) == 0 and dec_seq % (2 * TM) == 0
    assert 1 + dec_batch <= MOD_ROWS and seq % SSM_CHUNK == 0 and tp % dec_seq == 0

    def row_of_tile(tm):
        npt, tpb = tp // tm, dec_seq // tm
        return lambda i: jnp.where(i < npt, 0, 1 + (i - npt) // tpb)

    cos_t, sin_t = _rope_tables(dec_seq)
    rr = np.arange(TM)
    dims = {"tokens": t, "prompt_tokens": tp, "dec_seq": dec_seq, "row_of_tile": row_of_tile,
            "rope_cos": cos_t, "rope_sin": sin_t,
            "tri_strict": jnp.asarray((rr[:, None] > rr[None, :]).astype(np.float32)).astype(BF16)}

    cc = jnp.zeros((MOD_ROWS, d), F32).at[0].set(c_ctx).at[1:1 + dec_batch].set(c)
    mod = _mod_vectors(cc, w_mod, b_mod)
    modv = mod.reshape(depth, MOD_ROWS, 6, d).transpose(0, 2, 1, 3).reshape(depth * 6 * MOD_ROWS, 1, d)

    x = (x_prompt.reshape(tp, d), x_sample.reshape(dec_batch * dec_seq, d))
    zero_state = jnp.zeros((batch, SSM_HEADS // 2, LANES, SSM_STATE), F32)
    caches = [[] for _ in range(6)]
    y = None
    for l in range(depth):
        lay = _layer_params(l, norm1_g, norm2_g, w_in, q_norm_g, k_norm_g, conv_w, conv_b, a_log,
                            dt_bias, d_skip, ssm_norm_g, sink, w_out)
        (qa, ka, va, qc, kc, vc, z, xbc, dt, cka, cva, ckc, cvc) = _inproj(x, lay, modv, l, dims)
        ctx = lambda a: a[:, l].reshape(dec_batch, past, LANES).astype(BF16)

        oa = _attention(qa, ka, va, None, None, None, None, nb=batch, seq=seq, tq=seq, row_off=0,
                        window=False, name="attn_a_ctx")
        oa = _attention(qa, ka, va, ctx(cache_attn_k), ctx(cache_attn_v), None, oa, nb=dec_batch,
                        seq=dec_seq, tq=TQ_GLOBAL, row_off=tp, window=False, name="attn_a_lat")
        oc = _attention(qc, kc, vc, None, None, lay["sink"], None, nb=batch, seq=seq, tq=seq,
                        row_off=0, window=False, name="attn_c_ctx")
        oc = _attention(qc, kc, vc, ctx(cache_win_k), ctx(cache_win_v), lay["sink"], oc,
                        nb=dec_batch, seq=dec_seq, tq=TQ_WINDOW, row_off=tp, window=True,
                        name="attn_c_lat")
        pair = lambda s: s[:, l].reshape(dec_batch, SSM_HEADS // 2, LANES, SSM_STATE)
        ob, hf_p, hb_p = _ssd(xbc, dt, z, lay, zero_state, zero_state, None, nb=batch, seq=seq,
                              row_off=0)
        ob, _, _ = _ssd(xbc, dt, z, lay, pair(state_ssm_fwd), pair(state_ssm_bwd), ob, nb=dec_batch,
                        seq=dec_seq, row_off=tp)

        kv = lambda a: a[:tp].reshape(batch, seq, ATT_KV_HEADS, HEAD_DIM)
        st = lambda s: s.reshape(batch, SSM_HEADS, SSM_HEAD_DIM, SSM_STATE)
        for lst, val in zip(caches, (kv(cka), kv(cva), kv(ckc), kv(cvc), st(hf_p), st(hb_p))):
            lst.append(val)

        i = l // 2
        if l % 2 == 0:
            xr, h2 = _outproj(oa, ob, oc, x, lay, modv, l, dims, BF16)
            x = (_dense_ffn(h2, xr, ffn_w_gate[i].astype(BF16), ffn_w_up[i].astype(BF16),
                            ffn_w_down[i].astype(BF16), modv, l, dims),)
        else:
            xr, h2 = _outproj(oa, ob, oc, x, lay, modv, l, dims, F32)
            y = _moe_layer(h2, xr, moe_router[i], moe_w_gate[i], moe_w_up[i], moe_w_down[i], modv, l,
                           final_g.reshape(1, d), dims)

    y_prompt = y[0].reshape(batch, seq, d)
    y_sample = y[1].reshape(dec_batch, dec_seq, d)
    return (y_prompt, y_sample) + tuple(jnp.stack(lst, axis=1) for lst in caches)
```

```python
import numpy as np
import jax
import jax.numpy as jnp
from jax import lax
from jax.experimental import pallas as pl
from jax.experimental.pallas import tpu as pltpu

F32 = jnp.float32
BF16 = jnp.bfloat16
HIGHEST = lax.Precision.HIGHEST

GRID_W = 64
HEAD_DIM = 64
ATT_HEADS = 4
ATT_KV_HEADS = 2
WIN_HEADS = 4
WIN_KV_HEADS = 2
WINDOW = 128
ROPE_THETA = 10000.0
SSM_HEADS = 8
SSM_HEAD_DIM = 64
D_SSM = SSM_HEADS * SSM_HEAD_DIM
SSM_GROUPS = 2
SSM_STATE = 128
D_CONV = 5
SSM_CHUNK = 128
XBC_DIM = D_SSM + 2 * SSM_GROUPS * SSM_STATE
N_EXPERTS = 8
TOP_K = 2
NORM_EPS = 1e-6
NEG_BIG = -1e30
LOG2E = 1.4426950408889634

LANES = 128
SUBLANES = 8
MOD_ROWS = 16

OFF_QA, OFF_KA, OFF_VA, OFF_QC, OFF_KC, OFF_VC, OFF_Z, OFF_XBC, OFF_DT = (
    0, 256, 384, 512, 768, 896, 1024, 1536, 2560)

TM = 512
TQ_GLOBAL = 512
TQ_WINDOW = 256
ATTN_KEY_CHUNK = 256
ATTN_KEY_CHUNK_WINDOW = 512
MOE_BM = 1024
MOE_TF = 512
SSD_STEP = 1024
FFN_SLICE = 256


def _cparams(sem, vmem_mb):
    return pltpu.CompilerParams(dimension_semantics=sem, vmem_limit_bytes=vmem_mb * 1024 * 1024)


def _dot(a, b, precision=None):
    return jnp.dot(a, b, preferred_element_type=F32, precision=precision)


def _dot_nt(a, b):
    return lax.dot_general(a, b, (((1,), (1,)), ((), ())), preferred_element_type=F32)


def _silu(x):
    return x / (1.0 + jnp.exp(-x))


def _norm_mod(x, g, shift, scale):
    ms = jnp.mean(x * x, axis=-1, keepdims=True)
    y = x * lax.rsqrt(ms + NORM_EPS) * g
    return y * (1.0 + scale) + shift


def _pick_x(i, npt, xs):
    if len(xs) == 1:
        return xs[0][...]
    return jnp.where(i < npt, xs[0][...], xs[1][...])


def _x_specs(x, tm, npt):
    d = x[0].shape[1]
    if len(x) == 1:
        return [pl.BlockSpec((tm, d), lambda i, *_: (i, 0))]
    return [pl.BlockSpec((tm, d), lambda i, *_: (jnp.minimum(i, npt - 1), 0)),
            pl.BlockSpec((tm, d), lambda i, *_: (jnp.maximum(i - npt, 0), 0))]


def _mod_kernel(c_ref, w_ref, b_ref, o_ref):
    a = _silu(c_ref[...])
    o_ref[...] = _dot(a, w_ref[...], HIGHEST) + b_ref[...]


def _mod_vectors(cc, w_mod, b_mod):
    depth, d, n = w_mod.shape
    tn = 1536
    return pl.pallas_call(
        _mod_kernel,
        grid=(depth, n // tn),
        in_specs=[pl.BlockSpec((MOD_ROWS, d), lambda l, j: (0, 0)),
                  pl.BlockSpec((None, d, tn), lambda l, j: (l, 0, j)),
                  pl.BlockSpec((None, 1, tn), lambda l, j: (l, 0, j))],
        out_specs=pl.BlockSpec((None, MOD_ROWS, tn), lambda l, j: (l, 0, j)),
        out_shape=jax.ShapeDtypeStruct((depth, MOD_ROWS, n), F32),
        compiler_params=_cparams(("arbitrary", "arbitrary"), 40),
        name="mod_vectors",
    )(cc, w_mod, b_mod.reshape(depth, 1, n))


def _mod_spec(layer, k, d, row_of_tile):
    base = (layer * 6 + k) * MOD_ROWS
    return pl.BlockSpec((None, 1, d), lambda i, *_: (base + row_of_tile(i), 0, 0))


def _make_inproj_kernel(n_x, npt):
    def kern(*refs):
        xs = refs[:n_x]
        (g_ref, sh_ref, sc_ref, w_ref, wdt_ref, bd_ref, qg_ref, kg_ref, cos_ref, sin_ref,
         qa_ref, ka_ref, va_ref, qc_ref, kc_ref, vc_ref, z_ref, xbc_ref, dt_ref,
         cka_ref, cva_ref, ckc_ref, cvc_ref) = refs[n_x:]
        x = _pick_x(pl.program_id(0), npt, xs)
        h = _norm_mod(x, g_ref[...], sh_ref[...], sc_ref[...]).astype(BF16)

        def proj(a, b):
            return _dot(h, w_ref[:, a:b])

        cos = cos_ref[...]
        sin = sin_ref[...]
        tm = cos.shape[0]
        lane = lax.broadcasted_iota(jnp.int32, (tm, LANES), 1)
        first = (lane % 32) < 16
        low = lane < HEAD_DIM
        bd = bd_ref[...]

        def rope(v):
            partner = jnp.where(first, pltpu.roll(v, LANES - 16, 1), pltpu.roll(v, 16, 1))
            return v * cos + partner * sin

        def head_norm(v, g):
            sq = v * v
            hi = sq.astype(BF16)
            lo = (sq - hi.astype(F32)).astype(BF16)
            ms = _dot(hi, bd) + _dot(lo, bd)
            return v * lax.rsqrt(ms + NORM_EPS) * g

        def stack_heads(c0, c1, out_ref):
            zero = jnp.zeros_like(c0)
            out_ref[0] = jnp.where(low, c0, zero).astype(out_ref.dtype)
            out_ref[1] = jnp.where(low, pltpu.roll(c0, HEAD_DIM, 1), zero).astype(out_ref.dtype)
            out_ref[2] = jnp.where(low, zero, pltpu.roll(c1, HEAD_DIM, 1)).astype(out_ref.dtype)
            out_ref[3] = jnp.where(low, zero, c1).astype(out_ref.dtype)

        scale = LOG2E * HEAD_DIM ** -0.5
        qg = qg_ref[...]
        q_a = proj(OFF_QA, OFF_KA)
        qa0 = rope(head_norm(q_a[:, 0:LANES], qg)) * scale
        qa1 = rope(head_norm(q_a[:, LANES:2 * LANES], qg)) * scale
        stack_heads(qa0, qa1, qa_ref)
        kv_a = proj(OFF_KA, OFF_QC)
        ka = rope(head_norm(kv_a[:, 0:LANES], kg_ref[...]))
        ka_ref[...] = ka.astype(ka_ref.dtype)
        cka_ref[...] = ka
        va = kv_a[:, LANES:2 * LANES]
        va_ref[...] = va.astype(va_ref.dtype)
        cva_ref[...] = va

        q_c = proj(OFF_QC, OFF_KC)
        qc0 = rope(q_c[:, 0:LANES]) * scale
        qc1 = rope(q_c[:, LANES:2 * LANES]) * scale
        stack_heads(qc0, qc1, qc_ref)
        kv_c = proj(OFF_KC, OFF_Z)
        kc = rope(kv_c[:, 0:LANES])
        kc_ref[...] = kc.astype(kc_ref.dtype)
        ckc_ref[...] = kc
        vc = kv_c[:, LANES:2 * LANES]
        vc_ref[...] = vc.astype(vc_ref.dtype)
        cvc_ref[...] = vc

        z_ref[...] = proj(OFF_Z, OFF_XBC)
        xbc_ref[...] = proj(OFF_XBC, OFF_DT)
        dt_ref[...] = _dot(h, wdt_ref[...])
    return kern


def _inproj(x, lay, modv, layer, dims):
    t, d = dims["tokens"], x[0].shape[1]
    n_tiles = t // TM
    npt = dims["prompt_tokens"] // TM
    tpb = dims["dec_seq"] // TM
    row_of_tile = dims["row_of_tile"](TM)

    def rope_idx(i):
        return jnp.where(i < npt, tpb, (i - npt) % tpb)

    def cache_idx(i):
        return jnp.minimum(i, npt)

    tok = lambda w: pl.BlockSpec((TM, w), lambda i: (i, 0))
    full = lambda a: pl.BlockSpec(a.shape, lambda i: (0,) * a.ndim)
    stack = pl.BlockSpec((4, TM, LANES), lambda i: (0, i, 0))
    cache = pl.BlockSpec((TM, LANES), lambda i: (cache_idx(i), 0))
    rope = pl.BlockSpec((TM, LANES), lambda i: (rope_idx(i), 0))
    cache_shape = jax.ShapeDtypeStruct(((npt + 1) * TM, LANES), F32)
    return pl.pallas_call(
        _make_inproj_kernel(len(x), npt),
        grid=(n_tiles,),
        in_specs=_x_specs(x, TM, npt) + [
            full(lay["norm1_g"]),
            _mod_spec(layer, 0, d, row_of_tile), _mod_spec(layer, 1, d, row_of_tile),
            full(lay["w_main"]), full(lay["w_dt"]), full(lay["bd"]),
            full(lay["q_norm_g"]), full(lay["k_norm_g"]), rope, rope],
        out_specs=[stack, tok(LANES), tok(LANES), stack, tok(LANES), tok(LANES),
                   tok(D_SSM), tok(XBC_DIM), tok(LANES), cache, cache, cache, cache],
        out_shape=[jax.ShapeDtypeStruct((4, t, LANES), BF16),
                   jax.ShapeDtypeStruct((t, LANES), BF16), jax.ShapeDtypeStruct((t, LANES), BF16),
                   jax.ShapeDtypeStruct((4, t, LANES), BF16),
                   jax.ShapeDtypeStruct((t, LANES), BF16), jax.ShapeDtypeStruct((t, LANES), BF16),
                   jax.ShapeDtypeStruct((t, D_SSM), F32), jax.ShapeDtypeStruct((t, XBC_DIM), F32),
                   jax.ShapeDtypeStruct((t, LANES), F32),
                   cache_shape, cache_shape, cache_shape, cache_shape],
        compiler_params=_cparams(("arbitrary",), 48),
        name="inproj",
    )(*x, lay["norm1_g"], modv, modv, lay["w_main"], lay["w_dt"], lay["bd"],
      lay["q_norm_g"], lay["k_norm_g"], dims["rope_cos"], dims["rope_sin"])


def _make_attn_kernel(tq, n_ctx, n_lat, window_len, has_sink):
    ck = ATTN_KEY_CHUNK_WINDOW if window_len else ATTN_KEY_CHUNK

    def kern(*refs):
        refs = list(refs)
        sink_ref = refs.pop(0) if has_sink else None
        q_ref = refs.pop(0)
        kc_ref = refs.pop(0) if n_ctx else None
        vc_ref = refs.pop(0) if n_ctx else None
        kl_ref, vl_ref = refs.pop(0), refs.pop(0)
        refs.pop(0)
        o_ref, vaug_ref = refs

        @pl.when(pl.program_id(1) == 0)
        def _():
            if n_ctx:
                vaug_ref[0:n_ctx, 0:LANES] = vc_ref[...]
            vaug_ref[n_ctx:n_ctx + n_lat, 0:LANES] = vl_ref[...]
            vaug_ref[:, LANES:2 * LANES] = jnp.ones((n_ctx + n_lat, LANES), BF16)

        rows = 4 * tq
        q = q_ref[...].reshape(rows, LANES)
        if has_sink:
            head = lax.broadcasted_iota(jnp.int32, (rows, 1), 0) // tq
            snk = LOG2E * jnp.where(head == 0, sink_ref[0],
                                    jnp.where(head == 1, sink_ref[1],
                                              jnp.where(head == 2, sink_ref[2], sink_ref[3])))
            m = snk
        else:
            m = jnp.full((rows, 1), NEG_BIG, F32)
        acc = jnp.zeros((rows, 2 * LANES), F32)

        chunks = []
        for c0 in range(0, n_ctx, ck):
            n = min(ck, n_ctx - c0)
            chunks.append((kc_ref[c0:c0 + n, :], vaug_ref[c0:c0 + n, :], None))
        if window_len:
            q0 = pl.program_id(1) * tq
            ws = pl.multiple_of(jnp.clip(q0 - WINDOW, 0, n_lat - window_len), LANES)
            qpos = q0 + lax.broadcasted_iota(jnp.int32, (rows, window_len), 0) % tq
            kpos = ws + lax.broadcasted_iota(jnp.int32, (rows, window_len), 1)
            valid = jnp.abs(qpos - kpos) <= WINDOW
            chunks.append((kl_ref[pl.ds(ws, window_len), :],
                           vaug_ref[pl.ds(n_ctx + ws, window_len), :], valid))
        else:
            for c0 in range(0, n_lat, ck):
                n = min(ck, n_lat - c0)
                chunks.append((kl_ref[c0:c0 + n, :], vaug_ref[n_ctx + c0:n_ctx + c0 + n, :], None))

        for keys, vaug, valid in chunks:
            s = _dot_nt(q, keys)
            if valid is not None:
                s = jnp.where(valid, s, NEG_BIG)
            m_new = jnp.maximum(m, jnp.max(s, axis=-1, keepdims=True))
            p = jnp.exp2(s - m_new).astype(BF16)
            acc = acc * jnp.exp2(m - m_new) + _dot(p, vaug)
            m = m_new

        den = acc[:, LANES:2 * LANES]
        if has_sink:
            den = den + jnp.exp2(snk - m)
        o = acc[:, 0:LANES] / den
        low = lax.broadcasted_iota(jnp.int32, (tq, LANES), 1) < HEAD_DIM
        o0, o1, o2, o3 = (o[h * tq:(h + 1) * tq] for h in range(4))
        c0 = jnp.where(low, o0, pltpu.roll(o1, HEAD_DIM, 1))
        c1 = jnp.where(low, pltpu.roll(o2, HEAD_DIM, 1), o3)
        o_ref[:, 0:LANES] = c0.astype(o_ref.dtype)
        o_ref[:, LANES:2 * LANES] = c1.astype(o_ref.dtype)
    return kern


def _attention(q_stack, k_lat, v_lat, k_ctx, v_ctx, sink, prev_out, *, nb, seq, tq, row_off,
               window, name):
    t = q_stack.shape[1]
    nq = seq // tq
    qoff = row_off // tq
    boff = row_off // seq
    n_ctx = 0 if k_ctx is None else k_ctx.shape[1]
    window_len = min(seq, tq + 2 * WINDOW) if window else 0
    kern = _make_attn_kernel(tq, n_ctx, seq, window_len, sink is not None)
    in_specs, args = [], []
    if sink is not None:
        in_specs.append(pl.BlockSpec(memory_space=pltpu.SMEM))
        args.append(sink)
    in_specs.append(pl.BlockSpec((4, tq, LANES), lambda b, j: (0, qoff + b * nq + j, 0)))
    args.append(q_stack)
    if n_ctx:
        ctx_spec = pl.BlockSpec((None, n_ctx, LANES), lambda b, j: (b, 0, 0))
        in_specs += [ctx_spec, ctx_spec]
        args += [k_ctx, v_ctx]
    lat_spec = pl.BlockSpec((seq, LANES), lambda b, j: (boff + b, 0))
    in_specs += [lat_spec, lat_spec]
    args += [k_lat, v_lat]
    if prev_out is None:
        prev_out = jnp.zeros((t, 2 * LANES), BF16)
    in_specs.append(pl.BlockSpec(memory_space=pl.ANY))
    args.append(prev_out)
    aliases = {len(args) - 1: 0}
    return pl.pallas_call(
        kern,
        grid=(nb, nq),
        in_specs=in_specs,
        out_specs=pl.BlockSpec((tq, 2 * LANES), lambda b, j: (qoff + b * nq + j, 0)),
        out_shape=jax.ShapeDtypeStruct((t, 2 * LANES), BF16),
        scratch_shapes=[pltpu.VMEM((n_ctx + seq, 2 * LANES), BF16)],
        input_output_aliases=aliases,
        compiler_params=_cparams(("arbitrary", "arbitrary"), 48),
        name=name,
    )(*args)


def _split_bf16(v, n):
    parts = []
    for _ in range(n):
        part = v.astype(BF16)
        parts.append(part)
        v = v - part.astype(F32)
    return parts


def _dot_parts(a, parts):
    out = _dot(a, parts[0])
    for part in parts[1:]:
        out = out + _dot(a, part)
    return out


def _parts_dot(parts, b):
    out = _dot(parts[0], b)
    for part in parts[1:]:
        out = out + _dot(part, b)
    return out


def _ssd_chunk(direction, xc, dt_raw, dtb_ref, alog_ref, tri_ref, ewide_ref, epair_ref, h_scr):
    q = SSM_CHUNK
    xs = xc[:, 0:D_SSM]
    bm = xc[:, D_SSM:D_SSM + SSM_GROUPS * SSM_STATE]
    cm = xc[:, D_SSM + SSM_GROUPS * SSM_STATE:]

    dtx = dt_raw + dtb_ref[...]
    dt = jnp.maximum(dtx, 0.0) + jnp.log1p(jnp.exp(-jnp.abs(dtx)))
    dta = dt * (-LOG2E * jnp.exp(alog_ref[...]))
    a_col = _dot_parts(tri_ref[...], _split_bf16(dta, 3))
    a_row = a_col.T
    a_wide = _parts_dot(_split_bf16(a_col, 3), ewide_ref[...])
    dt_lanes = _parts_dot(_split_bf16(dt, 2), epair_ref[...])
    row = lax.broadcasted_iota(jnp.int32, (q, q), 0)
    col = lax.broadcasted_iota(jnp.int32, (q, q), 1)
    live = (row >= col) if direction == 0 else (col >= row)
    low_lane = col < SSM_HEAD_DIM
    low_row = row < SSM_HEAD_DIM
    edge = q - 1 if direction == 0 else 0

    group = lambda v, g: v[:, g * SSM_STATE:(g + 1) * SSM_STATE]
    cm_b = [group(cm, g).astype(BF16) for g in range(SSM_GROUPS)]
    bm_b = [group(bm, g).astype(BF16) for g in range(SSM_GROUPS)]
    cbs = [_dot_nt(cm_b[g], bm_b[g]) for g in range(SSM_GROUPS)]

    rep = SSM_HEADS // SSM_GROUPS
    y_pairs = []
    for p in range(SSM_HEADS // 2):
        g = (2 * p) // rep
        hl = [direction * SSM_HEADS + 2 * p, direction * SSM_HEADS + 2 * p + 1]
        a_head = [a_wide[:, (2 * p + k) * LANES:(2 * p + k + 1) * LANES] for k in range(2)]
        a_pair = jnp.where(low_lane, a_head[0], a_head[1])
        a_edge = a_pair[edge:edge + 1, :]
        xdt = xs[:, p * LANES:(p + 1) * LANES] * dt_lanes[:, p * LANES:(p + 1) * LANES]
        hs = h_scr[p]
        m_both = jnp.concatenate(
            [(cbs[g] * jnp.exp2(jnp.where(live, a_head[k] - a_row[hl[k]:hl[k] + 1, :], NEG_BIG))
              ).astype(BF16) for k in range(2)], axis=1)
        x_both = jnp.concatenate([jnp.where(low_lane, xdt, 0.0).astype(BF16),
                                  jnp.where(low_lane, 0.0, xdt).astype(BF16)], axis=0)
        yd = _dot(m_both, x_both)
        yo = _dot_nt(cm_b[g], hs.astype(BF16)) * jnp.exp2(a_pair)
        y_pairs.append(yd + yo)
        st = _dot((xdt * jnp.exp2(a_edge - a_pair)).T.astype(BF16), bm_b[g])
        carry = [jnp.exp2(a_col[edge:edge + 1, l:l + 1]) for l in hl]
        h_scr[p] = hs * jnp.where(low_row, carry[0], carry[1]) + st
    return jnp.concatenate(y_pairs, axis=1), xs


def _make_ssd_kernel(ns, step):
    q = SSM_CHUNK
    halo = SUBLANES
    n_sub = step // q

    def kern(xp_ref, xm_ref, xn_ref, dt_ref, z_ref, cw_ref, cb_ref, dtb_ref, alog_ref,
             trif_ref, trib_ref, ewf_ref, ewb_ref, epf_ref, epb_ref,
             h0f_ref, h0b_ref, dsk_ref, ng_ref, _alias_ref,
             y_ref, hf_ref, hb_ref, ext_scr, xc_scr, yf_scr, h_scr):
        s = pl.program_id(1)

        @pl.when(s == 0)
        def _():
            h_scr[...] = h0f_ref[...]

        @pl.when(s == ns)
        def _():
            h_scr[...] = h0b_ref[...]

        @pl.when(s < ns)
        def _():
            ext_scr[0:halo, :] = jnp.where(s > 0, xp_ref[...], 0.0)
            ext_scr[halo:halo + step, :] = xm_ref[...]
            ext_scr[halo + step:2 * halo + step, :] = jnp.where(s < ns - 1, xn_ref[...], 0.0)
            ext = ext_scr[...]
            acc = cb_ref[...] + cw_ref[D_CONV // 2:D_CONV // 2 + 1, :] * ext[halo:halo + step]
            for k in range(D_CONV):
                if k != D_CONV // 2:
                    shifted = pltpu.roll(ext, (D_CONV // 2 - k) % (step + 2 * halo), 0)
                    acc = acc + cw_ref[k:k + 1, :] * shifted[halo:halo + step]
            xc = _silu(acc)
            r0 = pl.multiple_of(s * step, step)
            xc_scr[pl.ds(r0, step), :] = xc
            for c in range(n_sub):
                rows = slice(c * q, (c + 1) * q)
                y, xs = _ssd_chunk(0, xc[rows], dt_ref[rows, :], dtb_ref, alog_ref, trif_ref,
                                   ewf_ref, epf_ref, h_scr)
                yf_scr[pl.ds(r0 + c * q, q), :] = y + dsk_ref[...] * xs

            @pl.when(s == ns - 1)
            def _():
                hf_ref[...] = h_scr[...]

        @pl.when(s >= ns)
        def _():
            r0 = pl.multiple_of((2 * ns - 1 - s) * step, step)
            gw = D_SSM // SSM_GROUPS
            for c in reversed(range(n_sub)):
                rows = slice(c * q, (c + 1) * q)
                y, _ = _ssd_chunk(1, xc_scr[pl.ds(r0 + c * q, q), :], dt_ref[rows, :], dtb_ref,
                                  alog_ref, trib_ref, ewb_ref, epb_ref, h_scr)
                gated = (yf_scr[pl.ds(r0 + c * q, q), :] + y) * _silu(z_ref[rows, :])
                for g in range(SSM_GROUPS):
                    part = gated[:, g * gw:(g + 1) * gw]
                    ms = jnp.mean(part * part, axis=-1, keepdims=True)
                    y_ref[rows, g * gw:(g + 1) * gw] = (
                        part * lax.rsqrt(ms + NORM_EPS) * ng_ref[:, g * gw:(g + 1) * gw]).astype(y_ref.dtype)

            @pl.when(s == 2 * ns - 1)
            def _():
                hb_ref[...] = h_scr[...]
    return kern


def _ssd(xbc, dt, z, lay, h0f, h0b, prev_out, *, nb, seq, row_off):
    t = xbc.shape[0]
    step = min(seq, SSD_STEP)
    ns = seq // step
    soff = row_off // step
    last_halo = t // SUBLANES - 1
    per = step // SUBLANES

    fwd_blk = lambda b, s: soff + b * ns + jnp.minimum(s, ns - 1)
    any_blk = lambda b, s: soff + b * ns + jnp.where(s < ns, s, 2 * ns - 1 - s)
    bwd_blk = lambda b, s: soff + b * ns + jnp.where(s < ns, ns - 1, 2 * ns - 1 - s)

    full = lambda a: pl.BlockSpec(a.shape, lambda b, s: (0,) * a.ndim)
    prev = pl.BlockSpec((SUBLANES, XBC_DIM), lambda b, s: (jnp.maximum(fwd_blk(b, s) * per - 1, 0), 0))
    nxt = pl.BlockSpec((SUBLANES, XBC_DIM),
                       lambda b, s: (jnp.minimum((fwd_blk(b, s) + 1) * per, last_halo), 0))
    state = pl.BlockSpec((None, SSM_HEADS // 2, LANES, SSM_STATE), lambda b, s: (b, 0, 0, 0))
    out_spec = pl.BlockSpec((step, D_SSM), lambda b, s: (bwd_blk(b, s), 0))
    consts = [lay["conv_w"], lay["conv_b"], lay["dt_bias"], lay["a_log"], lay["tri_fwd"], lay["tri_bwd"],
              lay["wide_fwd"], lay["wide_bwd"], lay["pair_fwd"], lay["pair_bwd"]]
    if prev_out is None:
        prev_out = jnp.zeros((t, D_SSM), BF16)
    state_shape = jax.ShapeDtypeStruct((nb, SSM_HEADS // 2, LANES, SSM_STATE), F32)
    return pl.pallas_call(
        _make_ssd_kernel(ns, step),
        grid=(nb, 2 * ns),
        in_specs=[prev, pl.BlockSpec((step, XBC_DIM), lambda b, s: (fwd_blk(b, s), 0)), nxt,
                  pl.BlockSpec((step, LANES), lambda b, s: (any_blk(b, s), 0)), out_spec]
                 + [full(a) for a in consts]
                 + [state, state, full(lay["d_skip"]), full(lay["ssm_norm_g"]),
                    pl.BlockSpec(memory_space=pl.ANY)],
        out_specs=[out_spec, state, state],
        out_shape=[jax.ShapeDtypeStruct((t, D_SSM), BF16), state_shape, state_shape],
        scratch_shapes=[pltpu.VMEM((step + 2 * SUBLANES, XBC_DIM), F32),
                        pltpu.VMEM((seq, XBC_DIM), F32),
                        pltpu.VMEM((seq, D_SSM), F32),
                        pltpu.VMEM((SSM_HEADS // 2, LANES, SSM_STATE), F32)],
        input_output_aliases={5 + len(consts) + 4: 0},
        compiler_params=_cparams(("arbitrary", "arbitrary"), 48),
        name="ssd",
    )(xbc, xbc, xbc, dt, z, *consts, h0f, h0b, lay["d_skip"], lay["ssm_norm_g"], prev_out)


def _make_outproj_kernel(n_x, npt, route):
    def kern(*refs):
        oa_ref, ob_ref, oc_ref = refs[:3]
        xs = refs[3:3 + n_x]
        gate_ref, w_ref, g_ref, sh_ref, sc_ref = refs[3 + n_x:8 + n_x]
        rest = refs[8 + n_x:]
        na = ATT_HEADS * HEAD_DIM
        o = (_dot(oa_ref[...], w_ref[0:na, :]) + _dot(ob_ref[...], w_ref[na:na + D_SSM, :])
             + _dot(oc_ref[...], w_ref[na + D_SSM:, :]))
        xn = _pick_x(pl.program_id(0), npt, xs) + gate_ref[...] * o
        h = _norm_mod(xn, g_ref[...], sh_ref[...], sc_ref[...])
        if route:
            r_ref, tri_ref, xo_ref, h_ref, info_ref, tw_ref, cnt_ref = rest
            _route_tile(h, r_ref, tri_ref, info_ref, tw_ref, cnt_ref)
        else:
            xo_ref, h_ref = rest
        xo_ref[...] = xn
        h_ref[...] = h.astype(h_ref.dtype)
    return kern


def _outproj(oa, ob, oc, x, lay, modv, layer, dims, router_pad=None):
    t, d = dims["tokens"], x[0].shape[1]
    npt = dims["prompt_tokens"] // TM
    row_of_tile = dims["row_of_tile"](TM)
    tok = lambda w: pl.BlockSpec((TM, w), lambda i: (i, 0))
    full = lambda a: pl.BlockSpec(a.shape, lambda i: (0,) * a.ndim)
    route = router_pad is not None
    in_specs = [tok(oa.shape[1]), tok(ob.shape[1]), tok(oc.shape[1])] + _x_specs(x, TM, npt) + [
        _mod_spec(layer, 2, d, row_of_tile), full(lay["w_out"]), full(lay["norm2_g"]),
        _mod_spec(layer, 3, d, row_of_tile), _mod_spec(layer, 4, d, row_of_tile)]
    args = [oa, ob, oc, *x, modv, lay["w_out"], lay["norm2_g"], modv, modv]
    out_specs = [tok(d), tok(d)]
    out_shape = [jax.ShapeDtypeStruct((t, d), F32), jax.ShapeDtypeStruct((t, d), BF16)]
    if route:
        in_specs += [full(router_pad), full(dims["tri_strict"])]
        args += [router_pad, dims["tri_strict"]]
        out_specs += [tok(LANES), tok(LANES), pl.BlockSpec((None, 1, LANES), lambda i: (i, 0, 0))]
        out_shape += [jax.ShapeDtypeStruct((t, LANES), jnp.int32), jax.ShapeDtypeStruct((t, LANES), F32),
                      jax.ShapeDtypeStruct((t // TM, 1, LANES), F32)]
    return pl.pallas_call(
        _make_outproj_kernel(len(x), npt, route),
        grid=(t // TM,),
        in_specs=in_specs,
        out_specs=out_specs,
        out_shape=out_shape,
        compiler_params=_cparams(("arbitrary",), 48),
        name="outproj_route" if route else "outproj",
    )(*args)


def _swiglu_tile(x, wg_ref, wu_ref, wd_ref):
    f = wg_ref.shape[1]
    part = None
    for c0 in range(0, f, FFN_SLICE):
        c1 = min(f, c0 + FFN_SLICE)
        g = _dot(x, wg_ref[:, c0:c1].astype(BF16))
        u = _dot(x, wu_ref[:, c0:c1].astype(BF16))
        d = _dot((_silu(g) * u).astype(BF16), wd_ref[c0:c1, :].astype(BF16))
        part = d if part is None else part + d
    return part


def _ffn_kernel(h_ref, wg_ref, wu_ref, wd_ref, x_ref, gate_ref, o_ref):
    o_ref[...] = x_ref[...] + gate_ref[...] * _swiglu_tile(h_ref[...], wg_ref, wu_ref, wd_ref)


def _dense_ffn(h, x, wg, wu, wd, modv, layer, dims):
    t, d = x.shape
    f = wg.shape[1]
    row_of_tile = dims["row_of_tile"](TM)
    gate_base = (layer * 6 + 5) * MOD_ROWS
    resident = lambda shape: pl.BlockSpec(shape, lambda i: (0, 0), pipeline_mode=pl.Buffered(1))
    return pl.pallas_call(
        _ffn_kernel,
        grid=(t // TM,),
        in_specs=[pl.BlockSpec((TM, d), lambda i: (i, 0)),
                  resident((d, f)), resident((d, f)), resident((f, d)),
                  pl.BlockSpec((TM, d), lambda i: (i, 0)),
                  pl.BlockSpec((None, 1, d), lambda i: (gate_base + row_of_tile(i), 0, 0))],
        out_specs=pl.BlockSpec((TM, d), lambda i: (i, 0)),
        out_shape=jax.ShapeDtypeStruct((t, d), F32),
        compiler_params=_cparams(("arbitrary",), 56),
        name="dense_ffn",
    )(h, wg, wu, wd, x, modv)


def _route_tile(h, r_ref, tri_ref, info_ref, w_ref, cnt_ref):
    h_hi, h_lo = _split_bf16(h, 2)
    r_hi, r_lo = _split_bf16(r_ref[...], 2)
    logits = _dot(h_hi, r_hi) + (_dot(h_lo, r_hi) + _dot(h_hi, r_lo))
    tm = logits.shape[0]
    lane = lax.broadcasted_iota(jnp.int32, (tm, LANES), 1)
    logits = jnp.where(lane < N_EXPERTS, logits, NEG_BIG)
    l1 = jnp.max(logits, axis=-1, keepdims=True)
    e1 = jnp.min(jnp.where(logits == l1, lane, LANES), axis=-1, keepdims=True)
    rest = jnp.where(lane == e1, NEG_BIG, logits)
    l2 = jnp.max(rest, axis=-1, keepdims=True)
    e2 = jnp.min(jnp.where(rest == l2, lane, LANES), axis=-1, keepdims=True)
    ex = jnp.exp(l2 - l1)
    w1 = 1.0 / (1.0 + ex)
    w2 = ex / (1.0 + ex)
    member = jnp.where((lane == e1) | (lane == e2), 1.0, 0.0)
    cnt = jnp.sum(member, axis=0, keepdims=True)
    run = jnp.floor((cnt + (SUBLANES - 1)) * (1.0 / SUBLANES)) * SUBLANES
    run_start = _dot_nt(jnp.broadcast_to(run, (SUBLANES, LANES)).astype(BF16),
                        tri_ref[0:LANES, 0:LANES])[0:1, :]
    before = _dot(tri_ref[...], member.astype(BF16)) + run_start
    r1 = jnp.sum(jnp.where(lane == e1, before, 0.0), axis=-1, keepdims=True)
    r2 = jnp.sum(jnp.where(lane == e2, before, 0.0), axis=-1, keepdims=True)
    info = jnp.where(lane == 0, e1.astype(F32),
                     jnp.where(lane == 1, e2.astype(F32),
                               jnp.where(lane == 2, r1, jnp.where(lane == 3, r2, 0.0))))
    info_ref[...] = info.astype(jnp.int32)
    w_ref[...] = jnp.where(lane == 0, w1, jnp.where(lane == 1, w2, 0.0))
    cnt_ref[...] = cnt


def _group_copies(src, src_row, dst, dst_row, n_groups, sem, wait=False, same_src=False):
    def body(g, carry):
        src_g = 0 if (same_src or wait) else g
        dst_g = 0 if wait else g
        copy = pltpu.make_async_copy(
            src.at[pl.ds(pl.multiple_of(src_row + src_g * SUBLANES, SUBLANES), SUBLANES), :],
            dst.at[pl.ds(pl.multiple_of(dst_row + dst_g * SUBLANES, SUBLANES), SUBLANES), :], sem)
        if wait:
            copy.wait()
        else:
            copy.start()
        return carry

    lax.fori_loop(0, n_groups, body, 0)


def _dispatch_kernel(seg_ref, off_ref, ng_ref, gap_ref, pos_ref, h_ref, xb_ref,
                     sorted_ref, zero_ref, sem, zsem):
    s = pl.program_id(0)
    rows, tm = sorted_ref.shape[0], h_ref.shape[0]
    slot = lax.broadcasted_iota(jnp.int32, (rows, tm), 0)
    place = jnp.where(slot == pos_ref[0:1, :], 1.0, jnp.where(slot == pos_ref[1:2, :], 1.0, 0.0))
    sorted_ref[...] = _dot(place.astype(BF16), h_ref[...].astype(BF16))

    def runs(wait):
        for e in range(N_EXPERTS):
            k = s * N_EXPERTS + e
            _group_copies(sorted_ref, 0 if wait else off_ref[k], xb_ref, 0 if wait else seg_ref[k],
                          ng_ref[k], sem, wait)

    runs(False)

    @pl.when(s == 0)
    def _():
        zero_ref[...] = jnp.zeros_like(zero_ref)
        for wait in (False, True):
            for g in range(N_EXPERTS + 1):
                _group_copies(zero_ref, 0, xb_ref, 0 if wait else gap_ref[2 * g], gap_ref[2 * g + 1],
                              zsem, wait, same_src=True)

    runs(True)


def _dispatch(h, pos_rows, seg, off, ngroups, gaps, n_rows):
    t, d = h.shape
    rows = TM * TOP_K + N_EXPERTS * SUBLANES
    return pl.pallas_call(
        _dispatch_kernel,
        grid_spec=pltpu.PrefetchScalarGridSpec(
            num_scalar_prefetch=4,
            grid=(t // TM,),
            in_specs=[pl.BlockSpec((None, TOP_K, TM), lambda i, *_: (i, 0, 0)),
                      pl.BlockSpec((TM, d), lambda i, *_: (i, 0))],
            out_specs=pl.BlockSpec(memory_space=pl.ANY),
            scratch_shapes=[pltpu.VMEM((rows, d), F32), pltpu.VMEM((SUBLANES, d), F32),
                            pltpu.SemaphoreType.DMA(()), pltpu.SemaphoreType.DMA(())]),
        out_shape=jax.ShapeDtypeStruct((n_rows, d), F32),
        compiler_params=_cparams(("arbitrary",), 48),
        name="moe_dispatch",
    )(seg, off, ngroups, gaps, pos_rows, h)


def _expert_kernel(be_ref, bv_ref, x_ref, wg_ref, wu_ref, wd_ref, y_ref, xb_scr):
    i = pl.program_id(0)
    j = pl.program_id(1)

    @pl.when(bv_ref[i] > 0)
    def _():
        @pl.when(j == 0)
        def _():
            xb_scr[...] = x_ref[...].astype(BF16)
            y_ref[...] = jnp.zeros_like(y_ref)

        y_ref[...] += _swiglu_tile(xb_scr[...], wg_ref, wu_ref, wd_ref)

    @pl.when((bv_ref[i] == 0) & (j == 0))
    def _():
        y_ref[...] = jnp.zeros_like(y_ref)


def _experts(xb, block_e, block_v, wg, wu, wd):
    n_rows, d = xb.shape
    f = wg.shape[2]

    def jj(i, j, bv):
        return jnp.where(bv[i] > 0, j, 0)

    return pl.pallas_call(
        _expert_kernel,
        grid_spec=pltpu.PrefetchScalarGridSpec(
            num_scalar_prefetch=2,
            grid=(n_rows // MOE_BM, f // MOE_TF),
            in_specs=[pl.BlockSpec((MOE_BM, d), lambda i, j, be, bv: (i, 0)),
                      pl.BlockSpec((None, d, MOE_TF), lambda i, j, be, bv: (be[i], 0, jj(i, j, bv))),
                      pl.BlockSpec((None, d, MOE_TF), lambda i, j, be, bv: (be[i], 0, jj(i, j, bv))),
                      pl.BlockSpec((None, MOE_TF, d), lambda i, j, be, bv: (be[i], jj(i, j, bv), 0))],
            out_specs=pl.BlockSpec((MOE_BM, d), lambda i, j, be, bv: (i, 0)),
            scratch_shapes=[pltpu.VMEM((MOE_BM, d), BF16)]),
        out_shape=jax.ShapeDtypeStruct((n_rows, d), F32),
        compiler_params=_cparams(("arbitrary", "arbitrary"), 56),
        name="moe_experts",
    )(block_e, block_v, xb, wg, wu, wd)


def _make_combine_kernel(npt):
    def kern(seg_ref, off_ref, ng_ref, yb_ref, pos_ref, w_ref, x_ref, gate_ref, fg_ref,
             op_ref, os_ref, runs_ref, sems):
        i = pl.program_id(0)
        n = pl.num_programs(0)
        rows = runs_ref.shape[1]
        tm = x_ref.shape[0]

        def fetch(tile, slot, wait=False):
            for e in range(N_EXPERTS):
                k = tile * N_EXPERTS + e
                _group_copies(yb_ref, 0 if wait else seg_ref[k], runs_ref.at[slot],
                              0 if wait else off_ref[k], ng_ref[k], sems.at[slot], wait)

        @pl.when(i == 0)
        def _():
            runs_ref[...] = jnp.zeros_like(runs_ref)
            fetch(0, 0)

        @pl.when(i + 1 < n)
        def _():
            fetch(i + 1, (i + 1) % 2)

        slot = i % 2
        fetch(i, slot, wait=True)
        y_runs = runs_ref[slot].astype(BF16)
        col = lax.broadcasted_iota(jnp.int32, (tm, rows), 1)
        pos = pos_ref[...]
        picked = [_dot(jnp.where(col == pos[:, k:k + 1], 1.0, 0.0).astype(BF16), y_runs)
                  for k in range(TOP_K)]
        w = w_ref[...]
        y = w[:, 0:1] * picked[0] + w[:, 1:2] * picked[1]
        xn = x_ref[...] + gate_ref[...] * y
        ms = jnp.mean(xn * xn, axis=-1, keepdims=True)
        out = xn * lax.rsqrt(ms + NORM_EPS) * fg_ref[...]

        @pl.when(i < npt)
        def _():
            op_ref[...] = out

        @pl.when(i >= npt)
        def _():
            os_ref[...] = out
    return kern


def _combine(yb, pos, seg, off, ngroups, top_w, x, modv, layer, final_g, dims):
    t, d = x.shape
    tp = dims["prompt_tokens"]
    npt = tp // TM
    rows = TM * TOP_K + N_EXPERTS * SUBLANES
    row_of_tile = dims["row_of_tile"](TM)
    gate_base = (layer * 6 + 5) * MOD_ROWS
    return pl.pallas_call(
        _make_combine_kernel(npt),
        grid_spec=pltpu.PrefetchScalarGridSpec(
            num_scalar_prefetch=3,
            grid=(t // TM,),
            in_specs=[pl.BlockSpec(memory_space=pl.ANY),
                      pl.BlockSpec((TM, TOP_K), lambda i, *_: (i, 0)),
                      pl.BlockSpec((TM, LANES), lambda i, *_: (i, 0)),
                      pl.BlockSpec((TM, d), lambda i, *_: (i, 0)),
                      pl.BlockSpec((None, 1, d), lambda i, *_: (gate_base + row_of_tile(i), 0, 0)),
                      pl.BlockSpec((1, d), lambda i, *_: (0, 0))],
            out_specs=[pl.BlockSpec((TM, d), lambda i, *_: (jnp.minimum(i, npt - 1), 0)),
                       pl.BlockSpec((TM, d), lambda i, *_: (jnp.maximum(i - npt, 0), 0))],
            scratch_shapes=[pltpu.VMEM((2, rows, d), F32), pltpu.SemaphoreType.DMA((2,))]),
        out_shape=[jax.ShapeDtypeStruct((tp, d), F32), jax.ShapeDtypeStruct((t - tp, d), F32)],
        compiler_params=_cparams(("arbitrary",), 56),
        name="moe_combine",
    )(seg, off, ngroups, yb, pos, top_w, x, modv, final_g)


def _moe_layer(h, x, routing, wg, wu, wd, modv, layer, final_g, dims):
    t, d = x.shape
    n_tiles = t // TM
    info, top_w, counts = routing
    run = (counts[:, 0, :N_EXPERTS].astype(jnp.int32) + SUBLANES - 1) // SUBLANES * SUBLANES
    off = jnp.cumsum(run, axis=1) - run
    before = jnp.cumsum(run, axis=0) - run
    total = jnp.sum(run, axis=0)
    padded = (total + MOE_BM - 1) // MOE_BM * MOE_BM
    pad_end = jnp.cumsum(padded)
    pad_start = pad_end - padded
    seg = pad_start[None, :] + before
    n_rows = (t * TOP_K + n_tiles * N_EXPERTS * (SUBLANES - 1) + MOE_BM - 1) // MOE_BM * MOE_BM \
        + N_EXPERTS * MOE_BM
    pos = info[:, TOP_K:2 * TOP_K]
    pos_rows = pos.reshape(n_tiles, TM, TOP_K).transpose(0, 2, 1)
    flat = lambda a: a.reshape(-1).astype(jnp.int32)
    gap_lo = jnp.concatenate([pad_start + total, pad_end[-1:]])
    gap_hi = jnp.concatenate([pad_end, jnp.array([n_rows], jnp.int32)])
    gaps = jnp.stack([gap_lo, (gap_hi - gap_lo) // SUBLANES], axis=1)
    block_start = jnp.arange(n_rows // MOE_BM, dtype=jnp.int32) * MOE_BM
    block_e = jnp.sum(pad_end[None, :] <= block_start[:, None], axis=1).astype(jnp.int32)
    block_v = (block_e < N_EXPERTS).astype(jnp.int32)
    last_e = jnp.max(jnp.where(padded > 0, jnp.arange(N_EXPERTS), 0)).astype(jnp.int32)
    block_e = jnp.where(block_v > 0, block_e, last_e)
    xb = _dispatch(h, pos_rows, flat(seg), flat(off), flat(run // SUBLANES), flat(gaps), n_rows)
    yb = _experts(xb, block_e, block_v, wg, wu, wd)
    return _combine(yb, pos, flat(seg), flat(off), flat(run // SUBLANES), top_w, x, modv, layer,
                    final_g, dims)


def _rope_tables(dec_seq):
    n_rows = dec_seq // GRID_W
    row = jnp.repeat(jnp.arange(n_rows, dtype=F32), GRID_W)
    col = (jnp.arange(dec_seq) % GRID_W).astype(F32)
    nf = HEAD_DIM // 4
    inv = ROPE_THETA ** (-jnp.arange(nf, dtype=F32) / nf)
    ang = jnp.stack([row[:, None] * inv, col[:, None] * inv], axis=1)
    cos = jnp.cos(ang)
    sin = jnp.sin(ang)
    cos_h = jnp.concatenate([cos, cos], axis=-1).reshape(dec_seq, HEAD_DIM)
    sin_h = jnp.concatenate([-sin, sin], axis=-1).reshape(dec_seq, HEAD_DIM)
    cos_t = jnp.concatenate([cos_h, cos_h], axis=-1)
    sin_t = jnp.concatenate([sin_h, sin_h], axis=-1)
    cos_t = jnp.concatenate([cos_t, jnp.ones((TM, LANES), F32)], axis=0)
    sin_t = jnp.concatenate([sin_t, jnp.zeros((TM, LANES), F32)], axis=0)
    return cos_t, sin_t


def _expansion(direction, width):
    src = direction * SSM_HEADS + np.arange(SSM_HEADS * width) // width
    return jnp.asarray((np.arange(LANES)[:, None] == src[None, :]).astype(np.float32)).astype(BF16)


def _layer_params(l, norm1_g, norm2_g, w_in, q_norm_g, k_norm_g, conv_w, conv_b, a_log, dt_bias,
                  d_skip, ssm_norm_g, sink, w_out):
    d = w_in.shape[1]
    pad = lambda v: jnp.zeros((1, LANES), F32).at[0, :v.size].set(v.reshape(-1))
    seg = np.arange(LANES) // HEAD_DIM
    bd = jnp.asarray((seg[:, None] == seg[None, :]).astype(np.float32) / HEAD_DIM).astype(BF16)
    r = np.arange(SSM_CHUNK)
    return {
        "norm1_g": norm1_g[l].reshape(1, d), "norm2_g": norm2_g[l].reshape(1, d),
        "w_main": w_in[l, :, :OFF_DT].astype(BF16),
        "w_dt": jnp.zeros((d, LANES), F32).at[:, :2 * SSM_HEADS].set(w_in[l, :, OFF_DT:]).astype(BF16),
        "bd": bd,
        "q_norm_g": jnp.tile(q_norm_g[l], 2).reshape(1, LANES),
        "k_norm_g": jnp.tile(k_norm_g[l], 2).reshape(1, LANES),
        "conv_w": jnp.zeros((SUBLANES, XBC_DIM), F32).at[:D_CONV].set(conv_w[l]),
        "conv_b": conv_b[l].reshape(1, XBC_DIM),
        "a_log": pad(a_log[l]), "dt_bias": pad(dt_bias[l]),
        "d_skip": jnp.repeat(d_skip[l], SSM_HEAD_DIM).reshape(1, D_SSM),
        "ssm_norm_g": ssm_norm_g[l].reshape(1, D_SSM),
        "sink": sink[l],
        "w_out": w_out[l].astype(BF16),
        "tri_fwd": jnp.asarray((r[:, None] >= r[None, :]).astype(np.float32)).astype(BF16),
        "tri_bwd": jnp.asarray((r[:, None] <= r[None, :]).astype(np.float32)).astype(BF16),
        "wide_fwd": _expansion(0, LANES), "wide_bwd": _expansion(1, LANES),
        "pair_fwd": _expansion(0, SSM_HEAD_DIM), "pair_bwd": _expansion(1, SSM_HEAD_DIM),
    }


def kernel(x_prompt, x_sample, cache_attn_k, cache_attn_v, cache_win_k, cache_win_v, state_ssm_fwd, state_ssm_bwd, c, c_ctx, norm1_g, norm2_g, w_mod, b_mod, w_in, q_norm_g, k_norm_g, conv_w, conv_b, a_log, dt_bias, d_skip, ssm_norm_g, sink, w_out, ffn_w_gate, ffn_w_up, ffn_w_down, moe_router, moe_w_gate, moe_w_up, moe_w_down, final_g):
    batch, seq, d = x_prompt.shape
    dec_batch, dec_seq, _ = x_sample.shape
    depth = w_in.shape[0]
    past = cache_attn_k.shape[2]
    tp = batch * seq
    t = tp + dec_batch * dec_seq
    assert depth == 2 and tp % (2 * TM) == 0 and dec_seq % (2 * TM) == 0
    assert 1 + dec_batch <= MOD_ROWS and seq % SSM_CHUNK == 0 and tp % dec_seq == 0

    def row_of_tile(tm):
        npt, tpb = tp // tm, dec_seq // tm
        return lambda i: jnp.where(i < npt, 0, 1 + (i - npt) // tpb)

    cos_t, sin_t = _rope_tables(dec_seq)
    rr = np.arange(TM)
    dims = {"tokens": t, "prompt_tokens": tp, "dec_seq": dec_seq, "row_of_tile": row_of_tile,
            "rope_cos": cos_t, "rope_sin": sin_t,
            "tri_strict": jnp.asarray((rr[:, None] > rr[None, :]).astype(np.float32)).astype(BF16)}

    cc = jnp.zeros((MOD_ROWS, d), F32).at[0].set(c_ctx).at[1:1 + dec_batch].set(c)
    mod = _mod_vectors(cc, w_mod, b_mod)
    modv = mod.reshape(depth, MOD_ROWS, 6, d).transpose(0, 2, 1, 3).reshape(depth * 6 * MOD_ROWS, 1, d)

    x = (x_prompt.reshape(tp, d), x_sample.reshape(dec_batch * dec_seq, d))
    zero_state = jnp.zeros((batch, SSM_HEADS // 2, LANES, SSM_STATE), F32)
    caches = [[] for _ in range(6)]
    y = None
    for l in range(depth):
        lay = _layer_params(l, norm1_g, norm2_g, w_in, q_norm_g, k_norm_g, conv_w, conv_b, a_log,
                            dt_bias, d_skip, ssm_norm_g, sink, w_out)
        (qa, ka, va, qc, kc, vc, z, xbc, dt, cka, cva, ckc, cvc) = _inproj(x, lay, modv, l, dims)
        ctx = lambda a: a[:, l].reshape(dec_batch, past, LANES).astype(BF16)

        oa = _attention(qa, ka, va, None, None, None, None, nb=batch, seq=seq, tq=seq, row_off=0,
                        window=False, name="attn_a_ctx")
        oa = _attention(qa, ka, va, ctx(cache_attn_k), ctx(cache_attn_v), None, oa, nb=dec_batch,
                        seq=dec_seq, tq=TQ_GLOBAL, row_off=tp, window=False, name="attn_a_lat")
        oc = _attention(qc, kc, vc, None, None, lay["sink"], None, nb=batch, seq=seq, tq=seq,
                        row_off=0, window=False, name="attn_c_ctx")
        oc = _attention(qc, kc, vc, ctx(cache_win_k), ctx(cache_win_v), lay["sink"], oc,
                        nb=dec_batch, seq=dec_seq, tq=TQ_WINDOW, row_off=tp, window=True,
                        name="attn_c_lat")
        pair = lambda s: s[:, l].reshape(dec_batch, SSM_HEADS // 2, LANES, SSM_STATE)
        ob, hf_p, hb_p = _ssd(xbc, dt, z, lay, zero_state, zero_state, None, nb=batch, seq=seq,
                              row_off=0)
        ob, _, _ = _ssd(xbc, dt, z, lay, pair(state_ssm_fwd), pair(state_ssm_bwd), ob, nb=dec_batch,
                        seq=dec_seq, row_off=tp)

        kv = lambda a: a[:tp].reshape(batch, seq, ATT_KV_HEADS, HEAD_DIM)
        st = lambda s: s.reshape(batch, SSM_HEADS, SSM_HEAD_DIM, SSM_STATE)
        for lst, val in zip(caches, (kv(cka), kv(cva), kv(ckc), kv(cvc), st(hf_p), st(hb_p))):
            lst.append(val)

        i = l // 2
        if l % 2 == 0:
            xr, h2 = _outproj(oa, ob, oc, x, lay, modv, l, dims)
            x = (_dense_ffn(h2, xr, ffn_w_gate[i].astype(BF16), ffn_w_up[i].astype(BF16),
                            ffn_w_down[i].astype(BF16), modv, l, dims),)
        else:
            router_pad = jnp.zeros((d, LANES), F32).at[:, :N_EXPERTS].set(moe_router[i])
            xr, h2, info, top_w, counts = _outproj(oa, ob, oc, x, lay, modv, l, dims, router_pad)
            y = _moe_layer(h2, xr, (info, top_w, counts), moe_w_gate[i], moe_w_up[i], moe_w_down[i],
                           modv, l, final_g.reshape(1, d), dims)

    y_prompt = y[0].reshape(batch, seq, d)
    y_sample = y[1].reshape(dec_batch, dec_seq, d)
    return (y_prompt, y_sample) + tuple(jnp.stack(lst, axis=1) for lst in caches)
```
